```python
import math
import jax, jax.numpy as jnp
from jax import lax
import numpy as np

D_MODEL = 2048
BATCH = 4
SEQ = 2048
DEPTH = 4

N_MIXERS = 3
HEAD_DIM = 64
N_HEADS = D_MODEL // HEAD_DIM
BRANCH = N_HEADS * HEAD_DIM
N_KV_A = N_HEADS // 8
KV_A = N_KV_A * HEAD_DIM
WINDOW = 128
BLOCK = 128
NORM_EPS = 1e-6
NEG = -1e30

A_COLS = BRANCH + 2 * KV_A + BRANCH
B_COLS = 4 * BRANCH
C_COLS = 4 * BRANCH + N_HEADS

kernel_name = "hybrid_swa_stickbreak_fox_trunk"


def _n_layers_of(kind):
    return sum(1 for i in range(DEPTH) if i % N_MIXERS == kind)


def _alibi_slopes(n):
    return jnp.asarray(2.0 ** (-8.0 * np.arange(1, n + 1, dtype=np.float32) / n), dtype=jnp.float32)


def rmsnorm(x, g):
    xf = x.astype(jnp.float32)
    r = lax.rsqrt(jnp.mean(xf * xf, axis=-1, keepdims=True) + NORM_EPS)
    return (xf * r * g.astype(jnp.float32)).astype(x.dtype)


def swa_sink_mixer(h, w_in, sinks):
    B, S, _ = h.shape
    nb = S // BLOCK
    G = N_HEADS // N_KV_A
    q, k, v, z = jnp.split(h @ w_in, [BRANCH, BRANCH + KV_A, BRANCH + 2 * KV_A], axis=-1)
    q = q.reshape(B, nb, BLOCK, N_KV_A, G, HEAD_DIM).astype(jnp.float32)
    k = k.reshape(B, nb, BLOCK, N_KV_A, HEAD_DIM).astype(jnp.float32)
    v = v.reshape(B, nb, BLOCK, N_KV_A, HEAD_DIM).astype(jnp.float32)

    def band(a):
        prev = jnp.concatenate([jnp.zeros_like(a[:, :1]), a[:, :-1]], axis=1)
        return jnp.concatenate([prev, a], axis=2)

    kb, vb = band(k), band(v)
    scores = jnp.einsum('bnqhgd,bnkhd->bnhgqk', q, kb) * (HEAD_DIM ** -0.5)
    qi = jnp.arange(BLOCK)[:, None]
    kj = jnp.arange(2 * BLOCK)[None, :]
    dist = (qi + BLOCK - kj).astype(jnp.float32)
    s_pos = jnp.arange(nb)[:, None, None] * BLOCK - BLOCK + kj[None]
    valid = (dist >= 0) & (dist < WINDOW) & (s_pos >= 0)
    slopes = _alibi_slopes(N_HEADS).reshape(N_KV_A, G)
    scores = scores - slopes[:, :, None, None] * dist
    scores = jnp.where(valid[None, :, None, None], scores, NEG)
    sink = sinks.astype(jnp.float32).reshape(N_KV_A, G)
    sink_col = jnp.broadcast_to(sink[None, None, :, :, None, None], scores.shape[:-1] + (1,))
    p = jax.nn.softmax(jnp.concatenate([scores, sink_col], axis=-1), axis=-1)[..., :-1]
    o = jnp.einsum('bnhgqk,bnkhd->bnqhgd', p, vb).reshape(B, S, BRANCH)
    return o, z


def stick_breaking_mixer(h, w_in):
    B, S, _ = h.shape
    nb = S // BLOCK
    q, k, v, z = jnp.split(h @ w_in, 4, axis=-1)
    q = q.reshape(B, nb, BLOCK, N_HEADS, HEAD_DIM).transpose(1, 0, 3, 2, 4).astype(jnp.float32)
    k = k.reshape(B, S, N_HEADS, HEAD_DIM).astype(jnp.float32)
    v = v.reshape(B, S, N_HEADS, HEAD_DIM).astype(jnp.float32)
    s_pos = jnp.arange(S)
    scale = HEAD_DIM ** -0.5

    def block(args):
        qb, n = args
        t_pos = n * BLOCK + jnp.arange(BLOCK)
        logits = jnp.einsum('bhqd,bshd->bhqs', qb, k) * scale
        before = s_pos[None, :] < t_pos[:, None]
        log_fail = jnp.where(before, jax.nn.log_sigmoid(-logits), 0.0)
        incl = lax.cumsum(log_fail, axis=3, reverse=True)
        suffix = jnp.concatenate([incl[..., 1:], jnp.zeros_like(incl[..., :1])], axis=-1)
        a = jnp.where(before, jnp.exp(jax.nn.log_sigmoid(logits) + suffix), 0.0)
        return jnp.einsum('bhqs,bshd->bqhd', a, v)

    o = lax.map(block, (q, jnp.arange(nb)))
    o = o.transpose(1, 0, 2, 3, 4).reshape(B, S, BRANCH)
    return o, z


def forgetting_mixer(h, w_in, b_f):
    B, S, _ = h.shape
    nb = S // BLOCK
    q, k, v, z, f_logit = jnp.split(h @ w_in, [BRANCH, 2 * BRANCH, 3 * BRANCH, 4 * BRANCH], axis=-1)
    log_f = jax.nn.log_sigmoid(f_logit.astype(jnp.float32) + b_f.astype(jnp.float32))
    cum = lax.cumsum(log_f, axis=1).transpose(0, 2, 1)
    q = q.reshape(B, nb, BLOCK, N_HEADS, HEAD_DIM).transpose(1, 0, 3, 2, 4).astype(jnp.float32)
    cq = cum.reshape(B, N_HEADS, nb, BLOCK).transpose(2, 0, 1, 3)
    k = k.reshape(B, S, N_HEADS, HEAD_DIM).astype(jnp.float32)
    v = v.reshape(B, S, N_HEADS, HEAD_DIM).astype(jnp.float32)
    s_pos = jnp.arange(S)
    scale = HEAD_DIM ** -0.5

    def block(args):
        qb, cqb, n = args
        t_pos = n * BLOCK + jnp.arange(BLOCK)
        logits = jnp.einsum('bhqd,bshd->bhqs', qb, k) * scale + cqb[..., :, None] - cum[:, :, None, :]
        causal = s_pos[None, :] <= t_pos[:, None]
        p = jax.nn.softmax(jnp.where(causal, logits, NEG), axis=-1)
        return jnp.einsum('bhqs,bshd->bqhd', p, v)

    o = lax.map(block, (q, cq, jnp.arange(nb)))
    o = o.transpose(1, 0, 2, 3, 4).reshape(B, S, BRANCH)
    return o, z


def setup_inputs(seed: int = 0) -> dict:
    key = jax.random.key(seed)
    ks = jax.random.split(key, 12)
    n_a, n_b, n_c = _n_layers_of(0), _n_layers_of(1), _n_layers_of(2)
    f32 = jnp.float32
    nrm = jax.random.normal
    return {
        "x": nrm(ks[0], (BATCH, SEQ, D_MODEL), f32),
        "g_pre": 1.0 + 0.02 * nrm(ks[1], (DEPTH, D_MODEL), f32),
        "g_post": 1.0 + 0.02 * nrm(ks[2], (DEPTH, D_MODEL), f32),
        "w_in_a": nrm(ks[3], (n_a, D_MODEL, A_COLS), f32) * D_MODEL ** -0.5,
        "w_out_a": nrm(ks[4], (n_a, BRANCH, D_MODEL), f32) * BRANCH ** -0.5,
        "sinks_a": 0.5 * nrm(ks[5], (n_a, N_HEADS), f32),
        "w_in_b": nrm(ks[6], (n_b, D_MODEL, B_COLS), f32) * D_MODEL ** -0.5,
        "w_out_b": nrm(ks[7], (n_b, BRANCH, D_MODEL), f32) * BRANCH ** -0.5,
        "w_in_c": nrm(ks[8], (n_c, D_MODEL, C_COLS), f32) * D_MODEL ** -0.5,
        "b_f_c": 1.0 + 5.0 * jax.random.uniform(ks[9], (n_c, N_HEADS), f32),
        "w_out_c": nrm(ks[10], (n_c, BRANCH, D_MODEL), f32) * BRANCH ** -0.5,
    }


def reference(x, g_pre, g_post, w_in_a, w_out_a, sinks_a, w_in_b, w_out_b, w_in_c, b_f_c, w_out_c):
    for i in range(DEPTH):
        kind, j = i % N_MIXERS, i // N_MIXERS
        h = rmsnorm(x, g_pre[i])
        if kind == 0:
            o, z = swa_sink_mixer(h, w_in_a[j], sinks_a[j])
            w_out = w_out_a[j]
        elif kind == 1:
            o, z = stick_breaking_mixer(h, w_in_b[j])
            w_out = w_out_b[j]
        else:
            o, z = forgetting_mixer(h, w_in_c[j], b_f_c[j])
            w_out = w_out_c[j]
        y = (o.astype(z.dtype) * jax.nn.silu(z)) @ w_out
        x = x + rmsnorm(y, g_post[i])
    return x
```

```python
import functools

import jax
import jax.numpy as jnp
import numpy as np
from jax import lax
from jax.experimental import pallas as pl
from jax.experimental.pallas import tpu as pltpu

D_MODEL = 2048
HEAD_DIM = 64
N_HEADS = 32
BRANCH = N_HEADS * HEAD_DIM
N_KV_A = 4
KV_A = N_KV_A * HEAD_DIM
GROUP_A = N_HEADS // N_KV_A
BLOCK = 128
LANES = 128
NORM_EPS = 1e-6
NEG = -1e30
SCALE = HEAD_DIM ** -0.5
N_PAIRS = N_HEADS // 2

F32 = jnp.float32
BF16 = jnp.bfloat16

VMEM_LIMIT = 52 * 1024 * 1024

_NT = (((1,), (1,)), ((), ()))


def _params(sem):
    return pltpu.CompilerParams(dimension_semantics=sem, vmem_limit_bytes=VMEM_LIMIT)


def _norm_proj_kernel(x_ref, g_ref, w_ref, o_ref, h_ref):
    @pl.when(pl.program_id(1) == 0)
    def _():
        x = x_ref[...]
        r = lax.rsqrt(jnp.mean(x * x, axis=-1, keepdims=True) + NORM_EPS)
        h_ref[...] = (x * r * g_ref[...]).astype(BF16)

    o_ref[...] = jnp.dot(h_ref[...], w_ref[...], preferred_element_type=F32).astype(o_ref.dtype)


def _norm_proj(x, g, w, tn, out_dtype, tm=1024):
    m, d = x.shape
    n = w.shape[1]
    return pl.pallas_call(
        _norm_proj_kernel,
        grid=(m // tm, n // tn),
        in_specs=[
            pl.BlockSpec((tm, d), lambda i, j: (i, 0)),
            pl.BlockSpec((1, d), lambda i, j: (0, 0)),
            pl.BlockSpec((d, tn), lambda i, j: (0, j)),
        ],
        out_specs=pl.BlockSpec((tm, tn), lambda i, j: (i, j)),
        out_shape=jax.ShapeDtypeStruct((m, n), out_dtype),
        scratch_shapes=[pltpu.VMEM((tm, d), BF16)],
        compiler_params=_params(("parallel", "arbitrary")),
        name="norm_proj",
    )(x, g.reshape(1, d), w)


def _gate_out_kernel(o_ref, z_ref, w_ref, g_ref, x_ref, out_ref):
    z = z_ref[...].astype(F32)
    gated = (o_ref[...].astype(F32) * (z * jax.nn.sigmoid(z))).astype(BF16)
    y = jnp.dot(gated, w_ref[...], preferred_element_type=F32)
    r = lax.rsqrt(jnp.mean(y * y, axis=-1, keepdims=True) + NORM_EPS)
    out_ref[...] = x_ref[...] + y * r * g_ref[...]


def _gate_out(o, proj, z_block, w_out, g, x, tm=256):
    m, d = x.shape
    return pl.pallas_call(
        _gate_out_kernel,
        grid=(m // tm,),
        in_specs=[
            pl.BlockSpec((tm, BRANCH), lambda i: (i, 0)),
            pl.BlockSpec((tm, BRANCH), lambda i: (i, z_block)),
            pl.BlockSpec((BRANCH, d), lambda i: (0, 0)),
            pl.BlockSpec((1, d), lambda i: (0, 0)),
            pl.BlockSpec((tm, d), lambda i: (i, 0)),
        ],
        out_specs=pl.BlockSpec((tm, d), lambda i: (i, 0)),
        out_shape=jax.ShapeDtypeStruct((m, d), F32),
        compiler_params=_params(("parallel",)),
        name="gate_out",
    )(o, proj, w_out, g.reshape(1, d), x)


def _stack_pair(q):
    lane = lax.broadcasted_iota(jnp.int32, q.shape, 1)
    zero = jnp.zeros_like(q)
    return jnp.concatenate([jnp.where(lane < HEAD_DIM, q, zero),
                            jnp.where(lane >= HEAD_DIM, q, zero)], axis=0)


def _merge_pair(acc, t):
    lane = lax.broadcasted_iota(jnp.int32, (t, LANES), 1)
    return jnp.where(lane < HEAD_DIM, acc[:t], acc[t:])


def _alibi_slopes():
    n = N_HEADS
    return (2.0 ** (-8.0 * np.arange(1, n + 1, dtype=np.float32) / n)).astype(np.float32)


def _swa_kernel(sink_ref, q_ref, kp_ref, kc_ref, vp_ref, vc_ref, o_ref, bias_ref, *, nb):
    first = (pl.program_id(0) == 0) & (pl.program_id(1) == 0)

    @pl.when(first)
    def _():
        qi = lax.broadcasted_iota(jnp.int32, (BLOCK, 2 * BLOCK), 0)
        kj = lax.broadcasted_iota(jnp.int32, (BLOCK, 2 * BLOCK), 1)
        dist = qi + BLOCK - kj
        valid = (dist >= 0) & (dist < BLOCK)
        distf = dist.astype(F32)
        slopes = _alibi_slopes()
        for h in range(N_HEADS):
            bias_ref[h] = jnp.where(valid, -float(slopes[h]) * distf, NEG)

    n = pl.program_id(1)
    lane2 = lax.broadcasted_iota(jnp.int32, (1, 2 * BLOCK), 1)
    pen = jnp.where((lane2 < BLOCK) & (n == 0), NEG, 0.0).astype(F32)

    lane = lax.broadcasted_iota(jnp.int32, (2 * BLOCK, LANES), 1)
    low = lane < HEAD_DIM
    low_q = lax.broadcasted_iota(jnp.int32, (BLOCK, LANES), 1) < HEAD_DIM
    for g in range(N_KV_A):
        p, half = divmod(g, 2)
        cols = slice(p * LANES, (p + 1) * LANES)
        k2 = jnp.concatenate([kp_ref[:, cols], kc_ref[:, cols]], axis=0).astype(F32)
        v2 = jnp.concatenate([vp_ref[:, cols], vc_ref[:, cols]], axis=0).astype(F32)
        k2r = pltpu.roll(k2, HEAD_DIM, 1)
        v2r = pltpu.roll(v2, HEAD_DIM, 1)
        own_low = low if half == 0 else jnp.logical_not(low)
        kd = jnp.where(own_low, k2, k2r).astype(BF16)
        vd = jnp.where(own_low, v2, v2r).astype(BF16)

        q = q_ref[:, g * GROUP_A * HEAD_DIM:(g + 1) * GROUP_A * HEAD_DIM] * SCALE
        rows = [_stack_pair(q[:, i * LANES:(i + 1) * LANES]) for i in range(GROUP_A // 2)]
        lhs = jnp.concatenate(rows, axis=0)
        s = lax.dot_general(lhs, kd, _NT, preferred_element_type=F32)
        ps, ls = [], []
        for hh in range(GROUP_A):
            h = g * GROUP_A + hh
            sh = s[hh * BLOCK:(hh + 1) * BLOCK] + bias_ref[h] + pen
            sink = sink_ref[h]
            m = jnp.maximum(jnp.max(sh, axis=-1, keepdims=True), sink)
            e = jnp.exp(sh - m)
            ls.append(jnp.sum(e, axis=-1, keepdims=True) + jnp.exp(sink - m))
            ps.append(e.astype(BF16))
        pv = jnp.dot(jnp.concatenate(ps, axis=0), vd, preferred_element_type=F32)
        for i in range(GROUP_A // 2):
            o0 = pv[(2 * i) * BLOCK:(2 * i + 1) * BLOCK] / ls[2 * i]
            o1 = pv[(2 * i + 1) * BLOCK:(2 * i + 2) * BLOCK] / ls[2 * i + 1]
            c0 = (g * GROUP_A // 2 + i) * LANES
            o_ref[:, c0:c0 + LANES] = jnp.where(low_q, o0, o1).astype(o_ref.dtype)


def _swa_attention(proj, sinks, batch, seq):
    m = proj.shape[0]
    nb = seq // BLOCK
    kcol = (2 * BRANCH) // KV_A
    vcol = kcol + 1

    def cur(b, n):
        return b * nb + n

    def prev(b, n):
        return b * nb + jnp.maximum(n - 1, 0)

    return pl.pallas_call(
        functools.partial(_swa_kernel, nb=nb),
        grid=(batch, nb),
        in_specs=[
            pl.BlockSpec(memory_space=pltpu.SMEM),
            pl.BlockSpec((BLOCK, BRANCH), lambda b, n: (cur(b, n), 0)),
            pl.BlockSpec((BLOCK, KV_A), lambda b, n: (prev(b, n), kcol)),
            pl.BlockSpec((BLOCK, KV_A), lambda b, n: (cur(b, n), kcol)),
            pl.BlockSpec((BLOCK, KV_A), lambda b, n: (prev(b, n), vcol)),
            pl.BlockSpec((BLOCK, KV_A), lambda b, n: (cur(b, n), vcol)),
        ],
        out_specs=pl.BlockSpec((BLOCK, BRANCH), lambda b, n: (cur(b, n), 0)),
        out_shape=jax.ShapeDtypeStruct((m, BRANCH), BF16),
        scratch_shapes=[pltpu.VMEM((N_HEADS, BLOCK, 2 * BLOCK), F32)],
        compiler_params=_params(("arbitrary", "arbitrary")),
        name="swa_attention",
    )(sinks, proj, proj, proj, proj, proj)


def _suffix_matrix():
    j = lax.broadcasted_iota(jnp.int32, (2 * BLOCK, 2 * BLOCK), 0) % BLOCK
    s = lax.broadcasted_iota(jnp.int32, (2 * BLOCK, 2 * BLOCK), 1)
    return jnp.where((s >= BLOCK) | (j > s), 1.0, 0.0).astype(BF16)


def _softplus(s):
    return jnp.maximum(s, 0.0) + jnp.log(1.0 + jnp.exp(-jnp.abs(s)))


def _stick_kernel(q_ref, k_ref, v_ref, o_ref, *, tq):
    i = pl.program_id(2)
    lhs = _stack_pair(q_ref[...] * SCALE)
    u2 = _suffix_matrix()

    def tile(j, carry, acc, mask):
        kb = k_ref[pl.ds(pl.multiple_of(j * BLOCK, BLOCK), BLOCK), :]
        vb = v_ref[pl.ds(pl.multiple_of(j * BLOCK, BLOCK), BLOCK), :]
        s = lax.dot_general(lhs, kb, _NT, preferred_element_type=F32)
        sp = _softplus(s)
        if mask is not None:
            sp = jnp.where(mask, sp, 0.0)
        hi = sp.astype(BF16)
        lo = (sp - hi.astype(F32)).astype(BF16)
        cs = jnp.dot(jnp.concatenate([hi, lo], axis=1), u2, preferred_element_type=F32)
        p = jnp.exp(s - sp - cs[:, :BLOCK] - carry)
        if mask is not None:
            p = jnp.where(mask, p, 0.0)
        acc = acc + jnp.dot(p.astype(BF16), vb, preferred_element_type=F32)
        return carry + cs[:, BLOCK:], acc

    row = lax.broadcasted_iota(jnp.int32, (tq, BLOCK), 0)
    col = lax.broadcasted_iota(jnp.int32, (tq, BLOCK), 1)
    carry = jnp.zeros((2 * tq, BLOCK), F32)
    acc = jnp.zeros((2 * tq, LANES), F32)
    nd = tq // BLOCK
    for d in reversed(range(nd)):
        before = col + d * BLOCK < row
        carry, acc = tile(i * nd + d, carry, acc, jnp.concatenate([before, before], axis=0))

    def body(t, c):
        j = i * nd - 1 - t
        return tile(j, c[0], c[1], None)

    carry, acc = lax.fori_loop(0, i * nd, body, (carry, acc))
    o_ref[...] = _merge_pair(acc, tq).astype(o_ref.dtype)


def _stick_attention(proj, batch, seq, tq=128):
    m = proj.shape[0]
    nq = seq // tq
    return pl.pallas_call(
        functools.partial(_stick_kernel, tq=tq),
        grid=(batch, N_PAIRS, nq),
        in_specs=[
            pl.BlockSpec((tq, LANES), lambda b, p, i: (b * nq + i, p)),
            pl.BlockSpec((seq, LANES), lambda b, p, i: (b, N_PAIRS + p)),
            pl.BlockSpec((seq, LANES), lambda b, p, i: (b, 2 * N_PAIRS + p)),
        ],
        out_specs=pl.BlockSpec((tq, LANES), lambda b, p, i: (b * nq + i, p)),
        out_shape=jax.ShapeDtypeStruct((m, BRANCH), BF16),
        compiler_params=_params(("parallel", "parallel", "arbitrary")),
        name="stick_attention",
    )(proj, proj, proj)


def _split3(x):
    hi = x.astype(BF16)
    r = x - hi.astype(F32)
    mid = r.astype(BF16)
    lo = (r - mid.astype(F32)).astype(BF16)
    return hi, mid, lo


def _cum_kernel(fl_ref, bf_ref, cum_ref):
    x = fl_ref[...] + bf_ref[...]
    lf = jnp.minimum(x, 0.0) - jnp.log(1.0 + jnp.exp(-jnp.abs(x)))
    t = lax.broadcasted_iota(jnp.int32, (BLOCK, BLOCK), 0)
    j = lax.broadcasted_iota(jnp.int32, (BLOCK, BLOCK), 1)
    tri = jnp.where(j <= t, 1.0, 0.0).astype(BF16)
    parts = jnp.concatenate(_split3(lf), axis=1)
    total = jnp.zeros((1, LANES), F32)
    for blk in range(x.shape[0] // BLOCK):
        rows = slice(blk * BLOCK, (blk + 1) * BLOCK)
        c = jnp.dot(tri, parts[rows], preferred_element_type=F32)
        c = c[:, :LANES] + c[:, LANES:2 * LANES] + c[:, 2 * LANES:] + total
        cum_ref[rows, :] = c
        total = c[BLOCK - 1:BLOCK, :]


def _cum_log_forget(f_logit, b_f, batch, seq):
    return pl.pallas_call(
        _cum_kernel,
        grid=(batch,),
        in_specs=[pl.BlockSpec((seq, LANES), lambda b: (b, 0)),
                  pl.BlockSpec((1, LANES), lambda b: (0, 0))],
        out_specs=pl.BlockSpec((seq, LANES), lambda b: (b, 0)),
        out_shape=jax.ShapeDtypeStruct((batch * seq, LANES), F32),
        compiler_params=_params(("parallel",)),
        name="cum_log_forget",
    )(f_logit, b_f)


def _fox_kernel(q_ref, k_ref, v_ref, cq_ref, ck_ref, o_ref, *, tq):
    p = pl.program_id(1)
    i = pl.program_id(2)
    lhs = _stack_pair(q_ref[...] * SCALE)

    c = lax.broadcasted_iota(jnp.int32, (LANES, 2 * LANES), 0)
    l = lax.broadcasted_iota(jnp.int32, (LANES, 2 * LANES), 1)
    sel = jnp.where(c == 2 * p + l // LANES, 1.0, 0.0).astype(BF16)
    hi, mid, lo = _split3(cq_ref[...])
    cq2 = (jnp.dot(hi, sel, preferred_element_type=F32)
           + jnp.dot(mid, sel, preferred_element_type=F32)
           + jnp.dot(lo, sel, preferred_element_type=F32))
    cq = jnp.concatenate([cq2[:, :LANES], cq2[:, LANES:]], axis=0)

    def tile(j, m, lsum, acc, mask):
        kb = k_ref[pl.ds(pl.multiple_of(j * BLOCK, BLOCK), BLOCK), :]
        vb = v_ref[pl.ds(pl.multiple_of(j * BLOCK, BLOCK), BLOCK), :]
        ck = ck_ref[0, 0, j]
        ckb = jnp.concatenate([jnp.broadcast_to(ck[0:1], (tq, BLOCK)),
                               jnp.broadcast_to(ck[1:2], (tq, BLOCK))], axis=0)
        s = lax.dot_general(lhs, kb, _NT, preferred_element_type=F32) + cq - ckb
        if mask is not None:
            s = jnp.where(mask, s, NEG)
        m_new = jnp.maximum(m, jnp.max(s, axis=-1, keepdims=True))
        alpha = jnp.exp(m - m_new)
        e = jnp.exp(s - m_new)
        lsum = alpha * lsum + jnp.sum(e, axis=-1, keepdims=True)
        acc = alpha * acc + jnp.dot(e.astype(BF16), vb, preferred_element_type=F32)
        return m_new, lsum, acc

    row = lax.broadcasted_iota(jnp.int32, (tq, BLOCK), 0)
    col = lax.broadcasted_iota(jnp.int32, (tq, BLOCK), 1)
    m = jnp.full((2 * tq, 1), NEG, F32)
    lsum = jnp.zeros((2 * tq, 1), F32)
    acc = jnp.zeros((2 * tq, LANES), F32)
    nd = tq // BLOCK
    for d in range(nd):
        causal = col + d * BLOCK <= row
        m, lsum, acc = tile(i * nd + d, m, lsum, acc, jnp.concatenate([causal, causal], axis=0))

    def body(j, c):
        return tile(j, c[0], c[1], c[2], None)

    m, lsum, acc = lax.fori_loop(0, i * nd, body, (m, lsum, acc))
    o_ref[...] = _merge_pair(acc / lsum, tq).astype(o_ref.dtype)


def _fox_attention(proj, cum, cum_t, batch, seq, tq=128):
    m = proj.shape[0]
    nq = seq // tq
    nkb = seq // BLOCK
    return pl.pallas_call(
        functools.partial(_fox_kernel, tq=tq),
        grid=(batch, N_PAIRS, nq),
        in_specs=[
            pl.BlockSpec((tq, LANES), lambda b, p, i: (b * nq + i, p)),
            pl.BlockSpec((seq, LANES), lambda b, p, i: (b, N_PAIRS + p)),
            pl.BlockSpec((seq, LANES), lambda b, p, i: (b, 2 * N_PAIRS + p)),
            pl.BlockSpec((tq, LANES), lambda b, p, i: (b * nq + i, 0)),
            pl.BlockSpec((1, 1, nkb, 2, BLOCK), lambda b, p, i: (b, p, 0, 0, 0)),
        ],
        out_specs=pl.BlockSpec((tq, LANES), lambda b, p, i: (b * nq + i, p)),
        out_shape=jax.ShapeDtypeStruct((m, BRANCH), BF16),
        compiler_params=_params(("parallel", "parallel", "arbitrary")),
        name="fox_attention",
    )(proj, proj, proj, cum, cum_t)


def kernel(x, g_pre, g_post, w_in_a, w_out_a, sinks_a, w_in_b, w_out_b, w_in_c, b_f_c, w_out_c):
    batch, seq, d = x.shape
    depth = g_pre.shape[0]
    xf = x.reshape(batch * seq, d)
    for i in range(depth):
        kind, j = i % 3, i // 3
        if kind == 0:
            w = w_in_a[j]
            w = jnp.concatenate([w[:, :BRANCH], w[:, BRANCH + 2 * KV_A:],
                                 w[:, BRANCH:BRANCH + 2 * KV_A]], axis=1).astype(BF16)
            proj = _norm_proj(xf, g_pre[i], w, 1536, BF16)
            o = _swa_attention(proj, sinks_a[j], batch, seq)
            z_block, w_out = 1, w_out_a[j]
        elif kind == 1:
            proj = _norm_proj(xf, g_pre[i], w_in_b[j].astype(BF16), 1024, BF16)
            o = _stick_attention(proj, batch, seq)
            z_block, w_out = 3, w_out_b[j]
        else:
            w = w_in_c[j]
            proj = _norm_proj(xf, g_pre[i], w[:, :4 * BRANCH].astype(BF16), 1024, BF16)
            w_f = jnp.pad(w[:, 4 * BRANCH:], ((0, 0), (0, LANES - N_HEADS))).astype(BF16)
            f_logit = _norm_proj(xf, g_pre[i], w_f, LANES, F32)
            b_f = jnp.pad(b_f_c[j], (0, LANES - N_HEADS)).reshape(1, LANES)
            cum = _cum_log_forget(f_logit, b_f, batch, seq)
            cum_t = cum.reshape(batch, seq // BLOCK, BLOCK, LANES)[..., :N_HEADS]
            cum_t = cum_t.reshape(batch, seq // BLOCK, BLOCK, N_PAIRS, 2).transpose(0, 3, 1, 4, 2)
            o = _fox_attention(proj, cum, cum_t, batch, seq)
            z_block, w_out = 3, w_out_c[j]
        xf = _gate_out(o, proj, z_block, w_out.astype(BF16), g_post[i], xf)
    return xf.reshape(batch, seq, d)
```

```python
import functools

import jax
import jax.numpy as jnp
import numpy as np
from jax import lax
from jax.experimental import pallas as pl
from jax.experimental.pallas import tpu as pltpu

D_MODEL = 2048
HEAD_DIM = 64
N_HEADS = 32
BRANCH = N_HEADS * HEAD_DIM
N_KV_A = 4
KV_A = N_KV_A * HEAD_DIM
GROUP_A = N_HEADS // N_KV_A
BLOCK = 128
LANES = 128
NORM_EPS = 1e-6
NEG = -1e30
SCALE = HEAD_DIM ** -0.5
N_PAIRS = N_HEADS // 2

F32 = jnp.float32
BF16 = jnp.bfloat16

VMEM_LIMIT = 52 * 1024 * 1024

_NT = (((1,), (1,)), ((), ()))


def _params(sem):
    return pltpu.CompilerParams(dimension_semantics=sem, vmem_limit_bytes=VMEM_LIMIT)


def _norm_proj_kernel(x_ref, g_ref, w_ref, o_ref, h_ref):
    @pl.when(pl.program_id(1) == 0)
    def _():
        x = x_ref[...]
        r = lax.rsqrt(jnp.mean(x * x, axis=-1, keepdims=True) + NORM_EPS)
        h_ref[...] = (x * r * g_ref[...]).astype(BF16)

    o_ref[...] = jnp.dot(h_ref[...], w_ref[...], preferred_element_type=F32).astype(o_ref.dtype)


def _norm_proj(x, g, w, tn, out_dtype, tm=1024):
    m, d = x.shape
    n = w.shape[1]
    return pl.pallas_call(
        _norm_proj_kernel,
        grid=(m // tm, n // tn),
        in_specs=[
            pl.BlockSpec((tm, d), lambda i, j: (i, 0)),
            pl.BlockSpec((1, d), lambda i, j: (0, 0)),
            pl.BlockSpec((d, tn), lambda i, j: (0, j)),
        ],
        out_specs=pl.BlockSpec((tm, tn), lambda i, j: (i, j)),
        out_shape=jax.ShapeDtypeStruct((m, n), out_dtype),
        scratch_shapes=[pltpu.VMEM((tm, d), BF16)],
        compiler_params=_params(("parallel", "arbitrary")),
        name="norm_proj",
    )(x, g.reshape(1, d), w)


def _gate_out_kernel(o_ref, z_ref, w_ref, g_ref, x_ref, out_ref):
    z = z_ref[...].astype(F32)
    gated = (o_ref[...].astype(F32) * (z * jax.nn.sigmoid(z))).astype(BF16)
    y = jnp.dot(gated, w_ref[...], preferred_element_type=F32)
    r = lax.rsqrt(jnp.mean(y * y, axis=-1, keepdims=True) + NORM_EPS)
    out_ref[...] = x_ref[...] + y * r * g_ref[...]


def _gate_out(o, proj, z_block, w_out, g, x, tm=256):
    m, d = x.shape
    return pl.pallas_call(
        _gate_out_kernel,
        grid=(m // tm,),
        in_specs=[
            pl.BlockSpec((tm, BRANCH), lambda i: (i, 0)),
            pl.BlockSpec((tm, BRANCH), lambda i: (i, z_block)),
            pl.BlockSpec((BRANCH, d), lambda i: (0, 0)),
            pl.BlockSpec((1, d), lambda i: (0, 0)),
            pl.BlockSpec((tm, d), lambda i: (i, 0)),
        ],
        out_specs=pl.BlockSpec((tm, d), lambda i: (i, 0)),
        out_shape=jax.ShapeDtypeStruct((m, d), F32),
        compiler_params=_params(("parallel",)),
        name="gate_out",
    )(o, proj, w_out, g.reshape(1, d), x)


def _stack_pair(q):
    lane = lax.broadcasted_iota(jnp.int32, q.shape, 1)
    zero = jnp.zeros_like(q)
    return jnp.concatenate([jnp.where(lane < HEAD_DIM, q, zero),
                            jnp.where(lane >= HEAD_DIM, q, zero)], axis=0)


def _merge_pair(acc, t):
    lane = lax.broadcasted_iota(jnp.int32, (t, LANES), 1)
    return jnp.where(lane < HEAD_DIM, acc[:t], acc[t:])


def _alibi_slopes():
    n = N_HEADS
    return (2.0 ** (-8.0 * np.arange(1, n + 1, dtype=np.float32) / n)).astype(np.float32)


def _swa_kernel(sink_ref, q_ref, kp_ref, kc_ref, vp_ref, vc_ref, o_ref, bias_ref, *, nb):
    first = (pl.program_id(0) == 0) & (pl.program_id(1) == 0)

    @pl.when(first)
    def _():
        qi = lax.broadcasted_iota(jnp.int32, (BLOCK, 2 * BLOCK), 0)
        kj = lax.broadcasted_iota(jnp.int32, (BLOCK, 2 * BLOCK), 1)
        dist = qi + BLOCK - kj
        valid = (dist >= 0) & (dist < BLOCK)
        distf = dist.astype(F32)
        slopes = _alibi_slopes()
        for h in range(N_HEADS):
            bias_ref[h] = jnp.where(valid, -float(slopes[h]) * distf, NEG)

    n = pl.program_id(1)
    lane2 = lax.broadcasted_iota(jnp.int32, (1, 2 * BLOCK), 1)
    pen = jnp.where((lane2 < BLOCK) & (n == 0), NEG, 0.0).astype(F32)

    lane = lax.broadcasted_iota(jnp.int32, (2 * BLOCK, LANES), 1)
    low = lane < HEAD_DIM
    low_q = lax.broadcasted_iota(jnp.int32, (BLOCK, LANES), 1) < HEAD_DIM
    for g in range(N_KV_A):
        p, half = divmod(g, 2)
        cols = slice(p * LANES, (p + 1) * LANES)
        k2 = jnp.concatenate([kp_ref[:, cols], kc_ref[:, cols]], axis=0).astype(F32)
        v2 = jnp.concatenate([vp_ref[:, cols], vc_ref[:, cols]], axis=0).astype(F32)
        k2r = pltpu.roll(k2, HEAD_DIM, 1)
        v2r = pltpu.roll(v2, HEAD_DIM, 1)
        own_low = low if half == 0 else jnp.logical_not(low)
        kd = jnp.where(own_low, k2, k2r).astype(BF16)
        vd = jnp.where(own_low, v2, v2r).astype(BF16)

        q = q_ref[:, g * GROUP_A * HEAD_DIM:(g + 1) * GROUP_A * HEAD_DIM] * SCALE
        rows = [_stack_pair(q[:, i * LANES:(i + 1) * LANES]) for i in range(GROUP_A // 2)]
        lhs = jnp.concatenate(rows, axis=0)
        s = lax.dot_general(lhs, kd, _NT, preferred_element_type=F32)
        ps, ls = [], []
        for hh in range(GROUP_A):
            h = g * GROUP_A + hh
            sh = s[hh * BLOCK:(hh + 1) * BLOCK] + bias_ref[h] + pen
            sink = sink_ref[h]
            m = jnp.maximum(jnp.max(sh, axis=-1, keepdims=True), sink)
            e = jnp.exp(sh - m)
            ls.append(jnp.sum(e, axis=-1, keepdims=True) + jnp.exp(sink - m))
            ps.append(e.astype(BF16))
        pv = jnp.dot(jnp.concatenate(ps, axis=0), vd, preferred_element_type=F32)
        for i in range(GROUP_A // 2):
            o0 = pv[(2 * i) * BLOCK:(2 * i + 1) * BLOCK] / ls[2 * i]
            o1 = pv[(2 * i + 1) * BLOCK:(2 * i + 2) * BLOCK] / ls[2 * i + 1]
            c0 = (g * GROUP_A // 2 + i) * LANES
            o_ref[:, c0:c0 + LANES] = jnp.where(low_q, o0, o1).astype(o_ref.dtype)


def _swa_attention(proj, sinks, batch, seq):
    m = proj.shape[0]
    nb = seq // BLOCK
    kcol = (2 * BRANCH) // KV_A
    vcol = kcol + 1

    def cur(b, n):
        return b * nb + n

    def prev(b, n):
        return b * nb + jnp.maximum(n - 1, 0)

    return pl.pallas_call(
        functools.partial(_swa_kernel, nb=nb),
        grid=(batch, nb),
        in_specs=[
            pl.BlockSpec(memory_space=pltpu.SMEM),
            pl.BlockSpec((BLOCK, BRANCH), lambda b, n: (cur(b, n), 0)),
            pl.BlockSpec((BLOCK, KV_A), lambda b, n: (prev(b, n), kcol)),
            pl.BlockSpec((BLOCK, KV_A), lambda b, n: (cur(b, n), kcol)),
            pl.BlockSpec((BLOCK, KV_A), lambda b, n: (prev(b, n), vcol)),
            pl.BlockSpec((BLOCK, KV_A), lambda b, n: (cur(b, n), vcol)),
        ],
        out_specs=pl.BlockSpec((BLOCK, BRANCH), lambda b, n: (cur(b, n), 0)),
        out_shape=jax.ShapeDtypeStruct((m, BRANCH), BF16),
        scratch_shapes=[pltpu.VMEM((N_HEADS, BLOCK, 2 * BLOCK), F32)],
        compiler_params=_params(("arbitrary", "arbitrary")),
        name="swa_attention",
    )(sinks, proj, proj, proj, proj, proj)


TQ = 256
CHUNK = 256
PAIRS_PER_STEP = 4
PAIR_GROUPS = N_PAIRS // PAIRS_PER_STEP
HEADS_PER_STEP = 2 * PAIRS_PER_STEP


def _head_split(x):
    lane = lax.broadcasted_iota(jnp.int32, x.shape, 1)
    zero = jnp.zeros_like(x)
    return jnp.concatenate([jnp.where(lane < HEAD_DIM, x, zero),
                            jnp.where(lane >= HEAD_DIM, x, zero)], axis=0)


def _chunk_rows(c):
    return pl.ds(pl.multiple_of(c * CHUNK, CHUNK), CHUNK)


def _causal_attention_call(kernel_fn, name, proj, extra_inputs, extra_specs, batch, seq):
    m = proj.shape[0]
    nq = seq // TQ
    width = PAIRS_PER_STEP * LANES
    return pl.pallas_call(
        kernel_fn,
        grid=(batch, PAIR_GROUPS, nq),
        in_specs=[
            pl.BlockSpec((TQ, width), lambda b, g, i: (b * nq + i, g)),
            pl.BlockSpec((seq, width), lambda b, g, i: (b, PAIR_GROUPS + g)),
            pl.BlockSpec((seq, width), lambda b, g, i: (b, 2 * PAIR_GROUPS + g)),
        ] + extra_specs,
        out_specs=pl.BlockSpec((TQ, width), lambda b, g, i: (b * nq + i, g)),
        out_shape=jax.ShapeDtypeStruct((m, BRANCH), BF16),
        compiler_params=_params(("parallel", "parallel", "arbitrary")),
        name=name,
    )(proj, proj, proj, *extra_inputs)


def _suffix_matrix():
    j = lax.broadcasted_iota(jnp.int32, (2 * BLOCK, 2 * BLOCK), 0) % BLOCK
    s = lax.broadcasted_iota(jnp.int32, (2 * BLOCK, 2 * BLOCK), 1)
    return jnp.where((s >= BLOCK) | (j > s), 1.0, 0.0).astype(BF16)


def _softplus(s):
    return jnp.maximum(s, 0.0) + jnp.log(1.0 + jnp.exp(-jnp.abs(s)))


def _stick_kernel(q_ref, k_ref, v_ref, o_ref):
    i = pl.program_id(2)
    u2 = _suffix_matrix()
    row = lax.broadcasted_iota(jnp.int32, (TQ, CHUNK), 0)
    col = lax.broadcasted_iota(jnp.int32, (TQ, CHUNK), 1)
    before = jnp.concatenate([col < row, col < row], axis=1)

    def chunk(c, state, mask):
        rows = _chunk_rows(c)
        new_state = []
        for pr in range(PAIRS_PER_STEP):
            cols = slice(pr * LANES, (pr + 1) * LANES)
            carry0, carry1, acc = state[pr]
            q2 = q_ref[:, cols] * SCALE
            s = lax.dot_general(q2, _head_split(k_ref[rows, cols]), _NT,
                                preferred_element_type=F32)
            sp = _softplus(s)
            if mask is not None:
                sp = jnp.where(mask, sp, 0.0)
            hi = sp.astype(BF16)
            lo = (sp - hi.astype(F32)).astype(BF16)
            base = s - sp
            ps = [None] * 4
            carries = [carry0, carry1]
            for h in range(2):
                for kb in reversed(range(CHUNK // BLOCK)):
                    sl = slice(h * CHUNK + kb * BLOCK, h * CHUNK + (kb + 1) * BLOCK)
                    cs = jnp.dot(jnp.concatenate([hi[:, sl], lo[:, sl]], axis=1), u2,
                                 preferred_element_type=F32)
                    ps[h * 2 + kb] = jnp.exp(base[:, sl] - cs[:, :BLOCK] - carries[h])
                    carries[h] = carries[h] + cs[:, BLOCK:]
            p = jnp.concatenate(ps, axis=1)
            if mask is not None:
                p = jnp.where(mask, p, 0.0)
            acc = acc + jnp.dot(p.astype(BF16), _head_split(v_ref[rows, cols]),
                                preferred_element_type=F32)
            new_state.append((carries[0], carries[1], acc))
        return tuple(new_state)

    zero = jnp.zeros((TQ, LANES), F32)
    state = tuple((zero, zero, zero) for _ in range(PAIRS_PER_STEP))
    state = chunk(i, state, before)
    state = lax.fori_loop(0, i, lambda t, st: chunk(i - 1 - t, st, None), state)
    for pr in range(PAIRS_PER_STEP):
        o_ref[:, pr * LANES:(pr + 1) * LANES] = state[pr][2].astype(o_ref.dtype)


def _stick_attention(proj, batch, seq):
    return _causal_attention_call(_stick_kernel, "stick_attention", proj, [], [], batch, seq)


def _split3(x):
    hi = x.astype(BF16)
    r = x - hi.astype(F32)
    mid = r.astype(BF16)
    lo = (r - mid.astype(F32)).astype(BF16)
    return hi, mid, lo


def _cum_kernel(fl_ref, bf_ref, cum_ref):
    x = fl_ref[...] + bf_ref[...]
    lf = jnp.minimum(x, 0.0) - jnp.log(1.0 + jnp.exp(-jnp.abs(x)))
    t = lax.broadcasted_iota(jnp.int32, (BLOCK, BLOCK), 0)
    j = lax.broadcasted_iota(jnp.int32, (BLOCK, BLOCK), 1)
    tri = jnp.where(j <= t, 1.0, 0.0).astype(BF16)
    parts = jnp.concatenate(_split3(lf), axis=1)
    total = jnp.zeros((1, LANES), F32)
    for blk in range(x.shape[0] // BLOCK):
        rows = slice(blk * BLOCK, (blk + 1) * BLOCK)
        c = jnp.dot(tri, parts[rows], preferred_element_type=F32)
        c = c[:, :LANES] + c[:, LANES:2 * LANES] + c[:, 2 * LANES:] + total
        cum_ref[rows, :] = c
        total = c[BLOCK - 1:BLOCK, :]


def _cum_log_forget(f_logit, b_f, batch, seq):
    return pl.pallas_call(
        _cum_kernel,
        grid=(batch,),
        in_specs=[pl.BlockSpec((seq, LANES), lambda b: (b, 0)),
                  pl.BlockSpec((1, LANES), lambda b: (0, 0))],
        out_specs=pl.BlockSpec((seq, LANES), lambda b: (b, 0)),
        out_shape=jax.ShapeDtypeStruct((batch * seq, LANES), F32),
        compiler_params=_params(("parallel",)),
        name="cum_log_forget",
    )(f_logit, b_f)


def _fox_kernel(q_ref, k_ref, v_ref, cq_ref, ck_ref, o_ref):
    i = pl.program_id(2)
    row = lax.broadcasted_iota(jnp.int32, (TQ, CHUNK), 0)
    col = lax.broadcasted_iota(jnp.int32, (TQ, CHUNK), 1)
    causal = col <= row
    low = lax.broadcasted_iota(jnp.int32, (TQ, LANES), 1) < HEAD_DIM
    cq_all = cq_ref[0, 0]

    def chunk(c, state, mask):
        rows = _chunk_rows(c)
        ck_all = ck_ref[0, c, 0]
        new_state = []
        for pr in range(PAIRS_PER_STEP):
            cols = slice(pr * LANES, (pr + 1) * LANES)
            ms, ls, acc = state[pr]
            q2 = q_ref[:, cols] * SCALE
            s = lax.dot_general(q2, _head_split(k_ref[rows, cols]), _NT,
                                preferred_element_type=F32)
            ps, alphas, ms_new, ls_new = [], [], [], []
            for h in range(2):
                hd = 2 * pr + h
                cq = cq_all[:, hd:hd + 1]
                u = s[:, h * CHUNK:(h + 1) * CHUNK] - ck_all[hd:hd + 1, :]
                if mask is not None:
                    u = jnp.where(mask, u, NEG)
                m_new = jnp.maximum(ms[h], jnp.max(u, axis=-1, keepdims=True) + cq)
                alpha = jnp.exp(ms[h] - m_new)
                e = jnp.exp(u - (m_new - cq))
                ls_new.append(alpha * ls[h] + jnp.sum(e, axis=-1, keepdims=True))
                ms_new.append(m_new)
                alphas.append(alpha)
                ps.append(e.astype(BF16))
            pv = jnp.dot(jnp.concatenate(ps, axis=1), _head_split(v_ref[rows, cols]),
                         preferred_element_type=F32)
            acc = acc * jnp.where(low, alphas[0], alphas[1]) + pv
            new_state.append((tuple(ms_new), tuple(ls_new), acc))
        return tuple(new_state)

    neg = jnp.full((TQ, 1), NEG, F32)
    zero = jnp.zeros((TQ, 1), F32)
    state = tuple(((neg, neg), (zero, zero), jnp.zeros((TQ, LANES), F32))
                  for _ in range(PAIRS_PER_STEP))
    state = chunk(i, state, causal)
    state = lax.fori_loop(0, i, lambda c, st: chunk(c, st, None), state)
    for pr in range(PAIRS_PER_STEP):
        _, ls, acc = state[pr]
        o_ref[:, pr * LANES:(pr + 1) * LANES] = (
            acc * jnp.where(low, 1.0 / ls[0], 1.0 / ls[1])).astype(o_ref.dtype)


def _fox_attention(proj, cum, batch, seq):
    nc = seq // CHUNK
    c = cum.reshape(batch, seq, LANES)[:, :, :N_HEADS]
    cq = c.reshape(batch, seq, PAIR_GROUPS, HEADS_PER_STEP).transpose(0, 2, 1, 3)
    ck = c.reshape(batch, nc, CHUNK, PAIR_GROUPS, HEADS_PER_STEP).transpose(0, 1, 3, 4, 2)
    specs = [
        pl.BlockSpec((1, 1, TQ, HEADS_PER_STEP), lambda b, g, i: (b, g, i, 0)),
        pl.BlockSpec((1, nc, 1, HEADS_PER_STEP, CHUNK), lambda b, g, i: (b, 0, g, 0, 0)),
    ]
    return _causal_attention_call(_fox_kernel, "fox_attention", proj, [cq, ck], specs, batch, seq)


def kernel(x, g_pre, g_post, w_in_a, w_out_a, sinks_a, w_in_b, w_out_b, w_in_c, b_f_c, w_out_c):
    batch, seq, d = x.shape
    depth = g_pre.shape[0]
    xf = x.reshape(batch * seq, d)
    for i in range(depth):
        kind, j = i % 3, i // 3
        if kind == 0:
            w = w_in_a[j]
            w = jnp.concatenate([w[:, :BRANCH], w[:, BRANCH + 2 * KV_A:],
                                 w[:, BRANCH:BRANCH + 2 * KV_A]], axis=1).astype(BF16)
            proj = _norm_proj(xf, g_pre[i], w, 1536, BF16)
            o = _swa_attention(proj, sinks_a[j], batch, seq)
            z_block, w_out = 1, w_out_a[j]
        elif kind == 1:
            proj = _norm_proj(xf, g_pre[i], w_in_b[j].astype(BF16), 1024, BF16)
            o = _stick_attention(proj, batch, seq)
            z_block, w_out = 3, w_out_b[j]
        else:
            w = w_in_c[j]
            proj = _norm_proj(xf, g_pre[i], w[:, :4 * BRANCH].astype(BF16), 1024, BF16)
            w_f = jnp.pad(w[:, 4 * BRANCH:], ((0, 0), (0, LANES - N_HEADS))).astype(BF16)
            f_logit = _norm_proj(xf, g_pre[i], w_f, LANES, F32)
            b_f = jnp.pad(b_f_c[j], (0, LANES - N_HEADS)).reshape(1, LANES)
            cum = _cum_log_forget(f_logit, b_f, batch, seq)
            o = _fox_attention(proj, cum, batch, seq)
            z_block, w_out = 3, w_out_c[j]
        xf = _gate_out(o, proj, z_block, w_out.astype(BF16), g_post[i], xf)
    return xf.reshape(batch, seq, d)
```

```python
import functools
from typing import NamedTuple

import jax
import jax.numpy as jnp
import numpy as np
from jax import lax
from jax.experimental import pallas as pl
from jax.experimental.pallas import tpu as pltpu

D_MODEL = 2048
HEAD_DIM = 64
N_HEADS = 32
BRANCH = N_HEADS * HEAD_DIM
N_KV_A = 4
KV_A = N_KV_A * HEAD_DIM
GROUP_A = N_HEADS // N_KV_A
BLOCK = 128
LANES = 128
NORM_EPS = 1e-6
NEG = -1e30
SCALE = HEAD_DIM ** -0.5
LOG2E = 1.4426950408889634
N_PAIRS = N_HEADS // 2

F32 = jnp.float32
BF16 = jnp.bfloat16

VMEM_LIMIT = 52 * 1024 * 1024

_NT = (((1,), (1,)), ((), ()))


def _params(sem):
    return pltpu.CompilerParams(dimension_semantics=sem, vmem_limit_bytes=VMEM_LIMIT)


def _norm_proj_kernel(x_ref, g_ref, w_ref, o_ref, h_ref):
    @pl.when(pl.program_id(1) == 0)
    def _():
        x = x_ref[...]
        r = lax.rsqrt(jnp.mean(x * x, axis=-1, keepdims=True) + NORM_EPS)
        h_ref[...] = (x * r * g_ref[...]).astype(BF16)

    o_ref[...] = jnp.dot(h_ref[...], w_ref[...], preferred_element_type=F32).astype(o_ref.dtype)


def _norm_proj(x, g, w, tn, out_dtype, tm=1024):
    m, d = x.shape
    n = w.shape[1]
    return pl.pallas_call(
        _norm_proj_kernel,
        grid=(m // tm, n // tn),
        in_specs=[
            pl.BlockSpec((tm, d), lambda i, j: (i, 0)),
            pl.BlockSpec((1, d), lambda i, j: (0, 0)),
            pl.BlockSpec((d, tn), lambda i, j: (0, j)),
        ],
        out_specs=pl.BlockSpec((tm, tn), lambda i, j: (i, j)),
        out_shape=jax.ShapeDtypeStruct((m, n), out_dtype),
        scratch_shapes=[pltpu.VMEM((tm, d), BF16)],
        compiler_params=_params(("parallel", "arbitrary")),
        name="norm_proj",
    )(x, g.reshape(1, d), w)


def _gate_out_kernel(o_ref, z_ref, w_ref, g_ref, x_ref, out_ref):
    z = z_ref[...].astype(F32)
    gated = (o_ref[...].astype(F32) * (z * jax.nn.sigmoid(z))).astype(BF16)
    y = jnp.dot(gated, w_ref[...], preferred_element_type=F32)
    r = lax.rsqrt(jnp.mean(y * y, axis=-1, keepdims=True) + NORM_EPS)
    out_ref[...] = x_ref[...] + y * r * g_ref[...]


def _gate_out(o, proj, z_block, w_out, g, x, tm=256):
    m, d = x.shape
    return pl.pallas_call(
        _gate_out_kernel,
        grid=(m // tm,),
        in_specs=[
            pl.BlockSpec((tm, BRANCH), lambda i: (i, 0)),
            pl.BlockSpec((tm, BRANCH), lambda i: (i, z_block)),
            pl.BlockSpec((BRANCH, d), lambda i: (0, 0)),
            pl.BlockSpec((1, d), lambda i: (0, 0)),
            pl.BlockSpec((tm, d), lambda i: (i, 0)),
        ],
        out_specs=pl.BlockSpec((tm, d), lambda i: (i, 0)),
        out_shape=jax.ShapeDtypeStruct((m, d), F32),
        compiler_params=_params(("parallel",)),
        name="gate_out",
    )(o, proj, w_out, g.reshape(1, d), x)


def _stack_pair(q):
    lane = lax.broadcasted_iota(jnp.int32, q.shape, 1)
    zero = jnp.zeros_like(q)
    return jnp.concatenate([jnp.where(lane < HEAD_DIM, q, zero),
                            jnp.where(lane >= HEAD_DIM, q, zero)], axis=0)


def _alibi_slopes():
    n = N_HEADS
    return (2.0 ** (-8.0 * np.arange(1, n + 1, dtype=np.float32) / n)).astype(np.float32)


def _swa_kernel(sink_ref, q_ref, kp_ref, kc_ref, vp_ref, vc_ref, o_ref, bias_ref, *, nb):
    first = (pl.program_id(0) == 0) & (pl.program_id(1) == 0)

    @pl.when(first)
    def _():
        qi = lax.broadcasted_iota(jnp.int32, (BLOCK, 2 * BLOCK), 0)
        kj = lax.broadcasted_iota(jnp.int32, (BLOCK, 2 * BLOCK), 1)
        dist = qi + BLOCK - kj
        valid = (dist >= 0) & (dist < BLOCK)
        distf = dist.astype(F32)
        slopes = _alibi_slopes()
        for h in range(N_HEADS):
            bias_ref[h] = jnp.where(valid, -float(slopes[h]) * distf, NEG)

    n = pl.program_id(1)
    lane2 = lax.broadcasted_iota(jnp.int32, (1, 2 * BLOCK), 1)
    pen = jnp.where((lane2 < BLOCK) & (n == 0), NEG, 0.0).astype(F32)

    lane = lax.broadcasted_iota(jnp.int32, (2 * BLOCK, LANES), 1)
    low = lane < HEAD_DIM
    low_q = lax.broadcasted_iota(jnp.int32, (BLOCK, LANES), 1) < HEAD_DIM
    for g in range(N_KV_A):
        p, half = divmod(g, 2)
        cols = slice(p * LANES, (p + 1) * LANES)
        k2 = jnp.concatenate([kp_ref[:, cols], kc_ref[:, cols]], axis=0).astype(F32)
        v2 = jnp.concatenate([vp_ref[:, cols], vc_ref[:, cols]], axis=0).astype(F32)
        k2r = pltpu.roll(k2, HEAD_DIM, 1)
        v2r = pltpu.roll(v2, HEAD_DIM, 1)
        own_low = low if half == 0 else jnp.logical_not(low)
        kd = jnp.where(own_low, k2, k2r).astype(BF16)
        vd = jnp.where(own_low, v2, v2r).astype(BF16)

        q = q_ref[:, g * GROUP_A * HEAD_DIM:(g + 1) * GROUP_A * HEAD_DIM] * SCALE
        rows = [_stack_pair(q[:, i * LANES:(i + 1) * LANES]) for i in range(GROUP_A // 2)]
        lhs = jnp.concatenate(rows, axis=0)
        s = lax.dot_general(lhs, kd, _NT, preferred_element_type=F32)
        ps, ls = [], []
        for hh in range(GROUP_A):
            h = g * GROUP_A + hh
            sh = s[hh * BLOCK:(hh + 1) * BLOCK] + bias_ref[h] + pen
            sink = sink_ref[h]
            m = jnp.maximum(jnp.max(sh, axis=-1, keepdims=True), sink)
            e = jnp.exp(sh - m)
            ls.append(jnp.sum(e, axis=-1, keepdims=True) + jnp.exp(sink - m))
            ps.append(e.astype(BF16))
        pv = jnp.dot(jnp.concatenate(ps, axis=0), vd, preferred_element_type=F32)
        for i in range(GROUP_A // 2):
            o0 = pv[(2 * i) * BLOCK:(2 * i + 1) * BLOCK] / ls[2 * i]
            o1 = pv[(2 * i + 1) * BLOCK:(2 * i + 2) * BLOCK] / ls[2 * i + 1]
            c0 = (g * GROUP_A // 2 + i) * LANES
            o_ref[:, c0:c0 + LANES] = jnp.where(low_q, o0, o1).astype(o_ref.dtype)


def _swa_attention(proj, sinks, batch, seq):
    m = proj.shape[0]
    nb = seq // BLOCK
    kcol = (2 * BRANCH) // KV_A
    vcol = kcol + 1

    def cur(b, n):
        return b * nb + n

    def prev(b, n):
        return b * nb + jnp.maximum(n - 1, 0)

    return pl.pallas_call(
        functools.partial(_swa_kernel, nb=nb),
        grid=(batch, nb),
        in_specs=[
            pl.BlockSpec(memory_space=pltpu.SMEM),
            pl.BlockSpec((BLOCK, BRANCH), lambda b, n: (cur(b, n), 0)),
            pl.BlockSpec((BLOCK, KV_A), lambda b, n: (prev(b, n), kcol)),
            pl.BlockSpec((BLOCK, KV_A), lambda b, n: (cur(b, n), kcol)),
            pl.BlockSpec((BLOCK, KV_A), lambda b, n: (prev(b, n), vcol)),
            pl.BlockSpec((BLOCK, KV_A), lambda b, n: (cur(b, n), vcol)),
        ],
        out_specs=pl.BlockSpec((BLOCK, BRANCH), lambda b, n: (cur(b, n), 0)),
        out_shape=jax.ShapeDtypeStruct((m, BRANCH), BF16),
        scratch_shapes=[pltpu.VMEM((N_HEADS, BLOCK, 2 * BLOCK), F32)],
        compiler_params=_params(("arbitrary", "arbitrary")),
        name="swa_attention",
    )(sinks, proj, proj, proj, proj, proj)


class _Tiling(NamedTuple):
    tile: int
    pairs: int

    @property
    def groups(self):
        return N_PAIRS // self.pairs

    @property
    def heads(self):
        return 2 * self.pairs


STICK_TILING = _Tiling(tile=256, pairs=4)
FOX_TILING = _Tiling(tile=512, pairs=2)


def _head_split(x):
    lane = lax.broadcasted_iota(jnp.int32, x.shape, 1)
    zero = jnp.zeros_like(x)
    return jnp.concatenate([jnp.where(lane < HEAD_DIM, x, zero),
                            jnp.where(lane >= HEAD_DIM, x, zero)], axis=0)


def _chunk_rows(c, tile):
    return pl.ds(pl.multiple_of(c * tile, tile), tile)


def _causal_attention_call(kernel_fn, name, tiling, proj, extra_inputs, extra_specs, batch, seq):
    m = proj.shape[0]
    tile, groups = tiling.tile, tiling.groups
    nq = seq // tile
    width = tiling.pairs * LANES
    return pl.pallas_call(
        functools.partial(kernel_fn, tiling=tiling),
        grid=(batch, groups, nq),
        in_specs=[
            pl.BlockSpec((tile, width), lambda b, g, i: (b * nq + i, g)),
            pl.BlockSpec((seq, width), lambda b, g, i: (b, groups + g)),
            pl.BlockSpec((seq, width), lambda b, g, i: (b, 2 * groups + g)),
        ] + extra_specs,
        out_specs=pl.BlockSpec((tile, width), lambda b, g, i: (b * nq + i, g)),
        out_shape=jax.ShapeDtypeStruct((m, BRANCH), BF16),
        compiler_params=_params(("parallel", "parallel", "arbitrary")),
        name=name,
    )(proj, proj, proj, *extra_inputs)


def _suffix_matrix():
    j = lax.broadcasted_iota(jnp.int32, (2 * BLOCK, 2 * BLOCK), 0) % BLOCK
    s = lax.broadcasted_iota(jnp.int32, (2 * BLOCK, 2 * BLOCK), 1)
    return jnp.where((s >= BLOCK) | (j > s), 1.0, 0.0).astype(BF16)


def _softplus2(s2):
    neg_abs = lax.bitcast_convert_type(
        lax.bitcast_convert_type(s2, jnp.uint32) | jnp.uint32(0x80000000), F32)
    return jnp.maximum(s2, 0.0) + jnp.log2(1.0 + jnp.exp2(neg_abs))


def _stick_kernel(q_ref, k_ref, v_ref, o_ref, *, tiling):
    tile, pairs = tiling
    nkb = tile // BLOCK
    i = pl.program_id(2)
    u2 = _suffix_matrix()
    row = lax.broadcasted_iota(jnp.int32, (tile, tile), 0)
    col = lax.broadcasted_iota(jnp.int32, (tile, tile), 1)
    before = jnp.concatenate([col < row, col < row], axis=1)

    def chunk(c, state, mask):
        rows = _chunk_rows(c, tile)
        new_state = []
        for pr in range(pairs):
            cols = slice(pr * LANES, (pr + 1) * LANES)
            carry0, carry1, acc = state[pr]
            q2 = q_ref[:, cols] * SCALE
            s2 = lax.dot_general(q2, _head_split(k_ref[rows, cols]), _NT,
                                 preferred_element_type=F32) * LOG2E
            sp = _softplus2(s2)
            if mask is not None:
                sp = jnp.where(mask, sp, 0.0)
            hi = sp.astype(BF16)
            lo = (sp - hi.astype(F32)).astype(BF16)
            base = s2 - sp
            ps = [None] * (2 * nkb)
            carries = [carry0, carry1]
            for h in range(2):
                for kb in reversed(range(nkb)):
                    sl = slice(h * tile + kb * BLOCK, h * tile + (kb + 1) * BLOCK)
                    cs = jnp.dot(jnp.concatenate([hi[:, sl], lo[:, sl]], axis=1), u2,
                                 preferred_element_type=F32)
                    ps[h * nkb + kb] = jnp.exp2(base[:, sl] - cs[:, :BLOCK] - carries[h])
                    carries[h] = carries[h] + cs[:, BLOCK:]
            p = jnp.concatenate(ps, axis=1)
            if mask is not None:
                p = jnp.where(mask, p, 0.0)
            acc = acc + jnp.dot(p.astype(BF16), _head_split(v_ref[rows, cols]),
                                preferred_element_type=F32)
            new_state.append((carries[0], carries[1], acc))
        return tuple(new_state)

    zero = jnp.zeros((tile, LANES), F32)
    state = tuple((zero, zero, zero) for _ in range(pairs))
    state = chunk(i, state, before)
    state = lax.fori_loop(0, i, lambda t, st: chunk(i - 1 - t, st, None), state)
    for pr in range(pairs):
        o_ref[:, pr * LANES:(pr + 1) * LANES] = state[pr][2].astype(o_ref.dtype)


def _stick_attention(proj, batch, seq):
    return _causal_attention_call(_stick_kernel, "stick_attention", STICK_TILING, proj, [], [],
                                  batch, seq)


def _split3(x):
    hi = x.astype(BF16)
    r = x - hi.astype(F32)
    mid = r.astype(BF16)
    lo = (r - mid.astype(F32)).astype(BF16)
    return hi, mid, lo


def _cum_kernel(fl_ref, bf_ref, cum_ref):
    x = fl_ref[...] + bf_ref[...]
    lf = jnp.minimum(x, 0.0) - jnp.log(1.0 + jnp.exp(-jnp.abs(x)))
    t = lax.broadcasted_iota(jnp.int32, (BLOCK, BLOCK), 0)
    j = lax.broadcasted_iota(jnp.int32, (BLOCK, BLOCK), 1)
    tri = jnp.where(j <= t, 1.0, 0.0).astype(BF16)
    parts = jnp.concatenate(_split3(lf), axis=1)
    total = jnp.zeros((1, LANES), F32)
    for blk in range(x.shape[0] // BLOCK):
        rows = slice(blk * BLOCK, (blk + 1) * BLOCK)
        c = jnp.dot(tri, parts[rows], preferred_element_type=F32)
        c = c[:, :LANES] + c[:, LANES:2 * LANES] + c[:, 2 * LANES:] + total
        cum_ref[rows, :] = c
        total = c[BLOCK - 1:BLOCK, :]


def _cum_log_forget(f_logit, b_f, batch, seq):
    return pl.pallas_call(
        _cum_kernel,
        grid=(batch,),
        in_specs=[pl.BlockSpec((seq, LANES), lambda b: (b, 0)),
                  pl.BlockSpec((1, LANES), lambda b: (0, 0))],
        out_specs=pl.BlockSpec((seq, LANES), lambda b: (b, 0)),
        out_shape=jax.ShapeDtypeStruct((batch * seq, LANES), F32),
        compiler_params=_params(("parallel",)),
        name="cum_log_forget",
    )(f_logit, b_f)


def _fox_kernel(q_ref, k_ref, v_ref, cq_ref, ck_ref, o_ref, *, tiling):
    tile, pairs = tiling
    i = pl.program_id(2)
    row = lax.broadcasted_iota(jnp.int32, (tile, tile), 0)
    col = lax.broadcasted_iota(jnp.int32, (tile, tile), 1)
    causal = col <= row
    low = lax.broadcasted_iota(jnp.int32, (tile, 2 * LANES), 1) % LANES < HEAD_DIM
    cq_all = cq_ref[0, 0]
    r2 = lax.broadcasted_iota(jnp.int32, (2 * tile, LANES), 0)
    l2 = lax.broadcasted_iota(jnp.int32, (2 * tile, LANES), 1)
    ones = jnp.where((r2 < tile) == (l2 < HEAD_DIM), 1.0, 0.0).astype(BF16)

    def chunk(c, state, mask):
        rows = _chunk_rows(c, tile)
        ck_all = ck_ref[0, c, 0]
        new_state = []
        for pr in range(pairs):
            cols = slice(pr * LANES, (pr + 1) * LANES)
            ms, acc = state[pr]
            q2 = q_ref[:, cols] * SCALE
            s = lax.dot_general(q2, _head_split(k_ref[rows, cols]), _NT,
                                preferred_element_type=F32)
            ps, alphas, ms_new = [], [], []
            for h in range(2):
                hd = 2 * pr + h
                cq = cq_all[:, hd:hd + 1]
                u = s[:, h * tile:(h + 1) * tile] - ck_all[hd:hd + 1, :]
                if mask is not None:
                    u = jnp.where(mask, u, NEG)
                m_new = jnp.maximum(ms[h], jnp.max(u, axis=-1, keepdims=True) + cq)
                alphas.append(jnp.exp(ms[h] - m_new))
                ps.append(jnp.exp(u - (m_new - cq)).astype(BF16))
                ms_new.append(m_new)
            rhs = jnp.concatenate([_head_split(v_ref[rows, cols]), ones], axis=1)
            pv = jnp.dot(jnp.concatenate(ps, axis=1), rhs,
                         preferred_element_type=F32)
            acc = acc * jnp.where(low, alphas[0], alphas[1]) + pv
            new_state.append((tuple(ms_new), acc))
        return tuple(new_state)

    neg = jnp.full((tile, 1), NEG, F32)
    state = tuple(((neg, neg), jnp.zeros((tile, 2 * LANES), F32)) for _ in range(pairs))
    state = chunk(i, state, causal)
    state = lax.fori_loop(0, i, lambda c, st: chunk(c, st, None), state)
    for pr in range(pairs):
        acc = state[pr][1]
        o_ref[:, pr * LANES:(pr + 1) * LANES] = (acc[:, :LANES] / acc[:, LANES:]).astype(o_ref.dtype)


def _fox_attention(proj, cum, batch, seq):
    tile, groups, heads = FOX_TILING.tile, FOX_TILING.groups, FOX_TILING.heads
    nc = seq // tile
    c = cum.reshape(batch, seq, LANES)[:, :, :N_HEADS]
    cq = c.reshape(batch, seq, groups, heads).transpose(0, 2, 1, 3)
    ck = c.reshape(batch, nc, tile, groups, heads).transpose(0, 1, 3, 4, 2)
    specs = [
        pl.BlockSpec((1, 1, tile, heads), lambda b, g, i: (b, g, i, 0)),
        pl.BlockSpec((1, nc, 1, heads, tile), lambda b, g, i: (b, 0, g, 0, 0)),
    ]
    return _causal_attention_call(_fox_kernel, "fox_attention", FOX_TILING, proj, [cq, ck], specs,
                                  batch, seq)


def kernel(x, g_pre, g_post, w_in_a, w_out_a, sinks_a, w_in_b, w_out_b, w_in_c, b_f_c, w_out_c):
    batch, seq, d = x.shape
    depth = g_pre.shape[0]
    xf = x.reshape(batch * seq, d)
    for i in range(depth):
        kind, j = i % 3, i // 3
        if kind == 0:
            w = w_in_a[j]
            w = jnp.concatenate([w[:, :BRANCH], w[:, BRANCH + 2 * KV_A:],
                                 w[:, BRANCH:BRANCH + 2 * KV_A]], axis=1).astype(BF16)
            proj = _norm_proj(xf, g_pre[i], w, 1536, BF16)
            o = _swa_attention(proj, sinks_a[j], batch, seq)
            z_block, w_out = 1, w_out_a[j]
        elif kind == 1:
            proj = _norm_proj(xf, g_pre[i], w_in_b[j].astype(BF16), 1024, BF16)
            o = _stick_attention(proj, batch, seq)
            z_block, w_out = 3, w_out_b[j]
        else:
            w = w_in_c[j]
            proj = _norm_proj(xf, g_pre[i], w[:, :4 * BRANCH].astype(BF16), 1024, BF16)
            w_f = jnp.pad(w[:, 4 * BRANCH:], ((0, 0), (0, LANES - N_HEADS))).astype(BF16)
            f_logit = _norm_proj(xf, g_pre[i], w_f, LANES, F32)
            b_f = jnp.pad(b_f_c[j], (0, LANES - N_HEADS)).reshape(1, LANES)
            cum = _cum_log_forget(f_logit, b_f, batch, seq)
            o = _fox_attention(proj, cum, batch, seq)
            z_block, w_out = 3, w_out_c[j]
        xf = _gate_out(o, proj, z_block, w_out.astype(BF16), g_post[i], xf)
    return xf.reshape(batch, seq, d)
```

```python
import functools
from typing import NamedTuple

import jax
import jax.numpy as jnp
import numpy as np
from jax import lax
from jax.experimental import pallas as pl
from jax.experimental.pallas import tpu as pltpu

D_MODEL = 2048
HEAD_DIM = 64
N_HEADS = 32
BRANCH = N_HEADS * HEAD_DIM
N_KV_A = 4
KV_A = N_KV_A * HEAD_DIM
GROUP_A = N_HEADS // N_KV_A
BLOCK = 128
LANES = 128
NORM_EPS = 1e-6
NEG = -1e30
SCALE = HEAD_DIM ** -0.5
N_PAIRS = N_HEADS // 2

F32 = jnp.float32
BF16 = jnp.bfloat16

VMEM_LIMIT = 52 * 1024 * 1024

_NT = (((1,), (1,)), ((), ()))


def _params(sem):
    return pltpu.CompilerParams(dimension_semantics=sem, vmem_limit_bytes=VMEM_LIMIT)


def _norm_proj_kernel(x_ref, g_ref, w_ref, o_ref, h_ref):
    @pl.when(pl.program_id(1) == 0)
    def _():
        x = x_ref[...]
        r = lax.rsqrt(jnp.mean(x * x, axis=-1, keepdims=True) + NORM_EPS)
        h_ref[...] = (x * r * g_ref[...]).astype(BF16)

    o_ref[...] = jnp.dot(h_ref[...], w_ref[...], preferred_element_type=F32).astype(o_ref.dtype)


def _norm_proj(x, g, w, tn, out_dtype, tm=1024):
    m, d = x.shape
    n = w.shape[1]
    return pl.pallas_call(
        _norm_proj_kernel,
        grid=(m // tm, n // tn),
        in_specs=[
            pl.BlockSpec((tm, d), lambda i, j: (i, 0)),
            pl.BlockSpec((1, d), lambda i, j: (0, 0)),
            pl.BlockSpec((d, tn), lambda i, j: (0, j)),
        ],
        out_specs=pl.BlockSpec((tm, tn), lambda i, j: (i, j)),
        out_shape=jax.ShapeDtypeStruct((m, n), out_dtype),
        scratch_shapes=[pltpu.VMEM((tm, d), BF16)],
        compiler_params=_params(("parallel", "arbitrary")),
        name="norm_proj",
    )(x, g.reshape(1, d), w)


def _gate_out_kernel(o_ref, z_ref, w_ref, g_ref, x_ref, out_ref):
    z = z_ref[...].astype(F32)
    gated = (o_ref[...].astype(F32) * (z * jax.nn.sigmoid(z))).astype(BF16)
    y = jnp.dot(gated, w_ref[...], preferred_element_type=F32)
    r = lax.rsqrt(jnp.mean(y * y, axis=-1, keepdims=True) + NORM_EPS)
    out_ref[...] = x_ref[...] + y * r * g_ref[...]


def _gate_out(o, proj, z_block, w_out, g, x, tm=256):
    m, d = x.shape
    return pl.pallas_call(
        _gate_out_kernel,
        grid=(m // tm,),
        in_specs=[
            pl.BlockSpec((tm, BRANCH), lambda i: (i, 0)),
            pl.BlockSpec((tm, BRANCH), lambda i: (i, z_block)),
            pl.BlockSpec((BRANCH, d), lambda i: (0, 0)),
            pl.BlockSpec((1, d), lambda i: (0, 0)),
            pl.BlockSpec((tm, d), lambda i: (i, 0)),
        ],
        out_specs=pl.BlockSpec((tm, d), lambda i: (i, 0)),
        out_shape=jax.ShapeDtypeStruct((m, d), F32),
        compiler_params=_params(("parallel",)),
        name="gate_out",
    )(o, proj, w_out, g.reshape(1, d), x)


def _stack_pair(q):
    lane = lax.broadcasted_iota(jnp.int32, q.shape, 1)
    zero = jnp.zeros_like(q)
    return jnp.concatenate([jnp.where(lane < HEAD_DIM, q, zero),
                            jnp.where(lane >= HEAD_DIM, q, zero)], axis=0)


def _alibi_slopes():
    n = N_HEADS
    return (2.0 ** (-8.0 * np.arange(1, n + 1, dtype=np.float32) / n)).astype(np.float32)


def _swa_kernel(sink_ref, q_ref, kp_ref, kc_ref, vp_ref, vc_ref, o_ref, bias_ref, *, nb):
    first = (pl.program_id(0) == 0) & (pl.program_id(1) == 0)

    @pl.when(first)
    def _():
        qi = lax.broadcasted_iota(jnp.int32, (BLOCK, 2 * BLOCK), 0)
        kj = lax.broadcasted_iota(jnp.int32, (BLOCK, 2 * BLOCK), 1)
        dist = qi + BLOCK - kj
        valid = (dist >= 0) & (dist < BLOCK)
        distf = dist.astype(F32)
        slopes = _alibi_slopes()
        for h in range(N_HEADS):
            bias_ref[h] = jnp.where(valid, -float(slopes[h]) * distf, NEG)

    n = pl.program_id(1)
    lane2 = lax.broadcasted_iota(jnp.int32, (1, 2 * BLOCK), 1)
    pen = jnp.where((lane2 < BLOCK) & (n == 0), NEG, 0.0).astype(F32)

    lane = lax.broadcasted_iota(jnp.int32, (2 * BLOCK, LANES), 1)
    low = lane < HEAD_DIM
    low_q = lax.broadcasted_iota(jnp.int32, (BLOCK, LANES), 1) < HEAD_DIM
    for g in range(N_KV_A):
        p, half = divmod(g, 2)
        cols = slice(p * LANES, (p + 1) * LANES)
        k2 = jnp.concatenate([kp_ref[:, cols], kc_ref[:, cols]], axis=0).astype(F32)
        v2 = jnp.concatenate([vp_ref[:, cols], vc_ref[:, cols]], axis=0).astype(F32)
        k2r = pltpu.roll(k2, HEAD_DIM, 1)
        v2r = pltpu.roll(v2, HEAD_DIM, 1)
        own_low = low if half == 0 else jnp.logical_not(low)
        kd = jnp.where(own_low, k2, k2r).astype(BF16)
        vd = jnp.where(own_low, v2, v2r).astype(BF16)

        q = q_ref[:, g * GROUP_A * HEAD_DIM:(g + 1) * GROUP_A * HEAD_DIM] * SCALE
        rows = [_stack_pair(q[:, i * LANES:(i + 1) * LANES]) for i in range(GROUP_A // 2)]
        lhs = jnp.concatenate(rows, axis=0)
        s = lax.dot_general(lhs, kd, _NT, preferred_element_type=F32)
        ps, ls = [], []
        for hh in range(GROUP_A):
            h = g * GROUP_A + hh
            sh = s[hh * BLOCK:(hh + 1) * BLOCK] + bias_ref[h] + pen
            sink = sink_ref[h]
            m = jnp.maximum(jnp.max(sh, axis=-1, keepdims=True), sink)
            e = jnp.exp(sh - m)
            ls.append(jnp.sum(e, axis=-1, keepdims=True) + jnp.exp(sink - m))
            ps.append(e.astype(BF16))
        pv = jnp.dot(jnp.concatenate(ps, axis=0), vd, preferred_element_type=F32)
        for i in range(GROUP_A // 2):
            o0 = pv[(2 * i) * BLOCK:(2 * i + 1) * BLOCK] / ls[2 * i]
            o1 = pv[(2 * i + 1) * BLOCK:(2 * i + 2) * BLOCK] / ls[2 * i + 1]
            c0 = (g * GROUP_A // 2 + i) * LANES
            o_ref[:, c0:c0 + LANES] = jnp.where(low_q, o0, o1).astype(o_ref.dtype)


def _swa_attention(proj, sinks, batch, seq):
    m = proj.shape[0]
    nb = seq // BLOCK
    kcol = (2 * BRANCH) // KV_A
    vcol = kcol + 1

    def cur(b, n):
        return b * nb + n

    def prev(b, n):
        return b * nb + jnp.maximum(n - 1, 0)

    return pl.pallas_call(
        functools.partial(_swa_kernel, nb=nb),
        grid=(batch, nb),
        in_specs=[
            pl.BlockSpec(memory_space=pltpu.SMEM),
            pl.BlockSpec((BLOCK, BRANCH), lambda b, n: (cur(b, n), 0)),
            pl.BlockSpec((BLOCK, KV_A), lambda b, n: (prev(b, n), kcol)),
            pl.BlockSpec((BLOCK, KV_A), lambda b, n: (cur(b, n), kcol)),
            pl.BlockSpec((BLOCK, KV_A), lambda b, n: (prev(b, n), vcol)),
            pl.BlockSpec((BLOCK, KV_A), lambda b, n: (cur(b, n), vcol)),
        ],
        out_specs=pl.BlockSpec((BLOCK, BRANCH), lambda b, n: (cur(b, n), 0)),
        out_shape=jax.ShapeDtypeStruct((m, BRANCH), BF16),
        scratch_shapes=[pltpu.VMEM((N_HEADS, BLOCK, 2 * BLOCK), F32)],
        compiler_params=_params(("arbitrary", "arbitrary")),
        name="swa_attention",
    )(sinks, proj, proj, proj, proj, proj)


class _Tiling(NamedTuple):
    tile: int
    pairs: int

    @property
    def groups(self):
        return N_PAIRS // self.pairs

    @property
    def heads(self):
        return 2 * self.pairs


STICK_TILING = _Tiling(tile=256, pairs=4)
FOX_TILING = _Tiling(tile=512, pairs=2)


def _head_split(x):
    lane = lax.broadcasted_iota(jnp.int32, x.shape, 1)
    zero = jnp.zeros_like(x)
    return jnp.concatenate([jnp.where(lane < HEAD_DIM, x, zero),
                            jnp.where(lane >= HEAD_DIM, x, zero)], axis=0)


def _chunk_rows(c, tile):
    return pl.ds(pl.multiple_of(c * tile, tile), tile)


def _causal_attention_call(kernel_fn, name, tiling, proj, extra_inputs, extra_specs, scratch,
                           batch, seq):
    m = proj.shape[0]
    tile, groups = tiling.tile, tiling.groups
    nq = seq // tile
    width = tiling.pairs * LANES
    return pl.pallas_call(
        functools.partial(kernel_fn, tiling=tiling),
        grid=(batch, groups, nq),
        in_specs=[
            pl.BlockSpec((tile, width), lambda b, g, i: (b * nq + i, g)),
            pl.BlockSpec((seq, width), lambda b, g, i: (b, groups + g)),
            pl.BlockSpec((seq, width), lambda b, g, i: (b, 2 * groups + g)),
        ] + extra_specs,
        out_specs=pl.BlockSpec((tile, width), lambda b, g, i: (b * nq + i, g)),
        out_shape=jax.ShapeDtypeStruct((m, BRANCH), BF16),
        scratch_shapes=scratch,
        compiler_params=_params(("parallel", "parallel", "arbitrary")),
        name=name,
    )(proj, proj, proj, *extra_inputs)


def _suffix_matrix():
    j = lax.broadcasted_iota(jnp.int32, (BLOCK, 2 * BLOCK), 0)
    s = lax.broadcasted_iota(jnp.int32, (BLOCK, 2 * BLOCK), 1)
    return jnp.where((s >= BLOCK) | (j > s), 1.0, 0.0).astype(BF16)


SOFTPLUS_CLAMP = 60.0


def _softplus(s):
    return jnp.maximum(s, jnp.log(1.0 + jnp.exp(jnp.minimum(s, SOFTPLUS_CLAMP))))


def _stick_kernel(q_ref, k_ref, v_ref, o_ref, acc_ref, carry_ref, *, tiling):
    tile, pairs = tiling
    nkb = tile // BLOCK
    i = pl.program_id(2)
    u = _suffix_matrix()
    row = lax.broadcasted_iota(jnp.int32, (tile, tile), 0)
    col = lax.broadcasted_iota(jnp.int32, (tile, tile), 1)
    before = jnp.concatenate([col < row, col < row], axis=1)

    def chunk(c, mask):
        rows = _chunk_rows(c, tile)
        for pr in range(pairs):
            cols = slice(pr * LANES, (pr + 1) * LANES)
            q2 = q_ref[:, cols] * SCALE
            s = lax.dot_general(q2, _head_split(k_ref[rows, cols]), _NT,
                                preferred_element_type=F32)
            sp = _softplus(s)
            if mask is not None:
                sp = jnp.where(mask, sp, 0.0)
            spb = sp.astype(BF16)
            base = s - sp
            ps = [None] * (2 * nkb)
            for h in range(2):
                carry = carry_ref[pr, h]
                for kb in reversed(range(nkb)):
                    sl = slice(h * tile + kb * BLOCK, h * tile + (kb + 1) * BLOCK)
                    cs = jnp.dot(spb[:, sl], u, preferred_element_type=F32)
                    ps[h * nkb + kb] = jnp.exp(base[:, sl] - cs[:, :BLOCK] - carry)
                    carry = carry + cs[:, BLOCK:]
                carry_ref[pr, h] = carry
            p = jnp.concatenate(ps, axis=1)
            if mask is not None:
                p = jnp.where(mask, p, 0.0)
            acc_ref[pr] += jnp.dot(p.astype(BF16), _head_split(v_ref[rows, cols]),
                                   preferred_element_type=F32)

    acc_ref[...] = jnp.zeros_like(acc_ref)
    carry_ref[...] = jnp.zeros_like(carry_ref)
    chunk(i, before)

    def body(t, _):
        chunk(i - 1 - t, None)
        return 0

    lax.fori_loop(0, i, body, 0)
    for pr in range(pairs):
        o_ref[:, pr * LANES:(pr + 1) * LANES] = acc_ref[pr].astype(o_ref.dtype)


def _stick_attention(proj, batch, seq):
    tile, pairs = STICK_TILING
    scratch = [pltpu.VMEM((pairs, tile, LANES), F32),
               pltpu.VMEM((pairs, 2, tile, LANES), F32)]
    return _causal_attention_call(_stick_kernel, "stick_attention", STICK_TILING, proj, [], [],
                                  scratch, batch, seq)


def _split3(x):
    hi = x.astype(BF16)
    r = x - hi.astype(F32)
    mid = r.astype(BF16)
    lo = (r - mid.astype(F32)).astype(BF16)
    return hi, mid, lo


def _cum_kernel(fl_ref, bf_ref, cum_ref):
    x = fl_ref[...] + bf_ref[...]
    lf = jnp.minimum(x, 0.0) - jnp.log(1.0 + jnp.exp(-jnp.abs(x)))
    t = lax.broadcasted_iota(jnp.int32, (BLOCK, BLOCK), 0)
    j = lax.broadcasted_iota(jnp.int32, (BLOCK, BLOCK), 1)
    tri = jnp.where(j <= t, 1.0, 0.0).astype(BF16)
    parts = jnp.concatenate(_split3(lf), axis=1)
    total = jnp.zeros((1, LANES), F32)
    for blk in range(x.shape[0] // BLOCK):
        rows = slice(blk * BLOCK, (blk + 1) * BLOCK)
        c = jnp.dot(tri, parts[rows], preferred_element_type=F32)
        c = c[:, :LANES] + c[:, LANES:2 * LANES] + c[:, 2 * LANES:] + total
        cum_ref[rows, :] = c
        total = c[BLOCK - 1:BLOCK, :]


def _cum_log_forget(f_logit, b_f, batch, seq):
    return pl.pallas_call(
        _cum_kernel,
        grid=(batch,),
        in_specs=[pl.BlockSpec((seq, LANES), lambda b: (b, 0)),
                  pl.BlockSpec((1, LANES), lambda b: (0, 0))],
        out_specs=pl.BlockSpec((seq, LANES), lambda b: (b, 0)),
        out_shape=jax.ShapeDtypeStruct((batch * seq, LANES), F32),
        compiler_params=_params(("parallel",)),
        name="cum_log_forget",
    )(f_logit, b_f)


def _fox_kernel(q_ref, k_ref, v_ref, cq_ref, ck_ref, o_ref, *, tiling):
    tile, pairs = tiling
    i = pl.program_id(2)
    row = lax.broadcasted_iota(jnp.int32, (tile, tile), 0)
    col = lax.broadcasted_iota(jnp.int32, (tile, tile), 1)
    causal = col <= row
    low = lax.broadcasted_iota(jnp.int32, (tile, 2 * LANES), 1) % LANES < HEAD_DIM
    cq_all = cq_ref[0, 0]
    r2 = lax.broadcasted_iota(jnp.int32, (2 * tile, LANES), 0)
    l2 = lax.broadcasted_iota(jnp.int32, (2 * tile, LANES), 1)
    ones = jnp.where((r2 < tile) == (l2 < HEAD_DIM), 1.0, 0.0).astype(BF16)

    def chunk(c, state, mask):
        rows = _chunk_rows(c, tile)
        ck_all = ck_ref[0, c, 0]
        new_state = []
        for pr in range(pairs):
            cols = slice(pr * LANES, (pr + 1) * LANES)
            ms, acc = state[pr]
            q2 = q_ref[:, cols] * SCALE
            s = lax.dot_general(q2, _head_split(k_ref[rows, cols]), _NT,
                                preferred_element_type=F32)
            ps, alphas, ms_new = [], [], []
            for h in range(2):
                hd = 2 * pr + h
                cq = cq_all[:, hd:hd + 1]
                u = s[:, h * tile:(h + 1) * tile] - ck_all[hd:hd + 1, :]
                if mask is not None:
                    u = jnp.where(mask, u, NEG)
                m_new = jnp.maximum(ms[h], jnp.max(u, axis=-1, keepdims=True) + cq)
                alphas.append(jnp.exp(ms[h] - m_new))
                ps.append(jnp.exp(u - (m_new - cq)).astype(BF16))
                ms_new.append(m_new)
            rhs = jnp.concatenate([_head_split(v_ref[rows, cols]), ones], axis=1)
            pv = jnp.dot(jnp.concatenate(ps, axis=1), rhs,
                         preferred_element_type=F32)
            acc = acc * jnp.where(low, alphas[0], alphas[1]) + pv
            new_state.append((tuple(ms_new), acc))
        return tuple(new_state)

    neg = jnp.full((tile, 1), NEG, F32)
    state = tuple(((neg, neg), jnp.zeros((tile, 2 * LANES), F32)) for _ in range(pairs))
    state = chunk(i, state, causal)
    state = lax.fori_loop(0, i, lambda c, st: chunk(c, st, None), state)
    for pr in range(pairs):
        acc = state[pr][1]
        o_ref[:, pr * LANES:(pr + 1) * LANES] = (acc[:, :LANES] / acc[:, LANES:]).astype(o_ref.dtype)


def _fox_attention(proj, cum, batch, seq):
    tile, groups, heads = FOX_TILING.tile, FOX_TILING.groups, FOX_TILING.heads
    nc = seq // tile
    c = cum.reshape(batch, seq, LANES)[:, :, :N_HEADS]
    cq = c.reshape(batch, seq, groups, heads).transpose(0, 2, 1, 3)
    ck = c.reshape(batch, nc, tile, groups, heads).transpose(0, 1, 3, 4, 2)
    specs = [
        pl.BlockSpec((1, 1, tile, heads), lambda b, g, i: (b, g, i, 0)),
        pl.BlockSpec((1, nc, 1, heads, tile), lambda b, g, i: (b, 0, g, 0, 0)),
    ]
    return _causal_attention_call(_fox_kernel, "fox_attention", FOX_TILING, proj, [cq, ck], specs,
                                  [], batch, seq)


def kernel(x, g_pre, g_post, w_in_a, w_out_a, sinks_a, w_in_b, w_out_b, w_in_c, b_f_c, w_out_c):
    batch, seq, d = x.shape
    depth = g_pre.shape[0]
    xf = x.reshape(batch * seq, d)
    for i in range(depth):
        kind, j = i % 3, i // 3
        if kind == 0:
            w = w_in_a[j]
            w = jnp.concatenate([w[:, :BRANCH], w[:, BRANCH + 2 * KV_A:],
                                 w[:, BRANCH:BRANCH + 2 * KV_A]], axis=1).astype(BF16)
            proj = _norm_proj(xf, g_pre[i], w, 1536, BF16)
            o = _swa_attention(proj, sinks_a[j], batch, seq)
            z_block, w_out = 1, w_out_a[j]
        elif kind == 1:
            proj = _norm_proj(xf, g_pre[i], w_in_b[j].astype(BF16), 1024, BF16)
            o = _stick_attention(proj, batch, seq)
            z_block, w_out = 3, w_out_b[j]
        else:
            w = w_in_c[j]
            proj = _norm_proj(xf, g_pre[i], w[:, :4 * BRANCH].astype(BF16), 1024, BF16)
            w_f = jnp.pad(w[:, 4 * BRANCH:], ((0, 0), (0, LANES - N_HEADS))).astype(BF16)
            f_logit = _norm_proj(xf, g_pre[i], w_f, LANES, F32)
            b_f = jnp.pad(b_f_c[j], (0, LANES - N_HEADS)).reshape(1, LANES)
            cum = _cum_log_forget(f_logit, b_f, batch, seq)
            o = _fox_attention(proj, cum, batch, seq)
            z_block, w_out = 3, w_out_c[j]
        xf = _gate_out(o, proj, z_block, w_out.astype(BF16), g_post[i], xf)
    return xf.reshape(batch, seq, d)
```

```python
import functools
from typing import NamedTuple

import jax
import jax.numpy as jnp
import numpy as np
from jax import lax
from jax.experimental import pallas as pl
from jax.experimental.pallas import tpu as pltpu

D_MODEL = 2048
HEAD_DIM = 64
N_HEADS = 32
BRANCH = N_HEADS * HEAD_DIM
N_KV_A = 4
KV_A = N_KV_A * HEAD_DIM
GROUP_A = N_HEADS // N_KV_A
BLOCK = 128
LANES = 128
NORM_EPS = 1e-6
NEG = -1e30
SCALE = HEAD_DIM ** -0.5
N_PAIRS = N_HEADS // 2

F32 = jnp.float32
BF16 = jnp.bfloat16

VMEM_LIMIT = 52 * 1024 * 1024
PROJ_TM, PROJ_TN = 1024, 512
GATE_TM = 512

_NT = (((1,), (1,)), ((), ()))


def _params(sem):
    return pltpu.CompilerParams(dimension_semantics=sem, vmem_limit_bytes=VMEM_LIMIT)


def _norm_proj_kernel(x_ref, g_ref, w_ref, o_ref, h_ref):
    @pl.when(pl.program_id(1) == 0)
    def _():
        x = x_ref[...]
        r = lax.rsqrt(jnp.mean(x * x, axis=-1, keepdims=True) + NORM_EPS)
        h_ref[...] = (x * r * g_ref[...]).astype(BF16)

    o_ref[...] = jnp.dot(h_ref[...], w_ref[...].astype(BF16),
                         preferred_element_type=F32).astype(o_ref.dtype)


def _norm_proj(x, g, w, layer, n, out_dtype, tm=PROJ_TM, tn=PROJ_TN):
    m, d = x.shape
    return pl.pallas_call(
        _norm_proj_kernel,
        grid=(m // tm, n // tn),
        in_specs=[
            pl.BlockSpec((tm, d), lambda i, j: (i, 0)),
            pl.BlockSpec((1, d), lambda i, j: (0, 0)),
            pl.BlockSpec((None, d, tn), lambda i, j: (layer, 0, j)),
        ],
        out_specs=pl.BlockSpec((tm, tn), lambda i, j: (i, j)),
        out_shape=jax.ShapeDtypeStruct((m, n), out_dtype),
        scratch_shapes=[pltpu.VMEM((tm, d), BF16)],
        compiler_params=_params(("parallel", "arbitrary")),
        name="norm_proj",
    )(x, g.reshape(1, d), w)


Z_SPLIT = 4
Z_BLOCK = BRANCH // Z_SPLIT


def _gate_out_kernel(o_ref, *refs):
    z_refs, (w_ref, g_ref, x_ref, out_ref) = refs[:Z_SPLIT], refs[Z_SPLIT:]
    z = jnp.concatenate([r[...] for r in z_refs], axis=1).astype(F32)
    gated = (o_ref[...].astype(F32) * (z * jax.nn.sigmoid(z))).astype(BF16)
    y = jnp.dot(gated, w_ref[...], preferred_element_type=F32)
    r = lax.rsqrt(jnp.mean(y * y, axis=-1, keepdims=True) + NORM_EPS)
    out_ref[...] = x_ref[...] + y * r * g_ref[...]


def _gate_out(o, proj, z_col, w_out, g, x, tm=GATE_TM):
    m, d = x.shape
    z0 = z_col // Z_BLOCK
    z_specs = [pl.BlockSpec((tm, Z_BLOCK), lambda i, c=c: (i, z0 + c)) for c in range(Z_SPLIT)]
    return pl.pallas_call(
        _gate_out_kernel,
        grid=(m // tm,),
        in_specs=[pl.BlockSpec((tm, BRANCH), lambda i: (i, 0))] + z_specs + [
            pl.BlockSpec((BRANCH, d), lambda i: (0, 0)),
            pl.BlockSpec((1, d), lambda i: (0, 0)),
            pl.BlockSpec((tm, d), lambda i: (i, 0)),
        ],
        out_specs=pl.BlockSpec((tm, d), lambda i: (i, 0)),
        out_shape=jax.ShapeDtypeStruct((m, d), F32),
        compiler_params=_params(("parallel",)),
        name="gate_out",
    )(o, *([proj] * Z_SPLIT), w_out, g.reshape(1, d), x)


def _stack_pair(q):
    lane = lax.broadcasted_iota(jnp.int32, q.shape, 1)
    zero = jnp.zeros_like(q)
    return jnp.concatenate([jnp.where(lane < HEAD_DIM, q, zero),
                            jnp.where(lane >= HEAD_DIM, q, zero)], axis=0)


def _alibi_slopes():
    n = N_HEADS
    return (2.0 ** (-8.0 * np.arange(1, n + 1, dtype=np.float32) / n)).astype(np.float32)


def _swa_kernel(sink_ref, q_ref, kp_ref, kc_ref, vp_ref, vc_ref, o_ref, bias_ref, *, nb):
    first = (pl.program_id(0) == 0) & (pl.program_id(1) == 0)

    @pl.when(first)
    def _():
        qi = lax.broadcasted_iota(jnp.int32, (BLOCK, 2 * BLOCK), 0)
        kj = lax.broadcasted_iota(jnp.int32, (BLOCK, 2 * BLOCK), 1)
        dist = qi + BLOCK - kj
        valid = (dist >= 0) & (dist < BLOCK)
        distf = dist.astype(F32)
        slopes = _alibi_slopes()
        for h in range(N_HEADS):
            bias_ref[h] = jnp.where(valid, -float(slopes[h]) * distf, NEG)

    n = pl.program_id(1)
    lane2 = lax.broadcasted_iota(jnp.int32, (1, 2 * BLOCK), 1)
    pen = jnp.where((lane2 < BLOCK) & (n == 0), NEG, 0.0).astype(F32)

    lane = lax.broadcasted_iota(jnp.int32, (2 * BLOCK, LANES), 1)
    low = lane < HEAD_DIM
    low_q = lax.broadcasted_iota(jnp.int32, (BLOCK, LANES), 1) < HEAD_DIM
    for g in range(N_KV_A):
        p, half = divmod(g, 2)
        cols = slice(p * LANES, (p + 1) * LANES)
        k2 = jnp.concatenate([kp_ref[:, cols], kc_ref[:, cols]], axis=0).astype(F32)
        v2 = jnp.concatenate([vp_ref[:, cols], vc_ref[:, cols]], axis=0).astype(F32)
        k2r = pltpu.roll(k2, HEAD_DIM, 1)
        v2r = pltpu.roll(v2, HEAD_DIM, 1)
        own_low = low if half == 0 else jnp.logical_not(low)
        kd = jnp.where(own_low, k2, k2r).astype(BF16)
        vd = jnp.where(own_low, v2, v2r).astype(BF16)

        q = q_ref[:, g * GROUP_A * HEAD_DIM:(g + 1) * GROUP_A * HEAD_DIM] * SCALE
        rows = [_stack_pair(q[:, i * LANES:(i + 1) * LANES]) for i in range(GROUP_A // 2)]
        lhs = jnp.concatenate(rows, axis=0)
        s = lax.dot_general(lhs, kd, _NT, preferred_element_type=F32)
        ps, ls = [], []
        for hh in range(GROUP_A):
            h = g * GROUP_A + hh
            sh = s[hh * BLOCK:(hh + 1) * BLOCK] + bias_ref[h] + pen
            sink = sink_ref[h]
            m = jnp.maximum(jnp.max(sh, axis=-1, keepdims=True), sink)
            e = jnp.exp(sh - m)
            ls.append(jnp.sum(e, axis=-1, keepdims=True) + jnp.exp(sink - m))
            ps.append(e.astype(BF16))
        pv = jnp.dot(jnp.concatenate(ps, axis=0), vd, preferred_element_type=F32)
        for i in range(GROUP_A // 2):
            o0 = pv[(2 * i) * BLOCK:(2 * i + 1) * BLOCK] / ls[2 * i]
            o1 = pv[(2 * i + 1) * BLOCK:(2 * i + 2) * BLOCK] / ls[2 * i + 1]
            c0 = (g * GROUP_A // 2 + i) * LANES
            o_ref[:, c0:c0 + LANES] = jnp.where(low_q, o0, o1).astype(o_ref.dtype)


def _swa_attention(proj, sinks, batch, seq):
    m = proj.shape[0]
    nb = seq // BLOCK
    kcol = BRANCH // KV_A
    vcol = kcol + 1

    def cur(b, n):
        return b * nb + n

    def prev(b, n):
        return b * nb + jnp.maximum(n - 1, 0)

    return pl.pallas_call(
        functools.partial(_swa_kernel, nb=nb),
        grid=(batch, nb),
        in_specs=[
            pl.BlockSpec(memory_space=pltpu.SMEM),
            pl.BlockSpec((BLOCK, BRANCH), lambda b, n: (cur(b, n), 0)),
            pl.BlockSpec((BLOCK, KV_A), lambda b, n: (prev(b, n), kcol)),
            pl.BlockSpec((BLOCK, KV_A), lambda b, n: (cur(b, n), kcol)),
            pl.BlockSpec((BLOCK, KV_A), lambda b, n: (prev(b, n), vcol)),
            pl.BlockSpec((BLOCK, KV_A), lambda b, n: (cur(b, n), vcol)),
        ],
        out_specs=pl.BlockSpec((BLOCK, BRANCH), lambda b, n: (cur(b, n), 0)),
        out_shape=jax.ShapeDtypeStruct((m, BRANCH), BF16),
        scratch_shapes=[pltpu.VMEM((N_HEADS, BLOCK, 2 * BLOCK), F32)],
        compiler_params=_params(("arbitrary", "arbitrary")),
        name="swa_attention",
    )(sinks, proj, proj, proj, proj, proj)


class _Tiling(NamedTuple):
    tile: int
    pairs: int

    @property
    def groups(self):
        return N_PAIRS // self.pairs

    @property
    def heads(self):
        return 2 * self.pairs


STICK_TILING = _Tiling(tile=256, pairs=8)
FOX_TILING = _Tiling(tile=512, pairs=2)


def _head_split(x):
    lane = lax.broadcasted_iota(jnp.int32, x.shape, 1)
    zero = jnp.zeros_like(x)
    return jnp.concatenate([jnp.where(lane < HEAD_DIM, x, zero),
                            jnp.where(lane >= HEAD_DIM, x, zero)], axis=0)


def _chunk_rows(c, tile):
    return pl.ds(pl.multiple_of(c * tile, tile), tile)


def _causal_attention_call(kernel_fn, name, tiling, proj, extra_inputs, extra_specs, scratch,
                           batch, seq):
    m = proj.shape[0]
    tile, groups = tiling.tile, tiling.groups
    nq = seq // tile
    width = tiling.pairs * LANES
    return pl.pallas_call(
        functools.partial(kernel_fn, tiling=tiling),
        grid=(batch, groups, nq),
        in_specs=[
            pl.BlockSpec((tile, width), lambda b, g, i: (b * nq + i, g)),
            pl.BlockSpec((seq, width), lambda b, g, i: (b, groups + g)),
            pl.BlockSpec((seq, width), lambda b, g, i: (b, 2 * groups + g)),
        ] + extra_specs,
        out_specs=pl.BlockSpec((tile, width), lambda b, g, i: (b * nq + i, g)),
        out_shape=jax.ShapeDtypeStruct((m, BRANCH), BF16),
        scratch_shapes=scratch,
        compiler_params=_params(("parallel", "parallel", "arbitrary")),
        name=name,
    )(proj, proj, proj, *extra_inputs)


def _suffix_matrix():
    j = lax.broadcasted_iota(jnp.int32, (BLOCK, 2 * BLOCK), 0)
    s = lax.broadcasted_iota(jnp.int32, (BLOCK, 2 * BLOCK), 1)
    return jnp.where((s >= BLOCK) | (j > s), 1.0, 0.0).astype(BF16)


SOFTPLUS_CLAMP = 60.0


def _softplus(s):
    return jnp.maximum(s, jnp.log(1.0 + jnp.exp(jnp.minimum(s, SOFTPLUS_CLAMP))))


def _stick_kernel(q_ref, k_ref, v_ref, o_ref, acc_ref, carry_ref, *, tiling):
    tile, pairs = tiling
    nkb = tile // BLOCK
    i = pl.program_id(2)
    u = _suffix_matrix()
    row = lax.broadcasted_iota(jnp.int32, (tile, tile), 0)
    col = lax.broadcasted_iota(jnp.int32, (tile, tile), 1)
    before = jnp.concatenate([col < row, col < row], axis=1)

    def chunk(c, mask):
        rows = _chunk_rows(c, tile)
        for pr in range(pairs):
            cols = slice(pr * LANES, (pr + 1) * LANES)
            q2 = q_ref[:, cols] * SCALE
            s = lax.dot_general(q2, _head_split(k_ref[rows, cols]), _NT,
                                preferred_element_type=F32)
            sp = _softplus(s)
            if mask is not None:
                sp = jnp.where(mask, sp, 0.0)
            spb = sp.astype(BF16)
            base = s - sp
            ps = [None] * (2 * nkb)
            for h in range(2):
                carry = carry_ref[pr, h]
                for kb in reversed(range(nkb)):
                    sl = slice(h * tile + kb * BLOCK, h * tile + (kb + 1) * BLOCK)
                    cs = jnp.dot(spb[:, sl], u, preferred_element_type=F32)
                    ps[h * nkb + kb] = jnp.exp(base[:, sl] - cs[:, :BLOCK] - carry)
                    carry = carry + cs[:, BLOCK:]
                carry_ref[pr, h] = carry
            p = jnp.concatenate(ps, axis=1)
            if mask is not None:
                p = jnp.where(mask, p, 0.0)
            acc_ref[pr] += jnp.dot(p.astype(BF16), _head_split(v_ref[rows, cols]),
                                   preferred_element_type=F32)

    acc_ref[...] = jnp.zeros_like(acc_ref)
    carry_ref[...] = jnp.zeros_like(carry_ref)
    chunk(i, before)

    def body(t, _):
        chunk(i - 1 - t, None)
        return 0

    lax.fori_loop(0, i, body, 0)
    for pr in range(pairs):
        o_ref[:, pr * LANES:(pr + 1) * LANES] = acc_ref[pr].astype(o_ref.dtype)


def _stick_attention(proj, batch, seq):
    tile, pairs = STICK_TILING
    scratch = [pltpu.VMEM((pairs, tile, LANES), F32),
               pltpu.VMEM((pairs, 2, tile, LANES), F32)]
    return _causal_attention_call(_stick_kernel, "stick_attention", STICK_TILING, proj, [], [],
                                  scratch, batch, seq)


def _split3(x):
    hi = x.astype(BF16)
    r = x - hi.astype(F32)
    mid = r.astype(BF16)
    lo = (r - mid.astype(F32)).astype(BF16)
    return hi, mid, lo


def _cum_kernel(fl_ref, bf_ref, cum_ref):
    x = fl_ref[...] + bf_ref[...]
    lf = jnp.minimum(x, 0.0) - jnp.log(1.0 + jnp.exp(-jnp.abs(x)))
    t = lax.broadcasted_iota(jnp.int32, (BLOCK, BLOCK), 0)
    j = lax.broadcasted_iota(jnp.int32, (BLOCK, BLOCK), 1)
    tri = jnp.where(j <= t, 1.0, 0.0).astype(BF16)
    parts = jnp.concatenate(_split3(lf), axis=1)
    total = jnp.zeros((1, LANES), F32)
    for blk in range(x.shape[0] // BLOCK):
        rows = slice(blk * BLOCK, (blk + 1) * BLOCK)
        c = jnp.dot(tri, parts[rows], preferred_element_type=F32)
        c = c[:, :LANES] + c[:, LANES:2 * LANES] + c[:, 2 * LANES:] + total
        cum_ref[rows, :] = c
        total = c[BLOCK - 1:BLOCK, :]


def _cum_log_forget(f_logit, b_f, batch, seq):
    return pl.pallas_call(
        _cum_kernel,
        grid=(batch,),
        in_specs=[pl.BlockSpec((seq, LANES), lambda b: (b, 0)),
                  pl.BlockSpec((1, LANES), lambda b: (0, 0))],
        out_specs=pl.BlockSpec((seq, LANES), lambda b: (b, 0)),
        out_shape=jax.ShapeDtypeStruct((batch * seq, LANES), F32),
        compiler_params=_params(("parallel",)),
        name="cum_log_forget",
    )(f_logit, b_f)


def _fox_kernel(q_ref, k_ref, v_ref, cq_ref, ck_ref, o_ref, *, tiling):
    tile, pairs = tiling
    i = pl.program_id(2)
    row = lax.broadcasted_iota(jnp.int32, (tile, tile), 0)
    col = lax.broadcasted_iota(jnp.int32, (tile, tile), 1)
    causal = col <= row
    low = lax.broadcasted_iota(jnp.int32, (tile, 2 * LANES), 1) % LANES < HEAD_DIM
    cq_all = cq_ref[0, 0]
    r2 = lax.broadcasted_iota(jnp.int32, (2 * tile, LANES), 0)
    l2 = lax.broadcasted_iota(jnp.int32, (2 * tile, LANES), 1)
    ones = jnp.where((r2 < tile) == (l2 < HEAD_DIM), 1.0, 0.0).astype(BF16)

    def chunk(c, state, mask):
        rows = _chunk_rows(c, tile)
        ck_all = ck_ref[0, c, 0]
        new_state = []
        for pr in range(pairs):
            cols = slice(pr * LANES, (pr + 1) * LANES)
            ms, acc = state[pr]
            q2 = q_ref[:, cols] * SCALE
            s = lax.dot_general(q2, _head_split(k_ref[rows, cols]), _NT,
                                preferred_element_type=F32)
            ps, alphas, ms_new = [], [], []
            for h in range(2):
                hd = 2 * pr + h
                cq = cq_all[:, hd:hd + 1]
                u = s[:, h * tile:(h + 1) * tile] - ck_all[hd:hd + 1, :]
                if mask is not None:
                    u = jnp.where(mask, u, NEG)
                m_new = jnp.maximum(ms[h], jnp.max(u, axis=-1, keepdims=True) + cq)
                alphas.append(jnp.exp(ms[h] - m_new))
                ps.append(jnp.exp(u - (m_new - cq)).astype(BF16))
                ms_new.append(m_new)
            rhs = jnp.concatenate([_head_split(v_ref[rows, cols]), ones], axis=1)
            pv = jnp.dot(jnp.concatenate(ps, axis=1), rhs,
                         preferred_element_type=F32)
            acc = acc * jnp.where(low, alphas[0], alphas[1]) + pv
            new_state.append((tuple(ms_new), acc))
        return tuple(new_state)

    neg = jnp.full((tile, 1), NEG, F32)
    state = tuple(((neg, neg), jnp.zeros((tile, 2 * LANES), F32)) for _ in range(pairs))
    state = chunk(i, state, causal)
    state = lax.fori_loop(0, i, lambda c, st: chunk(c, st, None), state)
    for pr in range(pairs):
        acc = state[pr][1]
        o_ref[:, pr * LANES:(pr + 1) * LANES] = (acc[:, :LANES] / acc[:, LANES:]).astype(o_ref.dtype)


def _fox_attention(proj, cum, batch, seq):
    tile, groups, heads = FOX_TILING.tile, FOX_TILING.groups, FOX_TILING.heads
    nc = seq // tile
    c = cum.reshape(batch, seq, LANES)[:, :, :N_HEADS]
    cq = c.reshape(batch, seq, groups, heads).transpose(0, 2, 1, 3)
    ck = c.reshape(batch, nc, tile, groups, heads).transpose(0, 1, 3, 4, 2)
    specs = [
        pl.BlockSpec((1, 1, tile, heads), lambda b, g, i: (b, g, i, 0)),
        pl.BlockSpec((1, nc, 1, heads, tile), lambda b, g, i: (b, 0, g, 0, 0)),
    ]
    return _causal_attention_call(_fox_kernel, "fox_attention", FOX_TILING, proj, [cq, ck], specs,
                                  [], batch, seq)


def kernel(x, g_pre, g_post, w_in_a, w_out_a, sinks_a, w_in_b, w_out_b, w_in_c, b_f_c, w_out_c):
    batch, seq, d = x.shape
    depth = g_pre.shape[0]
    xf = x.reshape(batch * seq, d)
    for i in range(depth):
        kind, j = i % 3, i // 3
        if kind == 0:
            proj = _norm_proj(xf, g_pre[i], w_in_a, j, BRANCH + 2 * KV_A + BRANCH, BF16)
            o = _swa_attention(proj, sinks_a[j], batch, seq)
            z_col, w_out = BRANCH + 2 * KV_A, w_out_a[j]
        elif kind == 1:
            proj = _norm_proj(xf, g_pre[i], w_in_b, j, 4 * BRANCH, BF16)
            o = _stick_attention(proj, batch, seq)
            z_col, w_out = 3 * BRANCH, w_out_b[j]
        else:
            proj = _norm_proj(xf, g_pre[i], w_in_c, j, 4 * BRANCH, BF16)
            w_f = jnp.pad(w_in_c[j, :, 4 * BRANCH:], ((0, 0), (0, LANES - N_HEADS)))[None]
            f_logit = _norm_proj(xf, g_pre[i], w_f, 0, LANES, F32, tn=LANES)
            b_f = jnp.pad(b_f_c[j], (0, LANES - N_HEADS)).reshape(1, LANES)
            cum = _cum_log_forget(f_logit, b_f, batch, seq)
            o = _fox_attention(proj, cum, batch, seq)
            z_col, w_out = 3 * BRANCH, w_out_c[j]
        xf = _gate_out(o, proj, z_col, w_out.astype(BF16), g_post[i], xf)
    return xf.reshape(batch, seq, d)
```

```python
import functools
from typing import NamedTuple

import jax
import jax.numpy as jnp
import numpy as np
from jax import lax
from jax.experimental import pallas as pl
from jax.experimental.pallas import tpu as pltpu

D_MODEL = 2048
HEAD_DIM = 64
N_HEADS = 32
BRANCH = N_HEADS * HEAD_DIM
N_KV_A = 4
KV_A = N_KV_A * HEAD_DIM
GROUP_A = N_HEADS // N_KV_A
BLOCK = 128
LANES = 128
NORM_EPS = 1e-6
NEG = -1e30
SCALE = HEAD_DIM ** -0.5
LOG2E = 1.4426950408889634
N_PAIRS = N_HEADS // 2

F32 = jnp.float32
BF16 = jnp.bfloat16

VMEM_LIMIT = 52 * 1024 * 1024
PROJ_TM = 1024
PROJ_TN_A, PROJ_TN = 1536, 1024
GATE_TM = 512

_NT = (((1,), (1,)), ((), ()))


def _params(sem):
    return pltpu.CompilerParams(dimension_semantics=sem, vmem_limit_bytes=VMEM_LIMIT)


def _norm_proj_kernel(x_ref, g_ref, w_ref, o_ref, h_ref):
    @pl.when(pl.program_id(1) == 0)
    def _():
        x = x_ref[...]
        r = lax.rsqrt(jnp.mean(x * x, axis=-1, keepdims=True) + NORM_EPS)
        h_ref[...] = (x * r * g_ref[...]).astype(BF16)

    o_ref[...] = jnp.dot(h_ref[...], w_ref[...], preferred_element_type=F32).astype(o_ref.dtype)


def _norm_proj(x, g, w, layer, n, out_dtype, tn, tm=PROJ_TM):
    m, d = x.shape
    return pl.pallas_call(
        _norm_proj_kernel,
        grid=(m // tm, n // tn),
        in_specs=[
            pl.BlockSpec((tm, d), lambda i, j: (i, 0)),
            pl.BlockSpec((1, d), lambda i, j: (0, 0)),
            pl.BlockSpec((None, d, tn), lambda i, j: (layer, 0, j)),
        ],
        out_specs=pl.BlockSpec((tm, tn), lambda i, j: (i, j)),
        out_shape=jax.ShapeDtypeStruct((m, n), out_dtype),
        scratch_shapes=[pltpu.VMEM((tm, d), BF16)],
        compiler_params=_params(("parallel", "arbitrary")),
        name="norm_proj",
    )(x, g.reshape(1, d), w)


Z_SPLIT = 4
Z_BLOCK = BRANCH // Z_SPLIT


def _gate_out_kernel(o_ref, *refs):
    z_refs, (w_ref, g_ref, x_ref, out_ref) = refs[:Z_SPLIT], refs[Z_SPLIT:]
    z = jnp.concatenate([r[...] for r in z_refs], axis=1).astype(F32)
    gated = (o_ref[...].astype(F32) * (z * jax.nn.sigmoid(z))).astype(BF16)
    y = jnp.dot(gated, w_ref[...], preferred_element_type=F32)
    r = lax.rsqrt(jnp.mean(y * y, axis=-1, keepdims=True) + NORM_EPS)
    out_ref[...] = x_ref[...] + y * r * g_ref[...]


def _gate_out(o, proj, z_col, w_out, g, x, tm=GATE_TM):
    m, d = x.shape
    z0 = z_col // Z_BLOCK
    z_specs = [pl.BlockSpec((tm, Z_BLOCK), lambda i, c=c: (i, z0 + c)) for c in range(Z_SPLIT)]
    return pl.pallas_call(
        _gate_out_kernel,
        grid=(m // tm,),
        in_specs=[pl.BlockSpec((tm, BRANCH), lambda i: (i, 0))] + z_specs + [
            pl.BlockSpec((BRANCH, d), lambda i: (0, 0)),
            pl.BlockSpec((1, d), lambda i: (0, 0)),
            pl.BlockSpec((tm, d), lambda i: (i, 0)),
        ],
        out_specs=pl.BlockSpec((tm, d), lambda i: (i, 0)),
        out_shape=jax.ShapeDtypeStruct((m, d), F32),
        compiler_params=_params(("parallel",)),
        name="gate_out",
    )(o, *([proj] * Z_SPLIT), w_out, g.reshape(1, d), x)


def _stack_pair(q):
    lane = lax.broadcasted_iota(jnp.int32, q.shape, 1)
    zero = jnp.zeros_like(q)
    return jnp.concatenate([jnp.where(lane < HEAD_DIM, q, zero),
                            jnp.where(lane >= HEAD_DIM, q, zero)], axis=0)


def _alibi_slopes():
    n = N_HEADS
    return (2.0 ** (-8.0 * np.arange(1, n + 1, dtype=np.float32) / n)).astype(np.float32)


def _swa_kernel(sink_ref, q_ref, kp_ref, kc_ref, vp_ref, vc_ref, o_ref, bias_ref, *, nb):
    first = (pl.program_id(0) == 0) & (pl.program_id(1) == 0)

    @pl.when(first)
    def _():
        qi = lax.broadcasted_iota(jnp.int32, (BLOCK, 2 * BLOCK), 0)
        kj = lax.broadcasted_iota(jnp.int32, (BLOCK, 2 * BLOCK), 1)
        dist = qi + BLOCK - kj
        valid = (dist >= 0) & (dist < BLOCK)
        distf = dist.astype(F32)
        slopes = _alibi_slopes()
        for h in range(N_HEADS):
            bias_ref[h] = jnp.where(valid, -float(slopes[h]) * distf, NEG)

    n = pl.program_id(1)
    lane2 = lax.broadcasted_iota(jnp.int32, (1, 2 * BLOCK), 1)
    pen = jnp.where((lane2 < BLOCK) & (n == 0), NEG, 0.0).astype(F32)

    lane = lax.broadcasted_iota(jnp.int32, (2 * BLOCK, LANES), 1)
    low = lane < HEAD_DIM
    low_q = lax.broadcasted_iota(jnp.int32, (BLOCK, LANES), 1) < HEAD_DIM
    for g in range(N_KV_A):
        p, half = divmod(g, 2)
        cols = slice(p * LANES, (p + 1) * LANES)
        k2 = jnp.concatenate([kp_ref[:, cols], kc_ref[:, cols]], axis=0).astype(F32)
        v2 = jnp.concatenate([vp_ref[:, cols], vc_ref[:, cols]], axis=0).astype(F32)
        k2r = pltpu.roll(k2, HEAD_DIM, 1)
        v2r = pltpu.roll(v2, HEAD_DIM, 1)
        own_low = low if half == 0 else jnp.logical_not(low)
        kd = jnp.where(own_low, k2, k2r).astype(BF16)
        vd = jnp.where(own_low, v2, v2r).astype(BF16)

        q = q_ref[:, g * GROUP_A * HEAD_DIM:(g + 1) * GROUP_A * HEAD_DIM] * SCALE
        rows = [_stack_pair(q[:, i * LANES:(i + 1) * LANES]) for i in range(GROUP_A // 2)]
        lhs = jnp.concatenate(rows, axis=0)
        s = lax.dot_general(lhs, kd, _NT, preferred_element_type=F32)
        ps, ls = [], []
        for hh in range(GROUP_A):
            h = g * GROUP_A + hh
            sh = s[hh * BLOCK:(hh + 1) * BLOCK] + bias_ref[h] + pen
            sink = sink_ref[h]
            m = jnp.maximum(jnp.max(sh, axis=-1, keepdims=True), sink)
            e = jnp.exp(sh - m)
            ls.append(jnp.sum(e, axis=-1, keepdims=True) + jnp.exp(sink - m))
            ps.append(e.astype(BF16))
        pv = jnp.dot(jnp.concatenate(ps, axis=0), vd, preferred_element_type=F32)
        for i in range(GROUP_A // 2):
            o0 = pv[(2 * i) * BLOCK:(2 * i + 1) * BLOCK] / ls[2 * i]
            o1 = pv[(2 * i + 1) * BLOCK:(2 * i + 2) * BLOCK] / ls[2 * i + 1]
            c0 = (g * GROUP_A // 2 + i) * LANES
            o_ref[:, c0:c0 + LANES] = jnp.where(low_q, o0, o1).astype(o_ref.dtype)


def _swa_attention(proj, sinks, batch, seq):
    m = proj.shape[0]
    nb = seq // BLOCK
    kcol = BRANCH // KV_A
    vcol = kcol + 1

    def cur(b, n):
        return b * nb + n

    def prev(b, n):
        return b * nb + jnp.maximum(n - 1, 0)

    return pl.pallas_call(
        functools.partial(_swa_kernel, nb=nb),
        grid=(batch, nb),
        in_specs=[
            pl.BlockSpec(memory_space=pltpu.SMEM),
            pl.BlockSpec((BLOCK, BRANCH), lambda b, n: (cur(b, n), 0)),
            pl.BlockSpec((BLOCK, KV_A), lambda b, n: (prev(b, n), kcol)),
            pl.BlockSpec((BLOCK, KV_A), lambda b, n: (cur(b, n), kcol)),
            pl.BlockSpec((BLOCK, KV_A), lambda b, n: (prev(b, n), vcol)),
            pl.BlockSpec((BLOCK, KV_A), lambda b, n: (cur(b, n), vcol)),
        ],
        out_specs=pl.BlockSpec((BLOCK, BRANCH), lambda b, n: (cur(b, n), 0)),
        out_shape=jax.ShapeDtypeStruct((m, BRANCH), BF16),
        scratch_shapes=[pltpu.VMEM((N_HEADS, BLOCK, 2 * BLOCK), F32)],
        compiler_params=_params(("arbitrary", "arbitrary")),
        name="swa_attention",
    )(sinks, proj, proj, proj, proj, proj)


class _Tiling(NamedTuple):
    tile: int
    pairs: int

    @property
    def groups(self):
        return N_PAIRS // self.pairs

    @property
    def heads(self):
        return 2 * self.pairs


STICK_TILING = _Tiling(tile=256, pairs=8)
FOX_TILING = _Tiling(tile=512, pairs=2)


def _head_split(x):
    lane = lax.broadcasted_iota(jnp.int32, x.shape, 1)
    zero = jnp.zeros_like(x)
    return jnp.concatenate([jnp.where(lane < HEAD_DIM, x, zero),
                            jnp.where(lane >= HEAD_DIM, x, zero)], axis=0)


def _chunk_rows(c, tile):
    return pl.ds(pl.multiple_of(c * tile, tile), tile)


def _causal_attention_call(kernel_fn, name, tiling, proj, extra_inputs, extra_specs, scratch,
                           batch, seq):
    m = proj.shape[0]
    tile, groups = tiling.tile, tiling.groups
    nq = seq // tile
    width = tiling.pairs * LANES
    return pl.pallas_call(
        functools.partial(kernel_fn, tiling=tiling),
        grid=(batch, groups, nq),
        in_specs=[
            pl.BlockSpec((tile, width), lambda b, g, i: (b * nq + i, g)),
            pl.BlockSpec((seq, width), lambda b, g, i: (b, groups + g)),
            pl.BlockSpec((seq, width), lambda b, g, i: (b, 2 * groups + g)),
        ] + extra_specs,
        out_specs=pl.BlockSpec((tile, width), lambda b, g, i: (b * nq + i, g)),
        out_shape=jax.ShapeDtypeStruct((m, BRANCH), BF16),
        scratch_shapes=scratch,
        compiler_params=_params(("parallel", "parallel", "arbitrary")),
        name=name,
    )(proj, proj, proj, *extra_inputs)


def _suffix_matrix():
    j = lax.broadcasted_iota(jnp.int32, (BLOCK, 2 * BLOCK), 0)
    s = lax.broadcasted_iota(jnp.int32, (BLOCK, 2 * BLOCK), 1)
    return jnp.where((s >= BLOCK) | (j > s), 1.0, 0.0).astype(BF16)


SOFTPLUS_CLAMP = 60.0


def _softplus(s):
    return jnp.maximum(s, jnp.log(1.0 + jnp.exp(jnp.minimum(s, SOFTPLUS_CLAMP))))


def _stick_kernel(q_ref, k_ref, v_ref, o_ref, acc_ref, carry_ref, *, tiling):
    tile, pairs = tiling
    nkb = tile // BLOCK
    i = pl.program_id(2)
    u = _suffix_matrix()
    row = lax.broadcasted_iota(jnp.int32, (tile, tile), 0)
    col = lax.broadcasted_iota(jnp.int32, (tile, tile), 1)
    before = jnp.concatenate([col < row, col < row], axis=1)

    def chunk(c, mask):
        rows = _chunk_rows(c, tile)
        for pr in range(pairs):
            cols = slice(pr * LANES, (pr + 1) * LANES)
            q2 = q_ref[:, cols] * SCALE
            s = lax.dot_general(q2, _head_split(k_ref[rows, cols]), _NT,
                                preferred_element_type=F32)
            sp = _softplus(s)
            if mask is not None:
                sp = jnp.where(mask, sp, 0.0)
            spb = sp.astype(BF16)
            base = s - sp
            ps = [None] * (2 * nkb)
            for h in range(2):
                carry = carry_ref[pr, h]
                for kb in reversed(range(nkb)):
                    sl = slice(h * tile + kb * BLOCK, h * tile + (kb + 1) * BLOCK)
                    cs = jnp.dot(spb[:, sl], u, preferred_element_type=F32)
                    ps[h * nkb + kb] = jnp.exp(base[:, sl] - cs[:, :BLOCK] - carry)
                    carry = carry + cs[:, BLOCK:]
                carry_ref[pr, h] = carry
            p = jnp.concatenate(ps, axis=1)
            if mask is not None:
                p = jnp.where(mask, p, 0.0)
            acc_ref[pr] += jnp.dot(p.astype(BF16), _head_split(v_ref[rows, cols]),
                                   preferred_element_type=F32)

    acc_ref[...] = jnp.zeros_like(acc_ref)
    carry_ref[...] = jnp.zeros_like(carry_ref)
    chunk(i, before)

    def body(t, _):
        chunk(i - 1 - t, None)
        return 0

    lax.fori_loop(0, i, body, 0)
    for pr in range(pairs):
        o_ref[:, pr * LANES:(pr + 1) * LANES] = acc_ref[pr].astype(o_ref.dtype)


def _stick_attention(proj, batch, seq):
    tile, pairs = STICK_TILING
    scratch = [pltpu.VMEM((pairs, tile, LANES), F32),
               pltpu.VMEM((pairs, 2, tile, LANES), F32)]
    return _causal_attention_call(_stick_kernel, "stick_attention", STICK_TILING, proj, [], [],
                                  scratch, batch, seq)


def _split3(x):
    hi = x.astype(BF16)
    r = x - hi.astype(F32)
    mid = r.astype(BF16)
    lo = (r - mid.astype(F32)).astype(BF16)
    return hi, mid, lo


def _cum_kernel(fl_ref, bf_ref, cum_ref):
    x = fl_ref[...] + bf_ref[...]
    lf = jnp.minimum(x, 0.0) - jnp.log(1.0 + jnp.exp(-jnp.abs(x)))
    t = lax.broadcasted_iota(jnp.int32, (BLOCK, BLOCK), 0)
    j = lax.broadcasted_iota(jnp.int32, (BLOCK, BLOCK), 1)
    tri = jnp.where(j <= t, 1.0, 0.0).astype(BF16)
    parts = jnp.concatenate(_split3(lf), axis=1)
    total = jnp.zeros((1, LANES), F32)
    for blk in range(x.shape[0] // BLOCK):
        rows = slice(blk * BLOCK, (blk + 1) * BLOCK)
        c = jnp.dot(tri, parts[rows], preferred_element_type=F32)
        c = c[:, :LANES] + c[:, LANES:2 * LANES] + c[:, 2 * LANES:] + total
        cum_ref[rows, :] = c
        total = c[BLOCK - 1:BLOCK, :]


def _cum_log_forget(f_logit, b_f, batch, seq):
    return pl.pallas_call(
        _cum_kernel,
        grid=(batch,),
        in_specs=[pl.BlockSpec((seq, LANES), lambda b: (b, 0)),
                  pl.BlockSpec((1, LANES), lambda b: (0, 0))],
        out_specs=pl.BlockSpec((seq, LANES), lambda b: (b, 0)),
        out_shape=jax.ShapeDtypeStruct((batch * seq, LANES), F32),
        compiler_params=_params(("parallel",)),
        name="cum_log_forget",
    )(f_logit, b_f)


def _fox_kernel(q_ref, k_ref, v_ref, cq_ref, ck_ref, o_ref, u_ref, acc_ref, *, tiling):
    tile, pairs = tiling
    nl = tile // LANES
    i = pl.program_id(2)
    row = lax.broadcasted_iota(jnp.int32, (tile, tile), 0)
    col = lax.broadcasted_iota(jnp.int32, (tile, tile), 1)
    causal = col <= row
    cq_all = cq_ref[0, 0]
    r2 = lax.broadcasted_iota(jnp.int32, (2 * tile, LANES), 0)
    l2 = lax.broadcasted_iota(jnp.int32, (2 * tile, LANES), 1)
    ones = jnp.where((r2 < tile) == (l2 < HEAD_DIM), 1.0, 0.0).astype(BF16)

    def scores(c, tops, mask):
        rows = _chunk_rows(c, tile)
        ck_all = ck_ref[0, c, 0]
        new_tops = []
        for pr in range(pairs):
            cols = slice(pr * LANES, (pr + 1) * LANES)
            q2 = q_ref[:, cols] * SCALE
            s = lax.dot_general(q2, _head_split(k_ref[rows, cols]), _NT,
                                preferred_element_type=F32)
            for h in range(2):
                hd = 2 * pr + h
                u = (s[:, h * tile:(h + 1) * tile] - ck_all[hd:hd + 1, :]) * LOG2E
                if mask is not None:
                    u = jnp.where(mask, u, NEG)
                u_ref[pr, c, :, h * tile:(h + 1) * tile] = u
                top = tops[hd]
                for b in range(nl):
                    top = jnp.maximum(top, u[:, b * LANES:(b + 1) * LANES])
                new_tops.append(top)
        return tuple(new_tops)

    tops = tuple(jnp.full((tile, LANES), NEG, F32) for _ in range(2 * pairs))
    tops = scores(i, tops, causal)
    tops = lax.fori_loop(0, i, lambda c, t: scores(c, t, None), tops)

    shifts = []
    for hd in range(2 * pairs):
        cq = cq_all[:, hd:hd + 1] * LOG2E
        m = jnp.max(tops[hd], axis=-1, keepdims=True) + cq
        shifts.append(jnp.concatenate([jnp.broadcast_to(m - cq, (tile, LANES))] * nl, axis=1))

    acc_ref[...] = jnp.zeros_like(acc_ref)

    def weigh(c, _):
        rows = _chunk_rows(c, tile)
        for pr in range(pairs):
            cols = slice(pr * LANES, (pr + 1) * LANES)
            ps = [jnp.exp2(u_ref[pr, c, :, h * tile:(h + 1) * tile] - shifts[2 * pr + h]).astype(BF16)
                  for h in range(2)]
            rhs = jnp.concatenate([_head_split(v_ref[rows, cols]), ones], axis=1)
            acc_ref[pr] += jnp.dot(jnp.concatenate(ps, axis=1), rhs,
                                   preferred_element_type=F32)
        return 0

    lax.fori_loop(0, i + 1, weigh, 0)
    for pr in range(pairs):
        acc = acc_ref[pr]
        o_ref[:, pr * LANES:(pr + 1) * LANES] = (acc[:, :LANES] / acc[:, LANES:]).astype(o_ref.dtype)


def _fox_attention(proj, cum, batch, seq):
    tile, groups, heads = FOX_TILING.tile, FOX_TILING.groups, FOX_TILING.heads
    nc = seq // tile
    c = cum.reshape(batch, seq, LANES)[:, :, :N_HEADS]
    cq = c.reshape(batch, seq, groups, heads).transpose(0, 2, 1, 3)
    ck = c.reshape(batch, nc, tile, groups, heads).transpose(0, 1, 3, 4, 2)
    specs = [
        pl.BlockSpec((1, 1, tile, heads), lambda b, g, i: (b, g, i, 0)),
        pl.BlockSpec((1, nc, 1, heads, tile), lambda b, g, i: (b, 0, g, 0, 0)),
    ]
    pairs = FOX_TILING.pairs
    scratch = [pltpu.VMEM((pairs, nc, tile, 2 * tile), F32),
               pltpu.VMEM((pairs, tile, 2 * LANES), F32)]
    return _causal_attention_call(_fox_kernel, "fox_attention", FOX_TILING, proj, [cq, ck], specs,
                                  scratch, batch, seq)


def kernel(x, g_pre, g_post, w_in_a, w_out_a, sinks_a, w_in_b, w_out_b, w_in_c, b_f_c, w_out_c):
    batch, seq, d = x.shape
    depth = g_pre.shape[0]
    xf = x.reshape(batch * seq, d)
    wa, wb, wc = w_in_a.astype(BF16), w_in_b.astype(BF16), w_in_c.astype(BF16)
    for i in range(depth):
        kind, j = i % 3, i // 3
        if kind == 0:
            proj = _norm_proj(xf, g_pre[i], wa, j, BRANCH + 2 * KV_A + BRANCH, BF16, PROJ_TN_A)
            o = _swa_attention(proj, sinks_a[j], batch, seq)
            z_col, w_out = BRANCH + 2 * KV_A, w_out_a[j]
        elif kind == 1:
            proj = _norm_proj(xf, g_pre[i], wb, j, 4 * BRANCH, BF16, PROJ_TN)
            o = _stick_attention(proj, batch, seq)
            z_col, w_out = 3 * BRANCH, w_out_b[j]
        else:
            proj = _norm_proj(xf, g_pre[i], wc, j, 4 * BRANCH, BF16, PROJ_TN)
            w_f = jnp.pad(wc[j, :, 4 * BRANCH:], ((0, 0), (0, LANES - N_HEADS)))[None]
            f_logit = _norm_proj(xf, g_pre[i], w_f, 0, LANES, F32, LANES)
            b_f = jnp.pad(b_f_c[j], (0, LANES - N_HEADS)).reshape(1, LANES)
            cum = _cum_log_forget(f_logit, b_f, batch, seq)
            o = _fox_attention(proj, cum, batch, seq)
            z_col, w_out = 3 * BRANCH, w_out_c[j]
        xf = _gate_out(o, proj, z_col, w_out.astype(BF16), g_post[i], xf)
    return xf.reshape(batch, seq, d)
```

```python
import functools
from typing import NamedTuple

import jax
import jax.numpy as jnp
import numpy as np
from jax import lax
from jax.experimental import pallas as pl
from jax.experimental.pallas import tpu as pltpu

D_MODEL = 2048
HEAD_DIM = 64
N_HEADS = 32
BRANCH = N_HEADS * HEAD_DIM
N_KV_A = 4
KV_A = N_KV_A * HEAD_DIM
GROUP_A = N_HEADS // N_KV_A
BLOCK = 128
LANES = 128
NORM_EPS = 1e-6
NEG = -1e30
SCALE = HEAD_DIM ** -0.5
LOG2E = 1.4426950408889634
N_PAIRS = N_HEADS // 2

F32 = jnp.float32
BF16 = jnp.bfloat16

VMEM_LIMIT = 52 * 1024 * 1024
PROJ_TM = 1024
PROJ_TN_A, PROJ_TN = 1536, 1024
GATE_TM = 512

_NT = (((1,), (1,)), ((), ()))


def _params(sem):
    return pltpu.CompilerParams(dimension_semantics=sem, vmem_limit_bytes=VMEM_LIMIT)


def _norm_proj_kernel(x_ref, g_ref, w_ref, o_ref, h_ref):
    @pl.when(pl.program_id(1) == 0)
    def _():
        x = x_ref[...]
        r = lax.rsqrt(jnp.mean(x * x, axis=-1, keepdims=True) + NORM_EPS)
        h_ref[...] = (x * r * g_ref[...]).astype(BF16)

    o_ref[...] = jnp.dot(h_ref[...], w_ref[...], preferred_element_type=F32).astype(o_ref.dtype)


def _norm_proj(x, g, w, layer, n, out_dtype, tn, tm=PROJ_TM):
    m, d = x.shape
    return pl.pallas_call(
        _norm_proj_kernel,
        grid=(m // tm, n // tn),
        in_specs=[
            pl.BlockSpec((tm, d), lambda i, j: (i, 0)),
            pl.BlockSpec((1, d), lambda i, j: (0, 0)),
            pl.BlockSpec((None, d, tn), lambda i, j: (layer, 0, j)),
        ],
        out_specs=pl.BlockSpec((tm, tn), lambda i, j: (i, j)),
        out_shape=jax.ShapeDtypeStruct((m, n), out_dtype),
        scratch_shapes=[pltpu.VMEM((tm, d), BF16)],
        compiler_params=_params(("parallel", "arbitrary")),
        name="norm_proj",
    )(x, g.reshape(1, d), w)


Z_SPLIT = 4
Z_BLOCK = BRANCH // Z_SPLIT


def _gate_out_kernel(o_ref, *refs):
    z_refs, (w_ref, g_ref, x_ref, out_ref) = refs[:Z_SPLIT], refs[Z_SPLIT:]
    z = jnp.concatenate([r[...] for r in z_refs], axis=1).astype(F32)
    gated = (o_ref[...].astype(F32) * (z * jax.nn.sigmoid(z))).astype(BF16)
    y = jnp.dot(gated, w_ref[...], preferred_element_type=F32)
    r = lax.rsqrt(jnp.mean(y * y, axis=-1, keepdims=True) + NORM_EPS)
    out_ref[...] = x_ref[...] + y * r * g_ref[...]


def _gate_out(o, proj, z_col, w_out, g, x, tm=GATE_TM):
    m, d = x.shape
    z0 = z_col // Z_BLOCK
    z_specs = [pl.BlockSpec((tm, Z_BLOCK), lambda i, c=c: (i, z0 + c)) for c in range(Z_SPLIT)]
    return pl.pallas_call(
        _gate_out_kernel,
        grid=(m // tm,),
        in_specs=[pl.BlockSpec((tm, BRANCH), lambda i: (i, 0))] + z_specs + [
            pl.BlockSpec((BRANCH, d), lambda i: (0, 0)),
            pl.BlockSpec((1, d), lambda i: (0, 0)),
            pl.BlockSpec((tm, d), lambda i: (i, 0)),
        ],
        out_specs=pl.BlockSpec((tm, d), lambda i: (i, 0)),
        out_shape=jax.ShapeDtypeStruct((m, d), F32),
        compiler_params=_params(("parallel",)),
        name="gate_out",
    )(o, *([proj] * Z_SPLIT), w_out, g.reshape(1, d), x)


def _stack_pair(q):
    lane = lax.broadcasted_iota(jnp.int32, q.shape, 1)
    zero = jnp.zeros_like(q)
    return jnp.concatenate([jnp.where(lane < HEAD_DIM, q, zero),
                            jnp.where(lane >= HEAD_DIM, q, zero)], axis=0)


def _alibi_slopes():
    n = N_HEADS
    return (2.0 ** (-8.0 * np.arange(1, n + 1, dtype=np.float32) / n)).astype(np.float32)


def _swa_kernel(sink_ref, q_ref, kp_ref, kc_ref, vp_ref, vc_ref, o_ref, bias_ref, *, nb):
    first = (pl.program_id(0) == 0) & (pl.program_id(1) == 0)

    @pl.when(first)
    def _():
        qi = lax.broadcasted_iota(jnp.int32, (BLOCK, 2 * BLOCK), 0)
        kj = lax.broadcasted_iota(jnp.int32, (BLOCK, 2 * BLOCK), 1)
        dist = qi + BLOCK - kj
        valid = (dist >= 0) & (dist < BLOCK)
        distf = dist.astype(F32)
        slopes = _alibi_slopes()
        for h in range(N_HEADS):
            bias_ref[h] = jnp.where(valid, -float(slopes[h]) * distf, NEG)

    n = pl.program_id(1)
    lane2 = lax.broadcasted_iota(jnp.int32, (1, 2 * BLOCK), 1)
    pen = jnp.where((lane2 < BLOCK) & (n == 0), NEG, 0.0).astype(F32)

    lane = lax.broadcasted_iota(jnp.int32, (2 * BLOCK, LANES), 1)
    low = lane < HEAD_DIM
    low_q = lax.broadcasted_iota(jnp.int32, (BLOCK, LANES), 1) < HEAD_DIM
    for g in range(N_KV_A):
        p, half = divmod(g, 2)
        cols = slice(p * LANES, (p + 1) * LANES)
        k2 = jnp.concatenate([kp_ref[:, cols], kc_ref[:, cols]], axis=0).astype(F32)
        v2 = jnp.concatenate([vp_ref[:, cols], vc_ref[:, cols]], axis=0).astype(F32)
        k2r = pltpu.roll(k2, HEAD_DIM, 1)
        v2r = pltpu.roll(v2, HEAD_DIM, 1)
        own_low = low if half == 0 else jnp.logical_not(low)
        kd = jnp.where(own_low, k2, k2r).astype(BF16)
        vd = jnp.where(own_low, v2, v2r).astype(BF16)

        q = q_ref[:, g * GROUP_A * HEAD_DIM:(g + 1) * GROUP_A * HEAD_DIM] * SCALE
        rows = [_stack_pair(q[:, i * LANES:(i + 1) * LANES]) for i in range(GROUP_A // 2)]
        lhs = jnp.concatenate(rows, axis=0)
        s = lax.dot_general(lhs, kd, _NT, preferred_element_type=F32)
        ps, ls = [], []
        for hh in range(GROUP_A):
            h = g * GROUP_A + hh
            sh = s[hh * BLOCK:(hh + 1) * BLOCK] + bias_ref[h] + pen
            sink = sink_ref[h]
            m = jnp.maximum(jnp.max(sh, axis=-1, keepdims=True), sink)
            e = jnp.exp(sh - m)
            ls.append(jnp.sum(e, axis=-1, keepdims=True) + jnp.exp(sink - m))
            ps.append(e.astype(BF16))
        pv = jnp.dot(jnp.concatenate(ps, axis=0), vd, preferred_element_type=F32)
        for i in range(GROUP_A // 2):
            o0 = pv[(2 * i) * BLOCK:(2 * i + 1) * BLOCK] / ls[2 * i]
            o1 = pv[(2 * i + 1) * BLOCK:(2 * i + 2) * BLOCK] / ls[2 * i + 1]
            c0 = (g * GROUP_A // 2 + i) * LANES
            o_ref[:, c0:c0 + LANES] = jnp.where(low_q, o0, o1).astype(o_ref.dtype)


def _swa_attention(proj, sinks, batch, seq):
    m = proj.shape[0]
    nb = seq // BLOCK
    kcol = BRANCH // KV_A
    vcol = kcol + 1

    def cur(b, n):
        return b * nb + n

    def prev(b, n):
        return b * nb + jnp.maximum(n - 1, 0)

    return pl.pallas_call(
        functools.partial(_swa_kernel, nb=nb),
        grid=(batch, nb),
        in_specs=[
            pl.BlockSpec(memory_space=pltpu.SMEM),
            pl.BlockSpec((BLOCK, BRANCH), lambda b, n: (cur(b, n), 0)),
            pl.BlockSpec((BLOCK, KV_A), lambda b, n: (prev(b, n), kcol)),
            pl.BlockSpec((BLOCK, KV_A), lambda b, n: (cur(b, n), kcol)),
            pl.BlockSpec((BLOCK, KV_A), lambda b, n: (prev(b, n), vcol)),
            pl.BlockSpec((BLOCK, KV_A), lambda b, n: (cur(b, n), vcol)),
        ],
        out_specs=pl.BlockSpec((BLOCK, BRANCH), lambda b, n: (cur(b, n), 0)),
        out_shape=jax.ShapeDtypeStruct((m, BRANCH), BF16),
        scratch_shapes=[pltpu.VMEM((N_HEADS, BLOCK, 2 * BLOCK), F32)],
        compiler_params=_params(("arbitrary", "arbitrary")),
        name="swa_attention",
    )(sinks, proj, proj, proj, proj, proj)


class _Tiling(NamedTuple):
    tile: int
    pairs: int

    @property
    def groups(self):
        return N_PAIRS // self.pairs

    @property
    def heads(self):
        return 2 * self.pairs


STICK_TILING = _Tiling(tile=256, pairs=8)
FOX_TILING = _Tiling(tile=512, pairs=2)


def _head_split(x):
    lane = lax.broadcasted_iota(jnp.int32, x.shape, 1)
    zero = jnp.zeros_like(x)
    return jnp.concatenate([jnp.where(lane < HEAD_DIM, x, zero),
                            jnp.where(lane >= HEAD_DIM, x, zero)], axis=0)


def _chunk_rows(c, tile):
    return pl.ds(pl.multiple_of(c * tile, tile), tile)


def _causal_attention_call(kernel_fn, name, tiling, proj, extra_inputs, extra_specs, scratch,
                           batch, seq):
    m = proj.shape[0]
    tile, groups = tiling.tile, tiling.groups
    nq = seq // tile
    width = tiling.pairs * LANES
    return pl.pallas_call(
        functools.partial(kernel_fn, tiling=tiling),
        grid=(batch, groups, nq),
        in_specs=[
            pl.BlockSpec((tile, width), lambda b, g, i: (b * nq + i, g)),
            pl.BlockSpec((seq, width), lambda b, g, i: (b, groups + g)),
            pl.BlockSpec((seq, width), lambda b, g, i: (b, 2 * groups + g)),
        ] + extra_specs,
        out_specs=pl.BlockSpec((tile, width), lambda b, g, i: (b * nq + i, g)),
        out_shape=jax.ShapeDtypeStruct((m, BRANCH), BF16),
        scratch_shapes=scratch,
        compiler_params=_params(("parallel", "parallel", "arbitrary")),
        name=name,
    )(proj, proj, proj, *extra_inputs)


def _suffix_matrix():
    j = lax.broadcasted_iota(jnp.int32, (BLOCK, 2 * BLOCK), 0)
    s = lax.broadcasted_iota(jnp.int32, (BLOCK, 2 * BLOCK), 1)
    return jnp.where((s >= BLOCK) | (j > s), 1.0, 0.0).astype(BF16)


SOFTPLUS_CLAMP = 60.0
EXP_UNDERFLOW = 105.0


def _softplus(s):
    return jnp.maximum(s, jnp.log(1.0 + jnp.exp(jnp.minimum(s, SOFTPLUS_CLAMP))))


def _stick_kernel(q_ref, k_ref, v_ref, o_ref, acc_ref, carry_ref, *, tiling):
    tile, pairs = tiling
    nkb = tile // BLOCK
    i = pl.program_id(2)
    u = _suffix_matrix()
    row = lax.broadcasted_iota(jnp.int32, (tile, tile), 0)
    col = lax.broadcasted_iota(jnp.int32, (tile, tile), 1)
    before = jnp.concatenate([col < row, col < row], axis=1)

    def chunk(c, mask):
        rows = _chunk_rows(c, tile)
        for pr in range(pairs):
            cols = slice(pr * LANES, (pr + 1) * LANES)
            q2 = q_ref[:, cols] * SCALE
            s = lax.dot_general(q2, _head_split(k_ref[rows, cols]), _NT,
                                preferred_element_type=F32)
            sp = _softplus(s)
            if mask is not None:
                sp = jnp.where(mask, sp, 0.0)
            spb = sp.astype(BF16)
            base = s - sp
            ps = [None] * (2 * nkb)
            for h in range(2):
                carry = carry_ref[pr, h]
                for kb in reversed(range(nkb)):
                    sl = slice(h * tile + kb * BLOCK, h * tile + (kb + 1) * BLOCK)
                    cs = jnp.dot(spb[:, sl], u, preferred_element_type=F32)
                    ps[h * nkb + kb] = jnp.exp(base[:, sl] - cs[:, :BLOCK] - carry)
                    carry = carry + cs[:, BLOCK:]
                carry_ref[pr, h] = carry
            p = jnp.concatenate(ps, axis=1)
            if mask is not None:
                p = jnp.where(mask, p, 0.0)
            acc_ref[pr] += jnp.dot(p.astype(BF16), _head_split(v_ref[rows, cols]),
                                   preferred_element_type=F32)

    acc_ref[...] = jnp.zeros_like(acc_ref)
    carry_ref[...] = jnp.zeros_like(carry_ref)
    chunk(i, before)

    def weights_vanish():
        return (jnp.min(carry_ref[...]) >= EXP_UNDERFLOW).astype(jnp.int32)

    def cond(st):
        t, done = st
        return jnp.logical_and(t < i, done == 0)

    def body(st):
        t, _ = st
        chunk(i - 1 - t, None)
        return t + 1, weights_vanish()

    lax.while_loop(cond, body, (jnp.int32(0), weights_vanish()))
    for pr in range(pairs):
        o_ref[:, pr * LANES:(pr + 1) * LANES] = acc_ref[pr].astype(o_ref.dtype)


def _stick_attention(proj, batch, seq):
    tile, pairs = STICK_TILING
    scratch = [pltpu.VMEM((pairs, tile, LANES), F32),
               pltpu.VMEM((pairs, 2, tile, LANES), F32)]
    return _causal_attention_call(_stick_kernel, "stick_attention", STICK_TILING, proj, [], [],
                                  scratch, batch, seq)


def _split3(x):
    hi = x.astype(BF16)
    r = x - hi.astype(F32)
    mid = r.astype(BF16)
    lo = (r - mid.astype(F32)).astype(BF16)
    return hi, mid, lo


def _cum_kernel(fl_ref, bf_ref, cum_ref):
    x = fl_ref[...] + bf_ref[...]
    lf = jnp.minimum(x, 0.0) - jnp.log(1.0 + jnp.exp(-jnp.abs(x)))
    t = lax.broadcasted_iota(jnp.int32, (BLOCK, BLOCK), 0)
    j = lax.broadcasted_iota(jnp.int32, (BLOCK, BLOCK), 1)
    tri = jnp.where(j <= t, 1.0, 0.0).astype(BF16)
    parts = jnp.concatenate(_split3(lf), axis=1)
    total = jnp.zeros((1, LANES), F32)
    for blk in range(x.shape[0] // BLOCK):
        rows = slice(blk * BLOCK, (blk + 1) * BLOCK)
        c = jnp.dot(tri, parts[rows], preferred_element_type=F32)
        c = c[:, :LANES] + c[:, LANES:2 * LANES] + c[:, 2 * LANES:] + total
        cum_ref[rows, :] = c
        total = c[BLOCK - 1:BLOCK, :]


def _cum_log_forget(f_logit, b_f, batch, seq):
    return pl.pallas_call(
        _cum_kernel,
        grid=(batch,),
        in_specs=[pl.BlockSpec((seq, LANES), lambda b: (b, 0)),
                  pl.BlockSpec((1, LANES), lambda b: (0, 0))],
        out_specs=pl.BlockSpec((seq, LANES), lambda b: (b, 0)),
        out_shape=jax.ShapeDtypeStruct((batch * seq, LANES), F32),
        compiler_params=_params(("parallel",)),
        name="cum_log_forget",
    )(f_logit, b_f)


def _fox_kernel(q_ref, k_ref, v_ref, cq_ref, ck_ref, o_ref, u_ref, acc_ref, *, tiling):
    tile, pairs = tiling
    nl = tile // LANES
    i = pl.program_id(2)
    row = lax.broadcasted_iota(jnp.int32, (tile, tile), 0)
    col = lax.broadcasted_iota(jnp.int32, (tile, tile), 1)
    causal = col <= row
    cq_all = cq_ref[0, 0]
    r2 = lax.broadcasted_iota(jnp.int32, (2 * tile, LANES), 0)
    l2 = lax.broadcasted_iota(jnp.int32, (2 * tile, LANES), 1)
    ones = jnp.where((r2 < tile) == (l2 < HEAD_DIM), 1.0, 0.0).astype(BF16)

    def scores(c, tops, mask):
        rows = _chunk_rows(c, tile)
        ck_all = ck_ref[0, c, 0]
        new_tops = []
        for pr in range(pairs):
            cols = slice(pr * LANES, (pr + 1) * LANES)
            q2 = q_ref[:, cols] * SCALE
            s = lax.dot_general(q2, _head_split(k_ref[rows, cols]), _NT,
                                preferred_element_type=F32)
            for h in range(2):
                hd = 2 * pr + h
                u = (s[:, h * tile:(h + 1) * tile] - ck_all[hd:hd + 1, :]) * LOG2E
                if mask is not None:
                    u = jnp.where(mask, u, NEG)
                u_ref[pr, c, :, h * tile:(h + 1) * tile] = u
                top = tops[hd]
                for b in range(nl):
                    top = jnp.maximum(top, u[:, b * LANES:(b + 1) * LANES])
                new_tops.append(top)
        return tuple(new_tops)

    tops = tuple(jnp.full((tile, LANES), NEG, F32) for _ in range(2 * pairs))
    tops = scores(i, tops, causal)
    tops = lax.fori_loop(0, i, lambda c, t: scores(c, t, None), tops)

    shifts = []
    for hd in range(2 * pairs):
        cq = cq_all[:, hd:hd + 1] * LOG2E
        m = jnp.max(tops[hd], axis=-1, keepdims=True) + cq
        shifts.append(jnp.concatenate([jnp.broadcast_to(m - cq, (tile, LANES))] * nl, axis=1))

    acc_ref[...] = jnp.zeros_like(acc_ref)

    def weigh(c, _):
        rows = _chunk_rows(c, tile)
        for pr in range(pairs):
            cols = slice(pr * LANES, (pr + 1) * LANES)
            ps = [jnp.exp2(u_ref[pr, c, :, h * tile:(h + 1) * tile] - shifts[2 * pr + h]).astype(BF16)
                  for h in range(2)]
            rhs = jnp.concatenate([_head_split(v_ref[rows, cols]), ones], axis=1)
            acc_ref[pr] += jnp.dot(jnp.concatenate(ps, axis=1), rhs,
                                   preferred_element_type=F32)
        return 0

    lax.fori_loop(0, i + 1, weigh, 0)
    for pr in range(pairs):
        acc = acc_ref[pr]
        o_ref[:, pr * LANES:(pr + 1) * LANES] = (acc[:, :LANES] / acc[:, LANES:]).astype(o_ref.dtype)


def _fox_attention(proj, cum, batch, seq):
    tile, groups, heads = FOX_TILING.tile, FOX_TILING.groups, FOX_TILING.heads
    nc = seq // tile
    c = cum.reshape(batch, seq, LANES)[:, :, :N_HEADS]
    cq = c.reshape(batch, seq, groups, heads).transpose(0, 2, 1, 3)
    ck = c.reshape(batch, nc, tile, groups, heads).transpose(0, 1, 3, 4, 2)
    specs = [
        pl.BlockSpec((1, 1, tile, heads), lambda b, g, i: (b, g, i, 0)),
        pl.BlockSpec((1, nc, 1, heads, tile), lambda b, g, i: (b, 0, g, 0, 0)),
    ]
    pairs = FOX_TILING.pairs
    scratch = [pltpu.VMEM((pairs, nc, tile, 2 * tile), F32),
               pltpu.VMEM((pairs, tile, 2 * LANES), F32)]
    return _causal_attention_call(_fox_kernel, "fox_attention", FOX_TILING, proj, [cq, ck], specs,
                                  scratch, batch, seq)


def kernel(x, g_pre, g_post, w_in_a, w_out_a, sinks_a, w_in_b, w_out_b, w_in_c, b_f_c, w_out_c):
    batch, seq, d = x.shape
    depth = g_pre.shape[0]
    xf = x.reshape(batch * seq, d)
    wa, wb, wc = w_in_a.astype(BF16), w_in_b.astype(BF16), w_in_c.astype(BF16)
    for i in range(depth):
        kind, j = i % 3, i // 3
        if kind == 0:
            proj = _norm_proj(xf, g_pre[i], wa, j, BRANCH + 2 * KV_A + BRANCH, BF16, PROJ_TN_A)
            o = _swa_attention(proj, sinks_a[j], batch, seq)
            z_col, w_out = BRANCH + 2 * KV_A, w_out_a[j]
        elif kind == 1:
            proj = _norm_proj(xf, g_pre[i], wb, j, 4 * BRANCH, BF16, PROJ_TN)
            o = _stick_attention(proj, batch, seq)
            z_col, w_out = 3 * BRANCH, w_out_b[j]
        else:
            proj = _norm_proj(xf, g_pre[i], wc, j, 4 * BRANCH, BF16, PROJ_TN)
            w_f = jnp.pad(wc[j, :, 4 * BRANCH:], ((0, 0), (0, LANES - N_HEADS)))[None]
            f_logit = _norm_proj(xf, g_pre[i], w_f, 0, LANES, F32, LANES)
            b_f = jnp.pad(b_f_c[j], (0, LANES - N_HEADS)).reshape(1, LANES)
            cum = _cum_log_forget(f_logit, b_f, batch, seq)
            o = _fox_attention(proj, cum, batch, seq)
            z_col, w_out = 3 * BRANCH, w_out_c[j]
        xf = _gate_out(o, proj, z_col, w_out.astype(BF16), g_post[i], xf)
    return xf.reshape(batch, seq, d)
```

```python
import functools
from typing import NamedTuple

import jax
import jax.numpy as jnp
import numpy as np
from jax import lax
from jax.experimental import pallas as pl
from jax.experimental.pallas import tpu as pltpu

D_MODEL = 2048
HEAD_DIM = 64
N_HEADS = 32
BRANCH = N_HEADS * HEAD_DIM
N_KV_A = 4
KV_A = N_KV_A * HEAD_DIM
GROUP_A = N_HEADS // N_KV_A
BLOCK = 128
LANES = 128
NORM_EPS = 1e-6
NEG = -1e30
SCALE = HEAD_DIM ** -0.5
LOG2E = 1.4426950408889634
N_PAIRS = N_HEADS // 2

F32 = jnp.float32
BF16 = jnp.bfloat16

VMEM_LIMIT = 52 * 1024 * 1024
PROJ_TM = 1024
PROJ_TN_A, PROJ_TN = 1536, 1024
GATE_TM = 512

_NT = (((1,), (1,)), ((), ()))


def _params(sem):
    return pltpu.CompilerParams(dimension_semantics=sem, vmem_limit_bytes=VMEM_LIMIT)


def _norm_proj_kernel(x_ref, g_ref, w_ref, o_ref, h_ref):
    @pl.when(pl.program_id(1) == 0)
    def _():
        x = x_ref[...]
        r = lax.rsqrt(jnp.mean(x * x, axis=-1, keepdims=True) + NORM_EPS)
        h_ref[...] = (x * r * g_ref[...]).astype(BF16)

    o_ref[...] = jnp.dot(h_ref[...], w_ref[...], preferred_element_type=F32).astype(o_ref.dtype)


def _norm_proj(x, g, w, layer, n, out_dtype, tn, tm=PROJ_TM):
    m, d = x.shape
    return pl.pallas_call(
        _norm_proj_kernel,
        grid=(m // tm, n // tn),
        in_specs=[
            pl.BlockSpec((tm, d), lambda i, j: (i, 0)),
            pl.BlockSpec((1, d), lambda i, j: (0, 0)),
            pl.BlockSpec((None, d, tn), lambda i, j: (layer, 0, j)),
        ],
        out_specs=pl.BlockSpec((tm, tn), lambda i, j: (i, j)),
        out_shape=jax.ShapeDtypeStruct((m, n), out_dtype),
        scratch_shapes=[pltpu.VMEM((tm, d), BF16)],
        compiler_params=_params(("parallel", "arbitrary")),
        name="norm_proj",
    )(x, g.reshape(1, d), w)


Z_SPLIT = 4
Z_BLOCK = BRANCH // Z_SPLIT


def _gate_out_kernel(o_ref, *refs):
    z_refs, (w_ref, g_ref, x_ref, out_ref) = refs[:Z_SPLIT], refs[Z_SPLIT:]
    z = jnp.concatenate([r[...] for r in z_refs], axis=1).astype(F32)
    gated = (o_ref[...].astype(F32) * (z * jax.nn.sigmoid(z))).astype(BF16)
    y = jnp.dot(gated, w_ref[...], preferred_element_type=F32)
    r = lax.rsqrt(jnp.mean(y * y, axis=-1, keepdims=True) + NORM_EPS)
    out_ref[...] = x_ref[...] + y * r * g_ref[...]


def _gate_out(o, proj, z_col, w_out, g, x, tm=GATE_TM):
    m, d = x.shape
    z0 = z_col // Z_BLOCK
    z_specs = [pl.BlockSpec((tm, Z_BLOCK), lambda i, c=c: (i, z0 + c)) for c in range(Z_SPLIT)]
    return pl.pallas_call(
        _gate_out_kernel,
        grid=(m // tm,),
        in_specs=[pl.BlockSpec((tm, BRANCH), lambda i: (i, 0))] + z_specs + [
            pl.BlockSpec((BRANCH, d), lambda i: (0, 0)),
            pl.BlockSpec((1, d), lambda i: (0, 0)),
            pl.BlockSpec((tm, d), lambda i: (i, 0)),
        ],
        out_specs=pl.BlockSpec((tm, d), lambda i: (i, 0)),
        out_shape=jax.ShapeDtypeStruct((m, d), F32),
        compiler_params=_params(("parallel",)),
        name="gate_out",
    )(o, *([proj] * Z_SPLIT), w_out, g.reshape(1, d), x)


def _stack_pair(q):
    lane = lax.broadcasted_iota(jnp.int32, q.shape, 1)
    zero = jnp.zeros_like(q)
    return jnp.concatenate([jnp.where(lane < HEAD_DIM, q, zero),
                            jnp.where(lane >= HEAD_DIM, q, zero)], axis=0)


def _alibi_slopes():
    n = N_HEADS
    return (2.0 ** (-8.0 * np.arange(1, n + 1, dtype=np.float32) / n)).astype(np.float32)


def _swa_kernel(sink_ref, q_ref, kp_ref, kc_ref, vp_ref, vc_ref, o_ref, bias_ref, *, nb):
    qi = lax.broadcasted_iota(jnp.int32, (BLOCK, BLOCK), 0)
    kj = lax.broadcasted_iota(jnp.int32, (BLOCK, BLOCK), 1)
    from_prev = kj > qi
    first = (pl.program_id(0) == 0) & (pl.program_id(1) == 0)

    @pl.when(first)
    def _():
        distf = jnp.where(from_prev, qi + BLOCK - kj, qi - kj).astype(F32)
        slopes = _alibi_slopes()
        for h in range(N_HEADS):
            alibi = -float(slopes[h]) * distf
            bias_ref[1, h] = alibi
            bias_ref[0, h] = jnp.where(from_prev, NEG, alibi)

    has_prev = jnp.minimum(pl.program_id(1), 1)

    lane = lax.broadcasted_iota(jnp.int32, (2 * BLOCK, LANES), 1)
    low = lane < HEAD_DIM
    low_q = lax.broadcasted_iota(jnp.int32, (BLOCK, LANES), 1) < HEAD_DIM
    for g in range(N_KV_A):
        p, half = divmod(g, 2)
        cols = slice(p * LANES, (p + 1) * LANES)
        k2 = jnp.concatenate([kp_ref[:, cols], kc_ref[:, cols]], axis=0).astype(F32)
        v2 = jnp.concatenate([vp_ref[:, cols], vc_ref[:, cols]], axis=0).astype(F32)
        k2r = pltpu.roll(k2, HEAD_DIM, 1)
        v2r = pltpu.roll(v2, HEAD_DIM, 1)
        own_low = low if half == 0 else jnp.logical_not(low)
        kd = jnp.where(own_low, k2, k2r).astype(BF16)
        vd = jnp.where(own_low, v2, v2r).astype(BF16)

        q = q_ref[:, g * GROUP_A * HEAD_DIM:(g + 1) * GROUP_A * HEAD_DIM] * SCALE
        rows = [_stack_pair(q[:, i * LANES:(i + 1) * LANES]) for i in range(GROUP_A // 2)]
        lhs = jnp.concatenate(rows, axis=0)
        s = lax.dot_general(lhs, kd, _NT, preferred_element_type=F32)
        ps, ls = [], []
        for hh in range(GROUP_A):
            h = g * GROUP_A + hh
            sb = s[hh * BLOCK:(hh + 1) * BLOCK]
            sh = jnp.where(from_prev, sb[:, :BLOCK], sb[:, BLOCK:]) + bias_ref[has_prev, h]
            sink = sink_ref[h]
            m = jnp.maximum(jnp.max(sh, axis=-1, keepdims=True), sink)
            e = jnp.exp(sh - m)
            ls.append(jnp.sum(e, axis=-1, keepdims=True) + jnp.exp(sink - m))
            ps.append(jnp.concatenate([jnp.where(from_prev, e, 0.0), jnp.where(from_prev, 0.0, e)],
                                      axis=1).astype(BF16))
        pv = jnp.dot(jnp.concatenate(ps, axis=0), vd, preferred_element_type=F32)
        for i in range(GROUP_A // 2):
            o0 = pv[(2 * i) * BLOCK:(2 * i + 1) * BLOCK] / ls[2 * i]
            o1 = pv[(2 * i + 1) * BLOCK:(2 * i + 2) * BLOCK] / ls[2 * i + 1]
            c0 = (g * GROUP_A // 2 + i) * LANES
            o_ref[:, c0:c0 + LANES] = jnp.where(low_q, o0, o1).astype(o_ref.dtype)


def _swa_attention(proj, sinks, batch, seq):
    m = proj.shape[0]
    nb = seq // BLOCK
    kcol = BRANCH // KV_A
    vcol = kcol + 1

    def cur(b, n):
        return b * nb + n

    def prev(b, n):
        return b * nb + jnp.maximum(n - 1, 0)

    return pl.pallas_call(
        functools.partial(_swa_kernel, nb=nb),
        grid=(batch, nb),
        in_specs=[
            pl.BlockSpec(memory_space=pltpu.SMEM),
            pl.BlockSpec((BLOCK, BRANCH), lambda b, n: (cur(b, n), 0)),
            pl.BlockSpec((BLOCK, KV_A), lambda b, n: (prev(b, n), kcol)),
            pl.BlockSpec((BLOCK, KV_A), lambda b, n: (cur(b, n), kcol)),
            pl.BlockSpec((BLOCK, KV_A), lambda b, n: (prev(b, n), vcol)),
            pl.BlockSpec((BLOCK, KV_A), lambda b, n: (cur(b, n), vcol)),
        ],
        out_specs=pl.BlockSpec((BLOCK, BRANCH), lambda b, n: (cur(b, n), 0)),
        out_shape=jax.ShapeDtypeStruct((m, BRANCH), BF16),
        scratch_shapes=[pltpu.VMEM((2, N_HEADS, BLOCK, BLOCK), F32)],
        compiler_params=_params(("arbitrary", "arbitrary")),
        name="swa_attention",
    )(sinks, proj, proj, proj, proj, proj)


class _Tiling(NamedTuple):
    tile: int
    pairs: int

    @property
    def groups(self):
        return N_PAIRS // self.pairs

    @property
    def heads(self):
        return 2 * self.pairs


STICK_TILING = _Tiling(tile=256, pairs=8)
FOX_TILING = _Tiling(tile=512, pairs=2)


def _head_split(x):
    lane = lax.broadcasted_iota(jnp.int32, x.shape, 1)
    zero = jnp.zeros_like(x)
    return jnp.concatenate([jnp.where(lane < HEAD_DIM, x, zero),
                            jnp.where(lane >= HEAD_DIM, x, zero)], axis=0)


def _chunk_rows(c, tile):
    return pl.ds(pl.multiple_of(c * tile, tile), tile)


def _causal_attention_call(kernel_fn, name, tiling, proj, extra_inputs, extra_specs, scratch,
                           batch, seq):
    m = proj.shape[0]
    tile, groups = tiling.tile, tiling.groups
    nq = seq // tile
    width = tiling.pairs * LANES
    return pl.pallas_call(
        functools.partial(kernel_fn, tiling=tiling),
        grid=(batch, groups, nq),
        in_specs=[
            pl.BlockSpec((tile, width), lambda b, g, i: (b * nq + i, g)),
            pl.BlockSpec((seq, width), lambda b, g, i: (b, groups + g)),
            pl.BlockSpec((seq, width), lambda b, g, i: (b, 2 * groups + g)),
        ] + extra_specs,
        out_specs=pl.BlockSpec((tile, width), lambda b, g, i: (b * nq + i, g)),
        out_shape=jax.ShapeDtypeStruct((m, BRANCH), BF16),
        scratch_shapes=scratch,
        compiler_params=_params(("parallel", "parallel", "arbitrary")),
        name=name,
    )(proj, proj, proj, *extra_inputs)


def _suffix_matrix():
    j = lax.broadcasted_iota(jnp.int32, (BLOCK, 2 * BLOCK), 0)
    s = lax.broadcasted_iota(jnp.int32, (BLOCK, 2 * BLOCK), 1)
    return jnp.where((s >= BLOCK) | (j > s), 1.0, 0.0).astype(BF16)


SOFTPLUS_CLAMP = 60.0
EXP_UNDERFLOW = 105.0


def _softplus(s):
    return jnp.maximum(s, jnp.log(1.0 + jnp.exp(jnp.minimum(s, SOFTPLUS_CLAMP))))


def _stick_kernel(q_ref, k_ref, v_ref, o_ref, acc_ref, carry_ref, *, tiling):
    tile, pairs = tiling
    nkb = tile // BLOCK
    i = pl.program_id(2)
    u = _suffix_matrix()
    row = lax.broadcasted_iota(jnp.int32, (tile, tile), 0)
    col = lax.broadcasted_iota(jnp.int32, (tile, tile), 1)
    before = jnp.concatenate([col < row, col < row], axis=1)

    def chunk(c, mask):
        rows = _chunk_rows(c, tile)
        for pr in range(pairs):
            cols = slice(pr * LANES, (pr + 1) * LANES)
            q2 = q_ref[:, cols] * SCALE
            s = lax.dot_general(q2, _head_split(k_ref[rows, cols]), _NT,
                                preferred_element_type=F32)
            sp = _softplus(s)
            if mask is not None:
                sp = jnp.where(mask, sp, 0.0)
            spb = sp.astype(BF16)
            base = s - sp
            ps = [None] * (2 * nkb)
            for h in range(2):
                carry = carry_ref[pr, h]
                for kb in reversed(range(nkb)):
                    sl = slice(h * tile + kb * BLOCK, h * tile + (kb + 1) * BLOCK)
                    cs = jnp.dot(spb[:, sl], u, preferred_element_type=F32)
                    ps[h * nkb + kb] = jnp.exp(base[:, sl] - cs[:, :BLOCK] - carry)
                    carry = carry + cs[:, BLOCK:]
                carry_ref[pr, h] = carry
            p = jnp.concatenate(ps, axis=1)
            if mask is not None:
                p = jnp.where(mask, p, 0.0)
            acc_ref[pr] += jnp.dot(p.astype(BF16), _head_split(v_ref[rows, cols]),
                                   preferred_element_type=F32)

    acc_ref[...] = jnp.zeros_like(acc_ref)
    carry_ref[...] = jnp.zeros_like(carry_ref)
    chunk(i, before)

    def weights_vanish():
        return (jnp.min(carry_ref[...]) >= EXP_UNDERFLOW).astype(jnp.int32)

    def cond(st):
        t, done = st
        return jnp.logical_and(t < i, done == 0)

    def body(st):
        t, _ = st
        chunk(i - 1 - t, None)
        return t + 1, weights_vanish()

    lax.while_loop(cond, body, (jnp.int32(0), weights_vanish()))
    for pr in range(pairs):
        o_ref[:, pr * LANES:(pr + 1) * LANES] = acc_ref[pr].astype(o_ref.dtype)


def _stick_attention(proj, batch, seq):
    tile, pairs = STICK_TILING
    scratch = [pltpu.VMEM((pairs, tile, LANES), F32),
               pltpu.VMEM((pairs, 2, tile, LANES), F32)]
    return _causal_attention_call(_stick_kernel, "stick_attention", STICK_TILING, proj, [], [],
                                  scratch, batch, seq)


def _split3(x):
    hi = x.astype(BF16)
    r = x - hi.astype(F32)
    mid = r.astype(BF16)
    lo = (r - mid.astype(F32)).astype(BF16)
    return hi, mid, lo


def _cum_kernel(fl_ref, bf_ref, cum_ref):
    x = fl_ref[...] + bf_ref[...]
    lf = jnp.minimum(x, 0.0) - jnp.log(1.0 + jnp.exp(-jnp.abs(x)))
    t = lax.broadcasted_iota(jnp.int32, (BLOCK, BLOCK), 0)
    j = lax.broadcasted_iota(jnp.int32, (BLOCK, BLOCK), 1)
    tri = jnp.where(j <= t, 1.0, 0.0).astype(BF16)
    parts = jnp.concatenate(_split3(lf), axis=1)
    total = jnp.zeros((1, LANES), F32)
    for blk in range(x.shape[0] // BLOCK):
        rows = slice(blk * BLOCK, (blk + 1) * BLOCK)
        c = jnp.dot(tri, parts[rows], preferred_element_type=F32)
        c = c[:, :LANES] + c[:, LANES:2 * LANES] + c[:, 2 * LANES:] + total
        cum_ref[rows, :] = c
        total = c[BLOCK - 1:BLOCK, :]


def _cum_log_forget(f_logit, b_f, batch, seq):
    return pl.pallas_call(
        _cum_kernel,
        grid=(batch,),
        in_specs=[pl.BlockSpec((seq, LANES), lambda b: (b, 0)),
                  pl.BlockSpec((1, LANES), lambda b: (0, 0))],
        out_specs=pl.BlockSpec((seq, LANES), lambda b: (b, 0)),
        out_shape=jax.ShapeDtypeStruct((batch * seq, LANES), F32),
        compiler_params=_params(("parallel",)),
        name="cum_log_forget",
    )(f_logit, b_f)


def _fox_kernel(q_ref, k_ref, v_ref, cq_ref, ck_ref, o_ref, u_ref, acc_ref, *, tiling):
    tile, pairs = tiling
    nl = tile // LANES
    i = pl.program_id(2)
    row = lax.broadcasted_iota(jnp.int32, (tile, tile), 0)
    col = lax.broadcasted_iota(jnp.int32, (tile, tile), 1)
    causal = col <= row
    cq_all = cq_ref[0, 0]
    r2 = lax.broadcasted_iota(jnp.int32, (2 * tile, LANES), 0)
    l2 = lax.broadcasted_iota(jnp.int32, (2 * tile, LANES), 1)
    ones = jnp.where((r2 < tile) == (l2 < HEAD_DIM), 1.0, 0.0).astype(BF16)

    def scores(c, tops, mask):
        rows = _chunk_rows(c, tile)
        ck_all = ck_ref[0, c, 0]
        new_tops = []
        for pr in range(pairs):
            cols = slice(pr * LANES, (pr + 1) * LANES)
            q2 = q_ref[:, cols] * SCALE
            s = lax.dot_general(q2, _head_split(k_ref[rows, cols]), _NT,
                                preferred_element_type=F32)
            for h in range(2):
                hd = 2 * pr + h
                u = (s[:, h * tile:(h + 1) * tile] - ck_all[hd:hd + 1, :]) * LOG2E
                if mask is not None:
                    u = jnp.where(mask, u, NEG)
                u_ref[pr, c, :, h * tile:(h + 1) * tile] = u
                top = tops[hd]
                for b in range(nl):
                    top = jnp.maximum(top, u[:, b * LANES:(b + 1) * LANES])
                new_tops.append(top)
        return tuple(new_tops)

    tops = tuple(jnp.full((tile, LANES), NEG, F32) for _ in range(2 * pairs))
    tops = lax.fori_loop(0, i, lambda c, t: scores(c, t, None), tops)
    tops = scores(i, tops, causal)

    shifts = []
    for hd in range(2 * pairs):
        cq = cq_all[:, hd:hd + 1] * LOG2E
        m = jnp.max(tops[hd], axis=-1, keepdims=True) + cq
        shifts.append(jnp.concatenate([jnp.broadcast_to(m - cq, (tile, LANES))] * nl, axis=1))

    def weigh(c, first):
        rows = _chunk_rows(c, tile)
        for pr in range(pairs):
            cols = slice(pr * LANES, (pr + 1) * LANES)
            ps = [jnp.exp2(u_ref[pr, c, :, h * tile:(h + 1) * tile] - shifts[2 * pr + h]).astype(BF16)
                  for h in range(2)]
            rhs = jnp.concatenate([_head_split(v_ref[rows, cols]), ones], axis=1)
            pv = jnp.dot(jnp.concatenate(ps, axis=1), rhs,
                         preferred_element_type=F32)
            acc_ref[pr] = pv if first else acc_ref[pr] + pv

    weigh(i, True)

    def body(c, _):
        weigh(c, False)
        return 0

    lax.fori_loop(0, i, body, 0)
    for pr in range(pairs):
        acc = acc_ref[pr]
        o_ref[:, pr * LANES:(pr + 1) * LANES] = (acc[:, :LANES] / acc[:, LANES:]).astype(o_ref.dtype)


def _fox_attention(proj, cum, batch, seq):
    tile, groups, heads = FOX_TILING.tile, FOX_TILING.groups, FOX_TILING.heads
    nc = seq // tile
    c = cum.reshape(batch, seq, LANES)[:, :, :N_HEADS]
    cq = c.reshape(batch, seq, groups, heads).transpose(0, 2, 1, 3)
    ck = c.reshape(batch, nc, tile, groups, heads).transpose(0, 1, 3, 4, 2)
    specs = [
        pl.BlockSpec((1, 1, tile, heads), lambda b, g, i: (b, g, i, 0)),
        pl.BlockSpec((1, nc, 1, heads, tile), lambda b, g, i: (b, 0, g, 0, 0)),
    ]
    pairs = FOX_TILING.pairs
    scratch = [pltpu.VMEM((pairs, nc, tile, 2 * tile), F32),
               pltpu.VMEM((pairs, tile, 2 * LANES), F32)]
    return _causal_attention_call(_fox_kernel, "fox_attention", FOX_TILING, proj, [cq, ck], specs,
                                  scratch, batch, seq)


def kernel(x, g_pre, g_post, w_in_a, w_out_a, sinks_a, w_in_b, w_out_b, w_in_c, b_f_c, w_out_c):
    batch, seq, d = x.shape
    depth = g_pre.shape[0]
    xf = x.reshape(batch * seq, d)
    wa, wb, wc = w_in_a.astype(BF16), w_in_b.astype(BF16), w_in_c.astype(BF16)
    for i in range(depth):
        kind, j = i % 3, i // 3
        if kind == 0:
            proj = _norm_proj(xf, g_pre[i], wa, j, BRANCH + 2 * KV_A + BRANCH, BF16, PROJ_TN_A)
            o = _swa_attention(proj, sinks_a[j], batch, seq)
            z_col, w_out = BRANCH + 2 * KV_A, w_out_a[j]
        elif kind == 1:
            proj = _norm_proj(xf, g_pre[i], wb, j, 4 * BRANCH, BF16, PROJ_TN)
            o = _stick_attention(proj, batch, seq)
            z_col, w_out = 3 * BRANCH, w_out_b[j]
        else:
            proj = _norm_proj(xf, g_pre[i], wc, j, 4 * BRANCH, BF16, PROJ_TN)
            w_f = jnp.pad(wc[j, :, 4 * BRANCH:], ((0, 0), (0, LANES - N_HEADS)))[None]
            f_logit = _norm_proj(xf, g_pre[i], w_f, 0, LANES, F32, LANES)
            b_f = jnp.pad(b_f_c[j], (0, LANES - N_HEADS)).reshape(1, LANES)
            cum = _cum_log_forget(f_logit, b_f, batch, seq)
            o = _fox_attention(proj, cum, batch, seq)
            z_col, w_out = 3 * BRANCH, w_out_c[j]
        xf = _gate_out(o, proj, z_col, w_out.astype(BF16), g_post[i], xf)
    return xf.reshape(batch, seq, d)
```

```python
import functools
from typing import NamedTuple

import jax
import jax.numpy as jnp
import numpy as np
from jax import lax
from jax.experimental import pallas as pl
from jax.experimental.pallas import tpu as pltpu

D_MODEL = 2048
HEAD_DIM = 64
N_HEADS = 32
BRANCH = N_HEADS * HEAD_DIM
N_KV_A = 4
KV_A = N_KV_A * HEAD_DIM
GROUP_A = N_HEADS // N_KV_A
BLOCK = 128
LANES = 128
NORM_EPS = 1e-6
NEG = -1e30
SCALE = HEAD_DIM ** -0.5
LOG2E = 1.4426950408889634
N_PAIRS = N_HEADS // 2

F32 = jnp.float32
BF16 = jnp.bfloat16

VMEM_LIMIT = 52 * 1024 * 1024
PROJ_TM = 1024
PROJ_TN_A, PROJ_TN = 768, 1024
GATE_TM = 512

_NT = (((1,), (1,)), ((), ()))


def _params(sem):
    return pltpu.CompilerParams(dimension_semantics=sem, vmem_limit_bytes=VMEM_LIMIT)


def _rmsnorm(x, g):
    r = lax.rsqrt(jnp.mean(x * x, axis=-1, keepdims=True) + NORM_EPS)
    return x * r * g


def _norm_kernel(x_ref, g_ref, h_ref):
    h_ref[...] = _rmsnorm(x_ref[...], g_ref[...]).astype(h_ref.dtype)


def _norm(x, g, tm=PROJ_TM):
    m, d = x.shape
    return pl.pallas_call(
        _norm_kernel,
        grid=(m // tm,),
        in_specs=[pl.BlockSpec((tm, d), lambda i: (i, 0)), pl.BlockSpec((1, d), lambda i: (0, 0))],
        out_specs=pl.BlockSpec((tm, d), lambda i: (i, 0)),
        out_shape=jax.ShapeDtypeStruct((m, d), BF16),
        compiler_params=_params(("parallel",)),
        name="norm",
    )(x, g.reshape(1, d))


def _proj_kernel(h_ref, w_ref, o_ref, wb_ref):
    @pl.when(pl.program_id(1) == 0)
    def _():
        wb_ref[...] = w_ref[...].astype(BF16)

    o_ref[...] = jnp.dot(h_ref[...], wb_ref[...], preferred_element_type=F32).astype(o_ref.dtype)


def _proj(h, w, layer, n, out_dtype, tn, tm=PROJ_TM):
    m, d = h.shape
    return pl.pallas_call(
        _proj_kernel,
        grid=(n // tn, m // tm),
        in_specs=[
            pl.BlockSpec((tm, d), lambda j, i: (i, 0)),
            pl.BlockSpec((None, d, tn), lambda j, i: (layer, 0, j)),
        ],
        out_specs=pl.BlockSpec((tm, tn), lambda j, i: (i, j)),
        out_shape=jax.ShapeDtypeStruct((m, n), out_dtype),
        scratch_shapes=[pltpu.VMEM((d, tn), BF16)],
        compiler_params=_params(("parallel", "arbitrary")),
        name="proj",
    )(h, w)


Z_SPLIT = 4
Z_BLOCK = BRANCH // Z_SPLIT


def _gate_out_kernel(o_ref, *refs, emit_next):
    z_refs, (w_ref, g_ref, x_ref), rest = refs[:Z_SPLIT], refs[Z_SPLIT:Z_SPLIT + 3], refs[Z_SPLIT + 3:]
    z = jnp.concatenate([r[...] for r in z_refs], axis=1).astype(F32)
    gated = (o_ref[...].astype(F32) * (z * jax.nn.sigmoid(z))).astype(BF16)
    y = jnp.dot(gated, w_ref[...], preferred_element_type=F32)
    x_new = x_ref[...] + _rmsnorm(y, g_ref[...])
    if emit_next:
        gn_ref, out_ref, hn_ref = rest
        hn_ref[...] = _rmsnorm(x_new, gn_ref[...]).astype(hn_ref.dtype)
    else:
        out_ref, = rest
    out_ref[...] = x_new


def _gate_out(o, proj, z_col, w_out, g, x, g_next, tm=GATE_TM):
    m, d = x.shape
    emit_next = g_next is not None
    z0 = z_col // Z_BLOCK
    z_specs = [pl.BlockSpec((tm, Z_BLOCK), lambda i, c=c: (i, z0 + c)) for c in range(Z_SPLIT)]
    row = pl.BlockSpec((tm, d), lambda i: (i, 0))
    gain = pl.BlockSpec((1, d), lambda i: (0, 0))
    outs = pl.pallas_call(
        functools.partial(_gate_out_kernel, emit_next=emit_next),
        grid=(m // tm,),
        in_specs=[pl.BlockSpec((tm, BRANCH), lambda i: (i, 0))] + z_specs + [
            pl.BlockSpec((BRANCH, d), lambda i: (0, 0)), gain, row] + [gain] * emit_next,
        out_specs=[row] + [row] * emit_next,
        out_shape=[jax.ShapeDtypeStruct((m, d), F32)] + [jax.ShapeDtypeStruct((m, d), BF16)] * emit_next,
        compiler_params=_params(("parallel",)),
        name="gate_out",
    )(o, *([proj] * Z_SPLIT), w_out, g.reshape(1, d), x,
      *([g_next.reshape(1, d)] if emit_next else []))
    return (outs[0], outs[1]) if emit_next else (outs[0], None)


def _stack_pair(q):
    lane = lax.broadcasted_iota(jnp.int32, q.shape, 1)
    zero = jnp.zeros_like(q)
    return jnp.concatenate([jnp.where(lane < HEAD_DIM, q, zero),
                            jnp.where(lane >= HEAD_DIM, q, zero)], axis=0)


def _alibi_slopes():
    n = N_HEADS
    return (2.0 ** (-8.0 * np.arange(1, n + 1, dtype=np.float32) / n)).astype(np.float32)


def _swa_kernel(sink_ref, q_ref, kp_ref, kc_ref, vp_ref, vc_ref, o_ref, bias_ref, *, nb):
    qi = lax.broadcasted_iota(jnp.int32, (BLOCK, BLOCK), 0)
    kj = lax.broadcasted_iota(jnp.int32, (BLOCK, BLOCK), 1)
    from_prev = kj > qi
    first = (pl.program_id(0) == 0) & (pl.program_id(1) == 0)

    @pl.when(first)
    def _():
        distf = jnp.where(from_prev, qi + BLOCK - kj, qi - kj).astype(F32)
        slopes = _alibi_slopes()
        for h in range(N_HEADS):
            alibi = -float(slopes[h]) * distf
            bias_ref[1, h] = alibi
            bias_ref[0, h] = jnp.where(from_prev, NEG, alibi)

    has_prev = jnp.minimum(pl.program_id(1), 1)

    lane = lax.broadcasted_iota(jnp.int32, (2 * BLOCK, LANES), 1)
    low = lane < HEAD_DIM
    low_q = lax.broadcasted_iota(jnp.int32, (BLOCK, LANES), 1) < HEAD_DIM
    for g in range(N_KV_A):
        p, half = divmod(g, 2)
        cols = slice(p * LANES, (p + 1) * LANES)
        k2 = jnp.concatenate([kp_ref[:, cols], kc_ref[:, cols]], axis=0).astype(F32)
        v2 = jnp.concatenate([vp_ref[:, cols], vc_ref[:, cols]], axis=0).astype(F32)
        k2r = pltpu.roll(k2, HEAD_DIM, 1)
        v2r = pltpu.roll(v2, HEAD_DIM, 1)
        own_low = low if half == 0 else jnp.logical_not(low)
        kd = jnp.where(own_low, k2, k2r).astype(BF16)
        vd = jnp.where(own_low, v2, v2r).astype(BF16)

        q = q_ref[:, g * GROUP_A * HEAD_DIM:(g + 1) * GROUP_A * HEAD_DIM] * SCALE
        rows = [_stack_pair(q[:, i * LANES:(i + 1) * LANES]) for i in range(GROUP_A // 2)]
        lhs = jnp.concatenate(rows, axis=0)
        s = lax.dot_general(lhs, kd, _NT, preferred_element_type=F32)
        ps, ls = [], []
        for hh in range(GROUP_A):
            h = g * GROUP_A + hh
            sb = s[hh * BLOCK:(hh + 1) * BLOCK]
            sh = jnp.where(from_prev, sb[:, :BLOCK], sb[:, BLOCK:]) + bias_ref[has_prev, h]
            sink = sink_ref[h]
            m = jnp.maximum(jnp.max(sh, axis=-1, keepdims=True), sink)
            e = jnp.exp(sh - m)
            ls.append(jnp.sum(e, axis=-1, keepdims=True) + jnp.exp(sink - m))
            ps.append(jnp.concatenate([jnp.where(from_prev, e, 0.0), jnp.where(from_prev, 0.0, e)],
                                      axis=1).astype(BF16))
        pv = jnp.dot(jnp.concatenate(ps, axis=0), vd, preferred_element_type=F32)
        for i in range(GROUP_A // 2):
            o0 = pv[(2 * i) * BLOCK:(2 * i + 1) * BLOCK] / ls[2 * i]
            o1 = pv[(2 * i + 1) * BLOCK:(2 * i + 2) * BLOCK] / ls[2 * i + 1]
            c0 = (g * GROUP_A // 2 + i) * LANES
            o_ref[:, c0:c0 + LANES] = jnp.where(low_q, o0, o1).astype(o_ref.dtype)


def _swa_attention(proj, sinks, batch, seq):
    m = proj.shape[0]
    nb = seq // BLOCK
    kcol = BRANCH // KV_A
    vcol = kcol + 1

    def cur(b, n):
        return b * nb + n

    def prev(b, n):
        return b * nb + jnp.maximum(n - 1, 0)

    return pl.pallas_call(
        functools.partial(_swa_kernel, nb=nb),
        grid=(batch, nb),
        in_specs=[
            pl.BlockSpec(memory_space=pltpu.SMEM),
            pl.BlockSpec((BLOCK, BRANCH), lambda b, n: (cur(b, n), 0)),
            pl.BlockSpec((BLOCK, KV_A), lambda b, n: (prev(b, n), kcol)),
            pl.BlockSpec((BLOCK, KV_A), lambda b, n: (cur(b, n), kcol)),
            pl.BlockSpec((BLOCK, KV_A), lambda b, n: (prev(b, n), vcol)),
            pl.BlockSpec((BLOCK, KV_A), lambda b, n: (cur(b, n), vcol)),
        ],
        out_specs=pl.BlockSpec((BLOCK, BRANCH), lambda b, n: (cur(b, n), 0)),
        out_shape=jax.ShapeDtypeStruct((m, BRANCH), BF16),
        scratch_shapes=[pltpu.VMEM((2, N_HEADS, BLOCK, BLOCK), F32)],
        compiler_params=_params(("arbitrary", "arbitrary")),
        name="swa_attention",
    )(sinks, proj, proj, proj, proj, proj)


class _Tiling(NamedTuple):
    tile: int
    pairs: int

    @property
    def groups(self):
        return N_PAIRS // self.pairs

    @property
    def heads(self):
        return 2 * self.pairs


STICK_TILING = _Tiling(tile=256, pairs=8)
FOX_TILING = _Tiling(tile=512, pairs=2)


def _head_split(x):
    lane = lax.broadcasted_iota(jnp.int32, x.shape, 1)
    zero = jnp.zeros_like(x)
    return jnp.concatenate([jnp.where(lane < HEAD_DIM, x, zero),
                            jnp.where(lane >= HEAD_DIM, x, zero)], axis=0)


def _chunk_rows(c, tile):
    return pl.ds(pl.multiple_of(c * tile, tile), tile)


def _causal_attention_call(kernel_fn, name, tiling, proj, extra_inputs, extra_specs, scratch,
                           batch, seq):
    m = proj.shape[0]
    tile, groups = tiling.tile, tiling.groups
    nq = seq // tile
    width = tiling.pairs * LANES
    return pl.pallas_call(
        functools.partial(kernel_fn, tiling=tiling),
        grid=(batch, groups, nq),
        in_specs=[
            pl.BlockSpec((tile, width), lambda b, g, i: (b * nq + i, g)),
            pl.BlockSpec((seq, width), lambda b, g, i: (b, groups + g)),
            pl.BlockSpec((seq, width), lambda b, g, i: (b, 2 * groups + g)),
        ] + extra_specs,
        out_specs=pl.BlockSpec((tile, width), lambda b, g, i: (b * nq + i, g)),
        out_shape=jax.ShapeDtypeStruct((m, BRANCH), BF16),
        scratch_shapes=scratch,
        compiler_params=_params(("parallel", "parallel", "arbitrary")),
        name=name,
    )(proj, proj, proj, *extra_inputs)


def _suffix_matrix():
    j = lax.broadcasted_iota(jnp.int32, (BLOCK, 2 * BLOCK), 0)
    s = lax.broadcasted_iota(jnp.int32, (BLOCK, 2 * BLOCK), 1)
    return jnp.where((s >= BLOCK) | (j > s), 1.0, 0.0).astype(BF16)


SOFTPLUS_CLAMP = 60.0
EXP_UNDERFLOW = 105.0


def _softplus(s):
    return jnp.maximum(s, jnp.log(1.0 + jnp.exp(jnp.minimum(s, SOFTPLUS_CLAMP))))


def _stick_kernel(q_ref, k_ref, v_ref, o_ref, acc_ref, carry_ref, *, tiling):
    tile, pairs = tiling
    nkb = tile // BLOCK
    i = pl.program_id(2)
    u = _suffix_matrix()
    row = lax.broadcasted_iota(jnp.int32, (tile, tile), 0)
    col = lax.broadcasted_iota(jnp.int32, (tile, tile), 1)
    before = jnp.concatenate([col < row, col < row], axis=1)

    def chunk(c, mask):
        rows = _chunk_rows(c, tile)
        for pr in range(pairs):
            cols = slice(pr * LANES, (pr + 1) * LANES)
            q2 = q_ref[:, cols] * SCALE
            s = lax.dot_general(q2, _head_split(k_ref[rows, cols]), _NT,
                                preferred_element_type=F32)
            sp = _softplus(s)
            if mask is not None:
                sp = jnp.where(mask, sp, 0.0)
            spb = sp.astype(BF16)
            base = s - sp
            ps = [None] * (2 * nkb)
            for h in range(2):
                carry = carry_ref[pr, h]
                for kb in reversed(range(nkb)):
                    sl = slice(h * tile + kb * BLOCK, h * tile + (kb + 1) * BLOCK)
                    cs = jnp.dot(spb[:, sl], u, preferred_element_type=F32)
                    ps[h * nkb + kb] = jnp.exp(base[:, sl] - cs[:, :BLOCK] - carry)
                    carry = carry + cs[:, BLOCK:]
                carry_ref[pr, h] = carry
            p = jnp.concatenate(ps, axis=1)
            if mask is not None:
                p = jnp.where(mask, p, 0.0)
            acc_ref[pr] += jnp.dot(p.astype(BF16), _head_split(v_ref[rows, cols]),
                                   preferred_element_type=F32)

    acc_ref[...] = jnp.zeros_like(acc_ref)
    carry_ref[...] = jnp.zeros_like(carry_ref)
    chunk(i, before)

    def weights_vanish():
        return (jnp.min(carry_ref[...]) >= EXP_UNDERFLOW).astype(jnp.int32)

    def cond(st):
        t, done = st
        return jnp.logical_and(t < i, done == 0)

    def body(st):
        t, _ = st
        chunk(i - 1 - t, None)
        return t + 1, weights_vanish()

    lax.while_loop(cond, body, (jnp.int32(0), weights_vanish()))
    for pr in range(pairs):
        o_ref[:, pr * LANES:(pr + 1) * LANES] = acc_ref[pr].astype(o_ref.dtype)


def _stick_attention(proj, batch, seq):
    tile, pairs = STICK_TILING
    scratch = [pltpu.VMEM((pairs, tile, LANES), F32),
               pltpu.VMEM((pairs, 2, tile, LANES), F32)]
    return _causal_attention_call(_stick_kernel, "stick_attention", STICK_TILING, proj, [], [],
                                  scratch, batch, seq)


def _split3(x):
    hi = x.astype(BF16)
    r = x - hi.astype(F32)
    mid = r.astype(BF16)
    lo = (r - mid.astype(F32)).astype(BF16)
    return hi, mid, lo


def _cum_kernel(fl_ref, bf_ref, cum_ref):
    x = fl_ref[...] + bf_ref[...]
    lf = jnp.minimum(x, 0.0) - jnp.log(1.0 + jnp.exp(-jnp.abs(x)))
    t = lax.broadcasted_iota(jnp.int32, (BLOCK, BLOCK), 0)
    j = lax.broadcasted_iota(jnp.int32, (BLOCK, BLOCK), 1)
    tri = jnp.where(j <= t, 1.0, 0.0).astype(BF16)
    parts = jnp.concatenate(_split3(lf), axis=1)
    total = jnp.zeros((1, LANES), F32)
    for blk in range(x.shape[0] // BLOCK):
        rows = slice(blk * BLOCK, (blk + 1) * BLOCK)
        c = jnp.dot(tri, parts[rows], preferred_element_type=F32)
        c = c[:, :LANES] + c[:, LANES:2 * LANES] + c[:, 2 * LANES:] + total
        cum_ref[rows, :] = c
        total = c[BLOCK - 1:BLOCK, :]


def _cum_log_forget(f_logit, b_f, batch, seq):
    return pl.pallas_call(
        _cum_kernel,
        grid=(batch,),
        in_specs=[pl.BlockSpec((seq, LANES), lambda b: (b, 0)),
                  pl.BlockSpec((1, LANES), lambda b: (0, 0))],
        out_specs=pl.BlockSpec((seq, LANES), lambda b: (b, 0)),
        out_shape=jax.ShapeDtypeStruct((batch * seq, LANES), F32),
        compiler_params=_params(("parallel",)),
        name="cum_log_forget",
    )(f_logit, b_f)


def _fox_kernel(q_ref, k_ref, v_ref, cq_ref, ck_ref, o_ref, u_ref, acc_ref, *, tiling):
    tile, pairs = tiling
    nl = tile // LANES
    i = pl.program_id(2)
    row = lax.broadcasted_iota(jnp.int32, (tile, tile), 0)
    col = lax.broadcasted_iota(jnp.int32, (tile, tile), 1)
    causal = col <= row
    cq_all = cq_ref[0, 0]
    r2 = lax.broadcasted_iota(jnp.int32, (2 * tile, LANES), 0)
    l2 = lax.broadcasted_iota(jnp.int32, (2 * tile, LANES), 1)
    ones = jnp.where((r2 < tile) == (l2 < HEAD_DIM), 1.0, 0.0).astype(BF16)

    def scores(c, tops, mask):
        rows = _chunk_rows(c, tile)
        ck_all = ck_ref[0, c, 0]
        new_tops = []
        for pr in range(pairs):
            cols = slice(pr * LANES, (pr + 1) * LANES)
            q2 = q_ref[:, cols] * SCALE
            s = lax.dot_general(q2, _head_split(k_ref[rows, cols]), _NT,
                                preferred_element_type=F32)
            for h in range(2):
                hd = 2 * pr + h
                u = (s[:, h * tile:(h + 1) * tile] - ck_all[hd:hd + 1, :]) * LOG2E
                if mask is not None:
                    u = jnp.where(mask, u, NEG)
                u_ref[pr, c, :, h * tile:(h + 1) * tile] = u
                top = tops[hd]
                for b in range(nl):
                    top = jnp.maximum(top, u[:, b * LANES:(b + 1) * LANES])
                new_tops.append(top)
        return tuple(new_tops)

    tops = tuple(jnp.full((tile, LANES), NEG, F32) for _ in range(2 * pairs))
    tops = lax.fori_loop(0, i, lambda c, t: scores(c, t, None), tops)
    tops = scores(i, tops, causal)

    shifts = []
    for hd in range(2 * pairs):
        cq = cq_all[:, hd:hd + 1] * LOG2E
        m = jnp.max(tops[hd], axis=-1, keepdims=True) + cq
        shifts.append(jnp.concatenate([jnp.broadcast_to(m - cq, (tile, LANES))] * nl, axis=1))

    def weigh(c, first):
        rows = _chunk_rows(c, tile)
        for pr in range(pairs):
            cols = slice(pr * LANES, (pr + 1) * LANES)
            ps = [jnp.exp2(u_ref[pr, c, :, h * tile:(h + 1) * tile] - shifts[2 * pr + h]).astype(BF16)
                  for h in range(2)]
            rhs = jnp.concatenate([_head_split(v_ref[rows, cols]), ones], axis=1)
            pv = jnp.dot(jnp.concatenate(ps, axis=1), rhs,
                         preferred_element_type=F32)
            acc_ref[pr] = pv if first else acc_ref[pr] + pv

    weigh(i, True)

    def body(c, _):
        weigh(c, False)
        return 0

    lax.fori_loop(0, i, body, 0)
    for pr in range(pairs):
        acc = acc_ref[pr]
        o_ref[:, pr * LANES:(pr + 1) * LANES] = (acc[:, :LANES] / acc[:, LANES:]).astype(o_ref.dtype)


def _fox_attention(proj, cum, batch, seq):
    tile, groups, heads = FOX_TILING.tile, FOX_TILING.groups, FOX_TILING.heads
    nc = seq // tile
    c = cum.reshape(batch, seq, LANES)[:, :, :N_HEADS]
    cq = c.reshape(batch, seq, groups, heads).transpose(0, 2, 1, 3)
    ck = c.reshape(batch, nc, tile, groups, heads).transpose(0, 1, 3, 4, 2)
    specs = [
        pl.BlockSpec((1, 1, tile, heads), lambda b, g, i: (b, g, i, 0)),
        pl.BlockSpec((1, nc, 1, heads, tile), lambda b, g, i: (b, 0, g, 0, 0)),
    ]
    pairs = FOX_TILING.pairs
    scratch = [pltpu.VMEM((pairs, nc, tile, 2 * tile), F32),
               pltpu.VMEM((pairs, tile, 2 * LANES), F32)]
    return _causal_attention_call(_fox_kernel, "fox_attention", FOX_TILING, proj, [cq, ck], specs,
                                  scratch, batch, seq)


def kernel(x, g_pre, g_post, w_in_a, w_out_a, sinks_a, w_in_b, w_out_b, w_in_c, b_f_c, w_out_c):
    batch, seq, d = x.shape
    depth = g_pre.shape[0]
    xf = x.reshape(batch * seq, d)
    h = _norm(xf, g_pre[0])
    for i in range(depth):
        kind, j = i % 3, i // 3
        if kind == 0:
            proj = _proj(h, w_in_a, j, BRANCH + 2 * KV_A + BRANCH, BF16, PROJ_TN_A)
            o = _swa_attention(proj, sinks_a[j], batch, seq)
            z_col, w_out = BRANCH + 2 * KV_A, w_out_a[j]
        elif kind == 1:
            proj = _proj(h, w_in_b, j, 4 * BRANCH, BF16, PROJ_TN)
            o = _stick_attention(proj, batch, seq)
            z_col, w_out = 3 * BRANCH, w_out_b[j]
        else:
            proj = _proj(h, w_in_c, j, 4 * BRANCH, BF16, PROJ_TN)
            w_f = jnp.pad(w_in_c[j, :, 4 * BRANCH:], ((0, 0), (0, LANES - N_HEADS)))[None]
            f_logit = _proj(h, w_f, 0, LANES, F32, LANES)
            b_f = jnp.pad(b_f_c[j], (0, LANES - N_HEADS)).reshape(1, LANES)
            cum = _cum_log_forget(f_logit, b_f, batch, seq)
            o = _fox_attention(proj, cum, batch, seq)
            z_col, w_out = 3 * BRANCH, w_out_c[j]
        g_next = g_pre[i + 1] if i + 1 < depth else None
        xf, h = _gate_out(o, proj, z_col, w_out.astype(BF16), g_post[i], xf, g_next)
    return xf.reshape(batch, seq, d)
```

```python
import functools
from typing import NamedTuple

import jax
import jax.numpy as jnp
import numpy as np
from jax import lax
from jax.experimental import pallas as pl
from jax.experimental.pallas import tpu as pltpu

D_MODEL = 2048
HEAD_DIM = 64
N_HEADS = 32
BRANCH = N_HEADS * HEAD_DIM
N_KV_A = 4
KV_A = N_KV_A * HEAD_DIM
GROUP_A = N_HEADS // N_KV_A
BLOCK = 128
LANES = 128
NORM_EPS = 1e-6
NEG = -1e30
SCALE = HEAD_DIM ** -0.5
LOG2E = 1.4426950408889634
N_PAIRS = N_HEADS // 2

F32 = jnp.float32
BF16 = jnp.bfloat16

VMEM_LIMIT = 52 * 1024 * 1024
PROJ_TM = 1024
PROJ_TN_A, PROJ_TN = 1536, 2048
GATE_TM = 512

_NT = (((1,), (1,)), ((), ()))


def _params(sem):
    return pltpu.CompilerParams(dimension_semantics=sem, vmem_limit_bytes=VMEM_LIMIT)


def _rmsnorm(x, g):
    r = lax.rsqrt(jnp.mean(x * x, axis=-1, keepdims=True) + NORM_EPS)
    return x * r * g


def _norm_proj_kernel(x_ref, g_ref, w_ref, o_ref, h_ref):
    @pl.when(pl.program_id(1) == 0)
    def _():
        h_ref[...] = _rmsnorm(x_ref[...], g_ref[...]).astype(BF16)

    o_ref[...] = jnp.dot(h_ref[...], w_ref[...], preferred_element_type=F32).astype(o_ref.dtype)


def _norm_proj(x, g, w, layer, n, out_dtype, tn, tm=PROJ_TM):
    m, d = x.shape
    return pl.pallas_call(
        _norm_proj_kernel,
        grid=(m // tm, n // tn),
        in_specs=[
            pl.BlockSpec((tm, d), lambda i, j: (i, 0)),
            pl.BlockSpec((1, d), lambda i, j: (0, 0)),
            pl.BlockSpec((None, d, tn), lambda i, j: (layer, 0, j)),
        ],
        out_specs=pl.BlockSpec((tm, tn), lambda i, j: (i, j)),
        out_shape=jax.ShapeDtypeStruct((m, n), out_dtype),
        scratch_shapes=[pltpu.VMEM((tm, d), BF16)],
        compiler_params=_params(("parallel", "arbitrary")),
        name="norm_proj",
    )(x, g.reshape(1, d), w)


Z_SPLIT = 4
Z_BLOCK = BRANCH // Z_SPLIT


def _gate_out_kernel(o_ref, *refs):
    z_refs, (w_ref, g_ref, x_ref, out_ref) = refs[:Z_SPLIT], refs[Z_SPLIT:]
    z = jnp.concatenate([r[...] for r in z_refs], axis=1).astype(F32)
    gated = (o_ref[...].astype(F32) * (z * jax.nn.sigmoid(z))).astype(BF16)
    y = jnp.dot(gated, w_ref[...], preferred_element_type=F32)
    out_ref[...] = x_ref[...] + _rmsnorm(y, g_ref[...])


def _gate_out(o, proj, z_col, w_out, g, x, tm=GATE_TM):
    m, d = x.shape
    z0 = z_col // Z_BLOCK
    z_specs = [pl.BlockSpec((tm, Z_BLOCK), lambda i, c=c: (i, z0 + c)) for c in range(Z_SPLIT)]
    return pl.pallas_call(
        _gate_out_kernel,
        grid=(m // tm,),
        in_specs=[pl.BlockSpec((tm, BRANCH), lambda i: (i, 0))] + z_specs + [
            pl.BlockSpec((BRANCH, d), lambda i: (0, 0)),
            pl.BlockSpec((1, d), lambda i: (0, 0)),
            pl.BlockSpec((tm, d), lambda i: (i, 0)),
        ],
        out_specs=pl.BlockSpec((tm, d), lambda i: (i, 0)),
        out_shape=jax.ShapeDtypeStruct((m, d), F32),
        compiler_params=_params(("parallel",)),
        name="gate_out",
    )(o, *([proj] * Z_SPLIT), w_out, g.reshape(1, d), x)


def _stack_pair(q):
    lane = lax.broadcasted_iota(jnp.int32, q.shape, 1)
    zero = jnp.zeros_like(q)
    return jnp.concatenate([jnp.where(lane < HEAD_DIM, q, zero),
                            jnp.where(lane >= HEAD_DIM, q, zero)], axis=0)


def _alibi_slopes():
    n = N_HEADS
    return (2.0 ** (-8.0 * np.arange(1, n + 1, dtype=np.float32) / n)).astype(np.float32)


def _swa_kernel(sink_ref, q_ref, kp_ref, kc_ref, vp_ref, vc_ref, o_ref, bias_ref, *, nb):
    qi = lax.broadcasted_iota(jnp.int32, (BLOCK, BLOCK), 0)
    kj = lax.broadcasted_iota(jnp.int32, (BLOCK, BLOCK), 1)
    from_prev = kj > qi
    first = (pl.program_id(0) == 0) & (pl.program_id(1) == 0)

    @pl.when(first)
    def _():
        distf = jnp.where(from_prev, qi + BLOCK - kj, qi - kj).astype(F32)
        slopes = _alibi_slopes()
        for h in range(N_HEADS):
            alibi = -float(slopes[h]) * distf
            bias_ref[1, h] = alibi
            bias_ref[0, h] = jnp.where(from_prev, NEG, alibi)

    has_prev = jnp.minimum(pl.program_id(1), 1)

    lane = lax.broadcasted_iota(jnp.int32, (2 * BLOCK, LANES), 1)
    low = lane < HEAD_DIM
    low_q = lax.broadcasted_iota(jnp.int32, (BLOCK, LANES), 1) < HEAD_DIM
    for g in range(N_KV_A):
        p, half = divmod(g, 2)
        cols = slice(p * LANES, (p + 1) * LANES)
        k2 = jnp.concatenate([kp_ref[:, cols], kc_ref[:, cols]], axis=0).astype(F32)
        v2 = jnp.concatenate([vp_ref[:, cols], vc_ref[:, cols]], axis=0).astype(F32)
        k2r = pltpu.roll(k2, HEAD_DIM, 1)
        v2r = pltpu.roll(v2, HEAD_DIM, 1)
        own_low = low if half == 0 else jnp.logical_not(low)
        kd = jnp.where(own_low, k2, k2r).astype(BF16)
        vd = jnp.where(own_low, v2, v2r).astype(BF16)

        q = q_ref[:, g * GROUP_A * HEAD_DIM:(g + 1) * GROUP_A * HEAD_DIM] * SCALE
        rows = [_stack_pair(q[:, i * LANES:(i + 1) * LANES]) for i in range(GROUP_A // 2)]
        lhs = jnp.concatenate(rows, axis=0)
        s = lax.dot_general(lhs, kd, _NT, preferred_element_type=F32)
        ps, ls = [], []
        for hh in range(GROUP_A):
            h = g * GROUP_A + hh
            sb = s[hh * BLOCK:(hh + 1) * BLOCK]
            sh = jnp.where(from_prev, sb[:, :BLOCK], sb[:, BLOCK:]) + bias_ref[has_prev, h]
            sink = sink_ref[h]
            m = jnp.maximum(jnp.max(sh, axis=-1, keepdims=True), sink)
            e = jnp.exp(sh - m)
            ls.append(jnp.sum(e, axis=-1, keepdims=True) + jnp.exp(sink - m))
            ps.append(jnp.concatenate([jnp.where(from_prev, e, 0.0), jnp.where(from_prev, 0.0, e)],
                                      axis=1).astype(BF16))
        pv = jnp.dot(jnp.concatenate(ps, axis=0), vd, preferred_element_type=F32)
        for i in range(GROUP_A // 2):
            o0 = pv[(2 * i) * BLOCK:(2 * i + 1) * BLOCK] / ls[2 * i]
            o1 = pv[(2 * i + 1) * BLOCK:(2 * i + 2) * BLOCK] / ls[2 * i + 1]
            c0 = (g * GROUP_A // 2 + i) * LANES
            o_ref[:, c0:c0 + LANES] = jnp.where(low_q, o0, o1).astype(o_ref.dtype)


def _swa_attention(proj, sinks, batch, seq):
    m = proj.shape[0]
    nb = seq // BLOCK
    kcol = BRANCH // KV_A
    vcol = kcol + 1

    def cur(b, n):
        return b * nb + n

    def prev(b, n):
        return b * nb + jnp.maximum(n - 1, 0)

    return pl.pallas_call(
        functools.partial(_swa_kernel, nb=nb),
        grid=(batch, nb),
        in_specs=[
            pl.BlockSpec(memory_space=pltpu.SMEM),
            pl.BlockSpec((BLOCK, BRANCH), lambda b, n: (cur(b, n), 0)),
            pl.BlockSpec((BLOCK, KV_A), lambda b, n: (prev(b, n), kcol)),
            pl.BlockSpec((BLOCK, KV_A), lambda b, n: (cur(b, n), kcol)),
            pl.BlockSpec((BLOCK, KV_A), lambda b, n: (prev(b, n), vcol)),
            pl.BlockSpec((BLOCK, KV_A), lambda b, n: (cur(b, n), vcol)),
        ],
        out_specs=pl.BlockSpec((BLOCK, BRANCH), lambda b, n: (cur(b, n), 0)),
        out_shape=jax.ShapeDtypeStruct((m, BRANCH), BF16),
        scratch_shapes=[pltpu.VMEM((2, N_HEADS, BLOCK, BLOCK), F32)],
        compiler_params=_params(("arbitrary", "arbitrary")),
        name="swa_attention",
    )(sinks, proj, proj, proj, proj, proj)


class _Tiling(NamedTuple):
    tile: int
    pairs: int

    @property
    def groups(self):
        return N_PAIRS // self.pairs

    @property
    def heads(self):
        return 2 * self.pairs


STICK_TILING = _Tiling(tile=256, pairs=8)
FOX_TILING = _Tiling(tile=512, pairs=2)


def _head_split(x):
    lane = lax.broadcasted_iota(jnp.int32, x.shape, 1)
    zero = jnp.zeros_like(x)
    return jnp.concatenate([jnp.where(lane < HEAD_DIM, x, zero),
                            jnp.where(lane >= HEAD_DIM, x, zero)], axis=0)


def _chunk_rows(c, tile):
    return pl.ds(pl.multiple_of(c * tile, tile), tile)


def _causal_attention_call(kernel_fn, name, tiling, proj, extra_inputs, extra_specs, scratch,
                           batch, seq):
    m = proj.shape[0]
    tile, groups = tiling.tile, tiling.groups
    nq = seq // tile
    width = tiling.pairs * LANES
    return pl.pallas_call(
        functools.partial(kernel_fn, tiling=tiling),
        grid=(batch, groups, nq),
        in_specs=[
            pl.BlockSpec((tile, width), lambda b, g, i: (b * nq + i, g)),
            pl.BlockSpec((seq, width), lambda b, g, i: (b, groups + g)),
            pl.BlockSpec((seq, width), lambda b, g, i: (b, 2 * groups + g)),
        ] + extra_specs,
        out_specs=pl.BlockSpec((tile, width), lambda b, g, i: (b * nq + i, g)),
        out_shape=jax.ShapeDtypeStruct((m, BRANCH), BF16),
        scratch_shapes=scratch,
        compiler_params=_params(("parallel", "parallel", "arbitrary")),
        name=name,
    )(proj, proj, proj, *extra_inputs)


def _suffix_matrix():
    j = lax.broadcasted_iota(jnp.int32, (BLOCK, 2 * BLOCK), 0)
    s = lax.broadcasted_iota(jnp.int32, (BLOCK, 2 * BLOCK), 1)
    return jnp.where((s >= BLOCK) | (j > s), 1.0, 0.0).astype(BF16)


SOFTPLUS_CLAMP = 60.0
EXP_UNDERFLOW = 105.0


def _softplus(s):
    return jnp.maximum(s, jnp.log(1.0 + jnp.exp(jnp.minimum(s, SOFTPLUS_CLAMP))))


def _stick_kernel(q_ref, k_ref, v_ref, o_ref, acc_ref, carry_ref, *, tiling):
    tile, pairs = tiling
    nkb = tile // BLOCK
    i = pl.program_id(2)
    u = _suffix_matrix()

    row = lax.broadcasted_iota(jnp.int32, (tile, tile), 0)
    col = lax.broadcasted_iota(jnp.int32, (tile, tile), 1)
    before = jnp.concatenate([col < row, col < row], axis=1)

    def chunk(c, mask):
        rows = _chunk_rows(c, tile)
        for pr in range(pairs):
            cols = slice(pr * LANES, (pr + 1) * LANES)
            q2 = q_ref[:, cols] * SCALE
            s = lax.dot_general(q2, _head_split(k_ref[rows, cols]), _NT,
                                preferred_element_type=F32)
            sp = _softplus(s)
            if mask is not None:
                sp = jnp.where(mask, sp, 0.0)
            spb = sp.astype(BF16)
            base = s - sp
            ps = [None] * (2 * nkb)
            for h in range(2):
                carry = carry_ref[pr, h]
                for kb in reversed(range(nkb)):
                    sl = slice(h * tile + kb * BLOCK, h * tile + (kb + 1) * BLOCK)
                    cs = jnp.dot(spb[:, sl], u, preferred_element_type=F32)
                    ps[h * nkb + kb] = jnp.exp(base[:, sl] - cs[:, :BLOCK] - carry)
                    carry = carry + cs[:, BLOCK:]
                carry_ref[pr, h] = carry
            p = jnp.concatenate(ps, axis=1)
            if mask is not None:
                p = jnp.where(mask, p, 0.0)
            acc_ref[pr] += jnp.dot(p.astype(BF16), _head_split(v_ref[rows, cols]),
                                   preferred_element_type=F32)

    acc_ref[...] = jnp.zeros_like(acc_ref)
    carry_ref[...] = jnp.zeros_like(carry_ref)
    chunk(i, before)

    def weights_vanish():
        return (jnp.min(carry_ref[...]) >= EXP_UNDERFLOW).astype(jnp.int32)

    def cond(st):
        t, done = st
        return jnp.logical_and(t < i, done == 0)

    def body(st):
        t, _ = st
        chunk(i - 1 - t, None)
        return t + 1, weights_vanish()

    lax.while_loop(cond, body, (jnp.int32(0), weights_vanish()))
    for pr in range(pairs):
        o_ref[:, pr * LANES:(pr + 1) * LANES] = acc_ref[pr].astype(o_ref.dtype)


def _stick_attention(proj, batch, seq):
    tile, pairs = STICK_TILING
    scratch = [pltpu.VMEM((pairs, tile, LANES), F32),
               pltpu.VMEM((pairs, 2, tile, LANES), F32)]
    return _causal_attention_call(_stick_kernel, "stick_attention", STICK_TILING, proj, [], [],
                                  scratch, batch, seq)


def _split3(x):
    hi = x.astype(BF16)
    r = x - hi.astype(F32)
    mid = r.astype(BF16)
    lo = (r - mid.astype(F32)).astype(BF16)
    return hi, mid, lo


def _cum_kernel(fl_ref, bf_ref, cum_ref):
    x = fl_ref[...] + bf_ref[...]
    lf = jnp.minimum(x, 0.0) - jnp.log(1.0 + jnp.exp(-jnp.abs(x)))
    t = lax.broadcasted_iota(jnp.int32, (BLOCK, BLOCK), 0)
    j = lax.broadcasted_iota(jnp.int32, (BLOCK, BLOCK), 1)
    tri = jnp.where(j <= t, 1.0, 0.0).astype(BF16)
    parts = jnp.concatenate(_split3(lf), axis=1)
    total = jnp.zeros((1, LANES), F32)
    for blk in range(x.shape[0] // BLOCK):
        rows = slice(blk * BLOCK, (blk + 1) * BLOCK)
        c = jnp.dot(tri, parts[rows], preferred_element_type=F32)
        c = c[:, :LANES] + c[:, LANES:2 * LANES] + c[:, 2 * LANES:] + total
        cum_ref[rows, :] = c
        total = c[BLOCK - 1:BLOCK, :]


def _cum_log_forget(f_logit, b_f, batch, seq):
    return pl.pallas_call(
        _cum_kernel,
        grid=(batch,),
        in_specs=[pl.BlockSpec((seq, LANES), lambda b: (b, 0)),
                  pl.BlockSpec((1, LANES), lambda b: (0, 0))],
        out_specs=pl.BlockSpec((seq, LANES), lambda b: (b, 0)),
        out_shape=jax.ShapeDtypeStruct((batch * seq, LANES), F32),
        compiler_params=_params(("parallel",)),
        name="cum_log_forget",
    )(f_logit, b_f)


def _fox_kernel(q_ref, k_ref, v_ref, cq_ref, ck_ref, o_ref, u_ref, acc_ref, *, tiling):
    tile, pairs = tiling
    nl = tile // LANES
    i = pl.program_id(2)
    row = lax.broadcasted_iota(jnp.int32, (tile, tile), 0)
    col = lax.broadcasted_iota(jnp.int32, (tile, tile), 1)
    causal = col <= row
    cq_all = cq_ref[0, 0]
    r2 = lax.broadcasted_iota(jnp.int32, (2 * tile, LANES), 0)
    l2 = lax.broadcasted_iota(jnp.int32, (2 * tile, LANES), 1)
    ones = jnp.where((r2 < tile) == (l2 < HEAD_DIM), 1.0, 0.0).astype(BF16)

    def scores(c, tops, mask):
        rows = _chunk_rows(c, tile)
        ck_all = ck_ref[0, c, 0]
        new_tops = []
        for pr in range(pairs):
            cols = slice(pr * LANES, (pr + 1) * LANES)
            q2 = q_ref[:, cols] * SCALE
            s = lax.dot_general(q2, _head_split(k_ref[rows, cols]), _NT,
                                preferred_element_type=F32)
            for h in range(2):
                hd = 2 * pr + h
                u = (s[:, h * tile:(h + 1) * tile] - ck_all[hd:hd + 1, :]) * LOG2E
                if mask is not None:
                    u = jnp.where(mask, u, NEG)
                u_ref[pr, c, :, h * tile:(h + 1) * tile] = u
                top = tops[hd]
                for b in range(nl):
                    top = jnp.maximum(top, u[:, b * LANES:(b + 1) * LANES])
                new_tops.append(top)
        return tuple(new_tops)

    tops = tuple(jnp.full((tile, LANES), NEG, F32) for _ in range(2 * pairs))
    tops = lax.fori_loop(0, i, lambda c, t: scores(c, t, None), tops)
    tops = scores(i, tops, causal)

    shifts = []
    for hd in range(2 * pairs):
        cq = cq_all[:, hd:hd + 1] * LOG2E
        m = jnp.max(tops[hd], axis=-1, keepdims=True) + cq
        shifts.append(jnp.concatenate([jnp.broadcast_to(m - cq, (tile, LANES))] * nl, axis=1))

    def weigh(c, first):
        rows = _chunk_rows(c, tile)
        for pr in range(pairs):
            cols = slice(pr * LANES, (pr + 1) * LANES)
            ps = [jnp.exp2(u_ref[pr, c, :, h * tile:(h + 1) * tile] - shifts[2 * pr + h]).astype(BF16)
                  for h in range(2)]
            rhs = jnp.concatenate([_head_split(v_ref[rows, cols]), ones], axis=1)
            pv = jnp.dot(jnp.concatenate(ps, axis=1), rhs,
                         preferred_element_type=F32)
            acc_ref[pr] = pv if first else acc_ref[pr] + pv

    weigh(i, True)

    def body(c, _):
        weigh(c, False)
        return 0

    lax.fori_loop(0, i, body, 0)
    for pr in range(pairs):
        acc = acc_ref[pr]
        o_ref[:, pr * LANES:(pr + 1) * LANES] = (acc[:, :LANES] / acc[:, LANES:]).astype(o_ref.dtype)


def _fox_attention(proj, cum, batch, seq):
    tile, groups, heads = FOX_TILING.tile, FOX_TILING.groups, FOX_TILING.heads
    nc = seq // tile
    c = cum.reshape(batch, seq, LANES)[:, :, :N_HEADS]
    cq = c.reshape(batch, seq, groups, heads).transpose(0, 2, 1, 3)
    ck = c.reshape(batch, nc, tile, groups, heads).transpose(0, 1, 3, 4, 2)
    specs = [
        pl.BlockSpec((1, 1, tile, heads), lambda b, g, i: (b, g, i, 0)),
        pl.BlockSpec((1, nc, 1, heads, tile), lambda b, g, i: (b, 0, g, 0, 0)),
    ]
    pairs = FOX_TILING.pairs
    scratch = [pltpu.VMEM((pairs, nc, tile, 2 * tile), F32),
               pltpu.VMEM((pairs, tile, 2 * LANES), F32)]
    return _causal_attention_call(_fox_kernel, "fox_attention", FOX_TILING, proj, [cq, ck], specs,
                                  scratch, batch, seq)


def kernel(x, g_pre, g_post, w_in_a, w_out_a, sinks_a, w_in_b, w_out_b, w_in_c, b_f_c, w_out_c):
    batch, seq, d = x.shape
    depth = g_pre.shape[0]
    xf = x.reshape(batch * seq, d)
    wa, wb = w_in_a.astype(BF16), w_in_b.astype(BF16)
    wc = w_in_c[:, :, :4 * BRANCH].astype(BF16)
    wc_forget = w_in_c[:, :, 4 * BRANCH:].astype(BF16)
    for i in range(depth):
        kind, j = i % 3, i // 3
        if kind == 0:
            proj = _norm_proj(xf, g_pre[i], wa, j, BRANCH + 2 * KV_A + BRANCH, BF16, PROJ_TN_A)
            o = _swa_attention(proj, sinks_a[j], batch, seq)
            z_col, w_out = BRANCH + 2 * KV_A, w_out_a[j]
        elif kind == 1:
            proj = _norm_proj(xf, g_pre[i], wb, j, 4 * BRANCH, BF16, PROJ_TN)
            o = _stick_attention(proj, batch, seq)
            z_col, w_out = 3 * BRANCH, w_out_b[j]
        else:
            proj = _norm_proj(xf, g_pre[i], wc, j, 4 * BRANCH, BF16, PROJ_TN)
            w_f = jnp.pad(wc_forget[j], ((0, 0), (0, LANES - N_HEADS)))[None]
            f_logit = _norm_proj(xf, g_pre[i], w_f, 0, LANES, F32, LANES)
            b_f = jnp.pad(b_f_c[j], (0, LANES - N_HEADS)).reshape(1, LANES)
            cum = _cum_log_forget(f_logit, b_f, batch, seq)
            o = _fox_attention(proj, cum, batch, seq)
            z_col, w_out = 3 * BRANCH, w_out_c[j]
        xf = _gate_out(o, proj, z_col, w_out.astype(BF16), g_post[i], xf)
    return xf.reshape(batch, seq, d)
```

```python
import functools
from typing import NamedTuple

import jax
import jax.numpy as jnp
import numpy as np
from jax import lax
from jax.experimental import pallas as pl
from jax.experimental.pallas import tpu as pltpu

D_MODEL = 2048
HEAD_DIM = 64
N_HEADS = 32
BRANCH = N_HEADS * HEAD_DIM
N_KV_A = 4
KV_A = N_KV_A * HEAD_DIM
GROUP_A = N_HEADS // N_KV_A
BLOCK = 128
LANES = 128
NORM_EPS = 1e-6
NEG = -1e30
SCALE = HEAD_DIM ** -0.5
LOG2E = 1.4426950408889634
N_PAIRS = N_HEADS // 2

F32 = jnp.float32
BF16 = jnp.bfloat16

VMEM_LIMIT = 52 * 1024 * 1024
PROJ_TM = 1024
PROJ_TN_A, PROJ_TN = 1536, 2048
GATE_TM = 512

_NT = (((1,), (1,)), ((), ()))


def _params(sem):
    return pltpu.CompilerParams(dimension_semantics=sem, vmem_limit_bytes=VMEM_LIMIT)


def _rmsnorm(x, g):
    r = lax.rsqrt(jnp.mean(x * x, axis=-1, keepdims=True) + NORM_EPS)
    return x * r * g


def _norm_proj_kernel(x_ref, g_ref, w_ref, o_ref, h_ref):
    @pl.when(pl.program_id(1) == 0)
    def _():
        h_ref[...] = _rmsnorm(x_ref[...], g_ref[...]).astype(BF16)

    o_ref[...] = jnp.dot(h_ref[...], w_ref[...], preferred_element_type=F32).astype(o_ref.dtype)


def _norm_proj(x, g, w, layer, n, out_dtype, tn, tm=PROJ_TM):
    m, d = x.shape
    return pl.pallas_call(
        _norm_proj_kernel,
        grid=(m // tm, n // tn),
        in_specs=[
            pl.BlockSpec((tm, d), lambda i, j: (i, 0)),
            pl.BlockSpec((1, d), lambda i, j: (0, 0)),
            pl.BlockSpec((None, d, tn), lambda i, j: (layer, 0, j)),
        ],
        out_specs=pl.BlockSpec((tm, tn), lambda i, j: (i, j)),
        out_shape=jax.ShapeDtypeStruct((m, n), out_dtype),
        scratch_shapes=[pltpu.VMEM((tm, d), BF16)],
        compiler_params=_params(("parallel", "arbitrary")),
        name="norm_proj",
    )(x, g.reshape(1, d), w)


Z_SPLIT = 4
Z_BLOCK = BRANCH // Z_SPLIT


def _gate_out_kernel(o_ref, *refs):
    z_refs, (w_ref, g_ref, x_ref, out_ref) = refs[:Z_SPLIT], refs[Z_SPLIT:]
    z = jnp.concatenate([r[...] for r in z_refs], axis=1).astype(F32)
    gated = (o_ref[...].astype(F32) * (z * jax.nn.sigmoid(z))).astype(BF16)
    y = jnp.dot(gated, w_ref[...], preferred_element_type=F32)
    out_ref[...] = x_ref[...] + _rmsnorm(y, g_ref[...])


def _gate_out(o, proj, z_col, w_out, g, x, tm=GATE_TM):
    m, d = x.shape
    z0 = z_col // Z_BLOCK
    z_specs = [pl.BlockSpec((tm, Z_BLOCK), lambda i, c=c: (i, z0 + c)) for c in range(Z_SPLIT)]
    return pl.pallas_call(
        _gate_out_kernel,
        grid=(m // tm,),
        in_specs=[pl.BlockSpec((tm, BRANCH), lambda i: (i, 0))] + z_specs + [
            pl.BlockSpec((BRANCH, d), lambda i: (0, 0)),
            pl.BlockSpec((1, d), lambda i: (0, 0)),
            pl.BlockSpec((tm, d), lambda i: (i, 0)),
        ],
        out_specs=pl.BlockSpec((tm, d), lambda i: (i, 0)),
        out_shape=jax.ShapeDtypeStruct((m, d), F32),
        compiler_params=_params(("parallel",)),
        name="gate_out",
    )(o, *([proj] * Z_SPLIT), w_out, g.reshape(1, d), x)


def _stack_pair(q):
    lane = lax.broadcasted_iota(jnp.int32, q.shape, 1)
    zero = jnp.zeros_like(q)
    return jnp.concatenate([jnp.where(lane < HEAD_DIM, q, zero),
                            jnp.where(lane >= HEAD_DIM, q, zero)], axis=0)


def _alibi_slopes():
    n = N_HEADS
    return (2.0 ** (-8.0 * np.arange(1, n + 1, dtype=np.float32) / n)).astype(np.float32)


def _swa_kernel(sink_ref, q_ref, kp_ref, kc_ref, vp_ref, vc_ref, o_ref, bias_ref, *, nb):
    qi = lax.broadcasted_iota(jnp.int32, (BLOCK, BLOCK), 0)
    kj = lax.broadcasted_iota(jnp.int32, (BLOCK, BLOCK), 1)
    from_prev = kj > qi
    first = (pl.program_id(0) == 0) & (pl.program_id(1) == 0)

    @pl.when(first)
    def _():
        distf = jnp.where(from_prev, qi + BLOCK - kj, qi - kj).astype(F32)
        slopes = _alibi_slopes()
        for h in range(N_HEADS):
            alibi = -float(slopes[h]) * distf
            bias_ref[1, h] = alibi
            bias_ref[0, h] = jnp.where(from_prev, NEG, alibi)

    has_prev = jnp.minimum(pl.program_id(1), 1)

    lane = lax.broadcasted_iota(jnp.int32, (2 * BLOCK, LANES), 1)
    low = lane < HEAD_DIM
    low_q = lax.broadcasted_iota(jnp.int32, (BLOCK, LANES), 1) < HEAD_DIM
    for g in range(N_KV_A):
        p, half = divmod(g, 2)
        cols = slice(p * LANES, (p + 1) * LANES)
        k2 = jnp.concatenate([kp_ref[:, cols], kc_ref[:, cols]], axis=0).astype(F32)
        v2 = jnp.concatenate([vp_ref[:, cols], vc_ref[:, cols]], axis=0).astype(F32)
        k2r = pltpu.roll(k2, HEAD_DIM, 1)
        v2r = pltpu.roll(v2, HEAD_DIM, 1)
        own_low = low if half == 0 else jnp.logical_not(low)
        kd = jnp.where(own_low, k2, k2r).astype(BF16)
        vd = jnp.where(own_low, v2, v2r).astype(BF16)

        q = q_ref[:, g * GROUP_A * HEAD_DIM:(g + 1) * GROUP_A * HEAD_DIM] * SCALE
        rows = [_stack_pair(q[:, i * LANES:(i + 1) * LANES]) for i in range(GROUP_A // 2)]
        lhs = jnp.concatenate(rows, axis=0)
        s = lax.dot_general(lhs, kd, _NT, preferred_element_type=F32)
        ps, ls = [], []
        for hh in range(GROUP_A):
            h = g * GROUP_A + hh
            sb = s[hh * BLOCK:(hh + 1) * BLOCK]
            sh = jnp.where(from_prev, sb[:, :BLOCK], sb[:, BLOCK:]) + bias_ref[has_prev, h]
            sink = sink_ref[h]
            m = jnp.maximum(jnp.max(sh, axis=-1, keepdims=True), sink)
            e = jnp.exp(sh - m)
            ls.append(jnp.sum(e, axis=-1, keepdims=True) + jnp.exp(sink - m))
            ps.append(jnp.concatenate([jnp.where(from_prev, e, 0.0), jnp.where(from_prev, 0.0, e)],
                                      axis=1).astype(BF16))
        pv = jnp.dot(jnp.concatenate(ps, axis=0), vd, preferred_element_type=F32)
        for i in range(GROUP_A // 2):
            o0 = pv[(2 * i) * BLOCK:(2 * i + 1) * BLOCK] / ls[2 * i]
            o1 = pv[(2 * i + 1) * BLOCK:(2 * i + 2) * BLOCK] / ls[2 * i + 1]
            c0 = (g * GROUP_A // 2 + i) * LANES
            o_ref[:, c0:c0 + LANES] = jnp.where(low_q, o0, o1).astype(o_ref.dtype)


def _swa_attention(proj, sinks, batch, seq):
    m = proj.shape[0]
    nb = seq // BLOCK
    kcol = BRANCH // KV_A
    vcol = kcol + 1

    def cur(b, n):
        return b * nb + n

    def prev(b, n):
        return b * nb + jnp.maximum(n - 1, 0)

    return pl.pallas_call(
        functools.partial(_swa_kernel, nb=nb),
        grid=(batch, nb),
        in_specs=[
            pl.BlockSpec(memory_space=pltpu.SMEM),
            pl.BlockSpec((BLOCK, BRANCH), lambda b, n: (cur(b, n), 0)),
            pl.BlockSpec((BLOCK, KV_A), lambda b, n: (prev(b, n), kcol)),
            pl.BlockSpec((BLOCK, KV_A), lambda b, n: (cur(b, n), kcol)),
            pl.BlockSpec((BLOCK, KV_A), lambda b, n: (prev(b, n), vcol)),
            pl.BlockSpec((BLOCK, KV_A), lambda b, n: (cur(b, n), vcol)),
        ],
        out_specs=pl.BlockSpec((BLOCK, BRANCH), lambda b, n: (cur(b, n), 0)),
        out_shape=jax.ShapeDtypeStruct((m, BRANCH), BF16),
        scratch_shapes=[pltpu.VMEM((2, N_HEADS, BLOCK, BLOCK), F32)],
        compiler_params=_params(("arbitrary", "arbitrary")),
        name="swa_attention",
    )(sinks, proj, proj, proj, proj, proj)


class _Tiling(NamedTuple):
    tile: int
    pairs: int

    @property
    def groups(self):
        return N_PAIRS // self.pairs

    @property
    def heads(self):
        return 2 * self.pairs


STICK_TILING = _Tiling(tile=256, pairs=8)
FOX_TILING = _Tiling(tile=512, pairs=2)


def _head_split(x):
    lane = lax.broadcasted_iota(jnp.int32, x.shape, 1)
    zero = jnp.zeros_like(x)
    return jnp.concatenate([jnp.where(lane < HEAD_DIM, x, zero),
                            jnp.where(lane >= HEAD_DIM, x, zero)], axis=0)


def _chunk_rows(c, tile):
    return pl.ds(pl.multiple_of(c * tile, tile), tile)


def _causal_attention_call(kernel_fn, name, tiling, proj, extra_inputs, extra_specs, scratch,
                           batch, seq):
    m = proj.shape[0]
    tile, groups = tiling.tile, tiling.groups
    nq = seq // tile
    width = tiling.pairs * LANES
    return pl.pallas_call(
        functools.partial(kernel_fn, tiling=tiling),
        grid=(batch, groups, nq),
        in_specs=[
            pl.BlockSpec((tile, width), lambda b, g, i: (b * nq + i, g)),
            pl.BlockSpec((seq, width), lambda b, g, i: (b, groups + g)),
            pl.BlockSpec((seq, width), lambda b, g, i: (b, 2 * groups + g)),
        ] + extra_specs,
        out_specs=pl.BlockSpec((tile, width), lambda b, g, i: (b * nq + i, g)),
        out_shape=jax.ShapeDtypeStruct((m, BRANCH), BF16),
        scratch_shapes=scratch,
        compiler_params=_params(("arbitrary", "arbitrary", "arbitrary")),
        name=name,
    )(proj, proj, proj, *extra_inputs)


def _suffix_matrix():
    j = lax.broadcasted_iota(jnp.int32, (BLOCK, 2 * BLOCK), 0)
    s = lax.broadcasted_iota(jnp.int32, (BLOCK, 2 * BLOCK), 1)
    return jnp.where((s >= BLOCK) | (j > s), 1.0, 0.0).astype(BF16)


SOFTPLUS_CLAMP = 60.0
EXP_UNDERFLOW = 105.0


def _softplus(s):
    return jnp.maximum(s, jnp.log(1.0 + jnp.exp(jnp.minimum(s, SOFTPLUS_CLAMP))))


def _stick_kernel(q_ref, k_ref, v_ref, o_ref, acc_ref, carry_ref, *, tiling):
    tile, pairs = tiling
    nkb = tile // BLOCK
    i = pl.program_id(2)
    u = _suffix_matrix()

    row = lax.broadcasted_iota(jnp.int32, (tile, tile), 0)
    col = lax.broadcasted_iota(jnp.int32, (tile, tile), 1)
    before = jnp.concatenate([col < row, col < row], axis=1)

    def chunk(c, mask):
        rows = _chunk_rows(c, tile)
        for pr in range(pairs):
            cols = slice(pr * LANES, (pr + 1) * LANES)
            q2 = q_ref[:, cols] * SCALE
            s = lax.dot_general(q2, _head_split(k_ref[rows, cols]), _NT,
                                preferred_element_type=F32)
            sp = _softplus(s)
            if mask is not None:
                sp = jnp.where(mask, sp, 0.0)
            spb = sp.astype(BF16)
            base = s - sp
            ps = [None] * (2 * nkb)
            for h in range(2):
                carry = carry_ref[pr, h]
                for kb in reversed(range(nkb)):
                    sl = slice(h * tile + kb * BLOCK, h * tile + (kb + 1) * BLOCK)
                    cs = jnp.dot(spb[:, sl], u, preferred_element_type=F32)
                    ps[h * nkb + kb] = jnp.exp(base[:, sl] - cs[:, :BLOCK] - carry)
                    carry = carry + cs[:, BLOCK:]
                carry_ref[pr, h] = carry
            p = jnp.concatenate(ps, axis=1)
            if mask is not None:
                p = jnp.where(mask, p, 0.0)
            acc_ref[pr] += jnp.dot(p.astype(BF16), _head_split(v_ref[rows, cols]),
                                   preferred_element_type=F32)

    acc_ref[...] = jnp.zeros_like(acc_ref)
    carry_ref[...] = jnp.zeros_like(carry_ref)
    chunk(i, before)

    def weights_vanish():
        return (jnp.min(carry_ref[...]) >= EXP_UNDERFLOW).astype(jnp.int32)

    def cond(st):
        t, done = st
        return jnp.logical_and(t < i, done == 0)

    def body(st):
        t, _ = st
        chunk(i - 1 - t, None)
        return t + 1, weights_vanish()

    lax.while_loop(cond, body, (jnp.int32(0), weights_vanish()))
    for pr in range(pairs):
        o_ref[:, pr * LANES:(pr + 1) * LANES] = acc_ref[pr].astype(o_ref.dtype)


def _stick_attention(proj, batch, seq):
    tile, pairs = STICK_TILING
    scratch = [pltpu.VMEM((pairs, tile, LANES), F32),
               pltpu.VMEM((pairs, 2, tile, LANES), F32)]
    return _causal_attention_call(_stick_kernel, "stick_attention", STICK_TILING, proj, [], [],
                                  scratch, batch, seq)


def _split3(x):
    hi = x.astype(BF16)
    r = x - hi.astype(F32)
    mid = r.astype(BF16)
    lo = (r - mid.astype(F32)).astype(BF16)
    return hi, mid, lo


def _cum_kernel(fl_ref, bf_ref, cum_ref):
    x = fl_ref[...] + bf_ref[...]
    lf = jnp.minimum(x, 0.0) - jnp.log(1.0 + jnp.exp(-jnp.abs(x)))
    t = lax.broadcasted_iota(jnp.int32, (BLOCK, BLOCK), 0)
    j = lax.broadcasted_iota(jnp.int32, (BLOCK, BLOCK), 1)
    tri = jnp.where(j <= t, 1.0, 0.0).astype(BF16)
    parts = jnp.concatenate(_split3(lf), axis=1)
    total = jnp.zeros((1, LANES), F32)
    for blk in range(x.shape[0] // BLOCK):
        rows = slice(blk * BLOCK, (blk + 1) * BLOCK)
        c = jnp.dot(tri, parts[rows], preferred_element_type=F32)
        c = c[:, :LANES] + c[:, LANES:2 * LANES] + c[:, 2 * LANES:] + total
        cum_ref[rows, :] = c
        total = c[BLOCK - 1:BLOCK, :]


def _cum_log_forget(f_logit, b_f, batch, seq):
    return pl.pallas_call(
        _cum_kernel,
        grid=(batch,),
        in_specs=[pl.BlockSpec((seq, LANES), lambda b: (b, 0)),
                  pl.BlockSpec((1, LANES), lambda b: (0, 0))],
        out_specs=pl.BlockSpec((seq, LANES), lambda b: (b, 0)),
        out_shape=jax.ShapeDtypeStruct((batch * seq, LANES), F32),
        compiler_params=_params(("parallel",)),
        name="cum_log_forget",
    )(f_logit, b_f)


def _fox_kernel(q_ref, k_ref, v_ref, cq_ref, ck_ref, o_ref, u_ref, acc_ref, top_ref, mask_ref,
                ones_ref, *, tiling):
    tile, pairs = tiling
    nl = tile // LANES
    i = pl.program_id(2)
    cq_all = cq_ref[0, 0]

    @pl.when((pl.program_id(0) == 0) & (pl.program_id(1) == 0) & (i == 0))
    def _():
        row = lax.broadcasted_iota(jnp.int32, (tile, tile), 0)
        col = lax.broadcasted_iota(jnp.int32, (tile, tile), 1)
        mask_ref[...] = jnp.where(col <= row, 0.0, NEG)
        r2 = lax.broadcasted_iota(jnp.int32, (2 * tile, LANES), 0)
        l2 = lax.broadcasted_iota(jnp.int32, (2 * tile, LANES), 1)
        ones_ref[...] = jnp.where((r2 < tile) == (l2 < HEAD_DIM), 1.0, 0.0).astype(BF16)

    def scores(c, diagonal):
        rows = _chunk_rows(c, tile)
        ck_all = ck_ref[0, c, 0]
        for pr in range(pairs):
            cols = slice(pr * LANES, (pr + 1) * LANES)
            q2 = q_ref[:, cols] * SCALE
            s = lax.dot_general(q2, _head_split(k_ref[rows, cols]), _NT,
                                preferred_element_type=F32)
            for h in range(2):
                hd = 2 * pr + h
                u = (s[:, h * tile:(h + 1) * tile] - ck_all[hd:hd + 1, :]) * LOG2E
                if diagonal:
                    u = u + mask_ref[...]
                u_ref[pr, c, :, h * tile:(h + 1) * tile] = u
                top = top_ref[hd]
                for b in range(nl):
                    top = jnp.maximum(top, u[:, b * LANES:(b + 1) * LANES])
                top_ref[hd] = top

    top_ref[...] = jnp.full_like(top_ref, NEG)

    def pass1(c, _):
        scores(c, False)
        return 0

    lax.fori_loop(0, i, pass1, 0)
    scores(i, True)

    shifts = []
    for hd in range(2 * pairs):
        cq = cq_all[:, hd:hd + 1] * LOG2E
        m = jnp.max(top_ref[hd], axis=-1, keepdims=True) + cq
        shifts.append(jnp.concatenate([jnp.broadcast_to(m - cq, (tile, LANES))] * nl, axis=1))

    def weigh(c, first):
        rows = _chunk_rows(c, tile)
        for pr in range(pairs):
            cols = slice(pr * LANES, (pr + 1) * LANES)
            ps = [jnp.exp2(u_ref[pr, c, :, h * tile:(h + 1) * tile] - shifts[2 * pr + h]).astype(BF16)
                  for h in range(2)]
            rhs = jnp.concatenate([_head_split(v_ref[rows, cols]), ones_ref[...]], axis=1)
            pv = jnp.dot(jnp.concatenate(ps, axis=1), rhs,
                         preferred_element_type=F32)
            acc_ref[pr] = pv if first else acc_ref[pr] + pv

    weigh(i, True)

    def body(c, _):
        weigh(c, False)
        return 0

    lax.fori_loop(0, i, body, 0)
    for pr in range(pairs):
        acc = acc_ref[pr]
        o_ref[:, pr * LANES:(pr + 1) * LANES] = (acc[:, :LANES] / acc[:, LANES:]).astype(o_ref.dtype)


def _fox_attention(proj, cum, batch, seq):
    tile, groups, heads = FOX_TILING.tile, FOX_TILING.groups, FOX_TILING.heads
    nc = seq // tile
    c = cum.reshape(batch, seq, LANES)[:, :, :N_HEADS]
    cq = c.reshape(batch, seq, groups, heads).transpose(0, 2, 1, 3)
    ck = c.reshape(batch, nc, tile, groups, heads).transpose(0, 1, 3, 4, 2)
    specs = [
        pl.BlockSpec((1, 1, tile, heads), lambda b, g, i: (b, g, i, 0)),
        pl.BlockSpec((1, nc, 1, heads, tile), lambda b, g, i: (b, 0, g, 0, 0)),
    ]
    pairs = FOX_TILING.pairs
    scratch = [pltpu.VMEM((pairs, nc, tile, 2 * tile), F32),
               pltpu.VMEM((pairs, tile, 2 * LANES), F32),
               pltpu.VMEM((heads, tile, LANES), F32),
               pltpu.VMEM((tile, tile), F32),
               pltpu.VMEM((2 * tile, LANES), BF16)]
    return _causal_attention_call(_fox_kernel, "fox_attention", FOX_TILING, proj, [cq, ck], specs,
                                  scratch, batch, seq)


def kernel(x, g_pre, g_post, w_in_a, w_out_a, sinks_a, w_in_b, w_out_b, w_in_c, b_f_c, w_out_c):
    batch, seq, d = x.shape
    depth = g_pre.shape[0]
    xf = x.reshape(batch * seq, d)
    wa, wb, wc = w_in_a.astype(BF16), w_in_b.astype(BF16), w_in_c.astype(BF16)
    for i in range(depth):
        kind, j = i % 3, i // 3
        if kind == 0:
            proj = _norm_proj(xf, g_pre[i], wa, j, BRANCH + 2 * KV_A + BRANCH, BF16, PROJ_TN_A)
            o = _swa_attention(proj, sinks_a[j], batch, seq)
            z_col, w_out = BRANCH + 2 * KV_A, w_out_a[j]
        elif kind == 1:
            proj = _norm_proj(xf, g_pre[i], wb, j, 4 * BRANCH, BF16, PROJ_TN)
            o = _stick_attention(proj, batch, seq)
            z_col, w_out = 3 * BRANCH, w_out_b[j]
        else:
            proj = _norm_proj(xf, g_pre[i], wc, j, 4 * BRANCH, BF16, PROJ_TN)
            w_f = jnp.pad(wc[j, :, 4 * BRANCH:], ((0, 0), (0, LANES - N_HEADS)))[None]
            f_logit = _norm_proj(xf, g_pre[i], w_f, 0, LANES, F32, LANES)
            b_f = jnp.pad(b_f_c[j], (0, LANES - N_HEADS)).reshape(1, LANES)
            cum = _cum_log_forget(f_logit, b_f, batch, seq)
            o = _fox_attention(proj, cum, batch, seq)
            z_col, w_out = 3 * BRANCH, w_out_c[j]
        xf = _gate_out(o, proj, z_col, w_out.astype(BF16), g_post[i], xf)
    return xf.reshape(batch, seq, d)
```

```python
import functools
from typing import NamedTuple

import jax
import jax.numpy as jnp
import numpy as np
from jax import lax
from jax.experimental import pallas as pl
from jax.experimental.pallas import tpu as pltpu

HEAD_DIM = 64
N_HEADS = 32
BRANCH = N_HEADS * HEAD_DIM
N_KV_A = 4
KV_A = N_KV_A * HEAD_DIM
GROUP_A = N_HEADS // N_KV_A
BLOCK = 128
LANES = 128
NORM_EPS = 1e-6
NEG = -1e30
SCALE = HEAD_DIM ** -0.5
LOG2E = 1.4426950408889634
N_PAIRS = N_HEADS // 2

F32 = jnp.float32
BF16 = jnp.bfloat16

VMEM_LIMIT = 52 * 1024 * 1024
PROJ_TM = 1024
PROJ_TN_A, PROJ_TN = 1536, 2048
GATE_TM = 512

_NT = (((1,), (1,)), ((), ()))


def _params(sem):
    return pltpu.CompilerParams(dimension_semantics=sem, vmem_limit_bytes=VMEM_LIMIT)


def _rmsnorm(x, g):
    r = lax.rsqrt(jnp.mean(x * x, axis=-1, keepdims=True) + NORM_EPS)
    return x * r * g


def _norm_proj_kernel(x_ref, g_ref, w_ref, o_ref, h_ref):
    @pl.when(pl.program_id(1) == 0)
    def _():
        h_ref[...] = _rmsnorm(x_ref[...], g_ref[...]).astype(BF16)

    o_ref[...] = jnp.dot(h_ref[...], w_ref[...], preferred_element_type=F32).astype(o_ref.dtype)


def _norm_proj(x, g, w, layer, n, out_dtype, tn, tm=PROJ_TM):
    m, d = x.shape
    return pl.pallas_call(
        _norm_proj_kernel,
        grid=(m // tm, n // tn),
        in_specs=[
            pl.BlockSpec((tm, d), lambda i, j: (i, 0)),
            pl.BlockSpec((1, d), lambda i, j: (0, 0)),
            pl.BlockSpec((None, d, tn), lambda i, j: (layer, 0, j)),
        ],
        out_specs=pl.BlockSpec((tm, tn), lambda i, j: (i, j)),
        out_shape=jax.ShapeDtypeStruct((m, n), out_dtype),
        scratch_shapes=[pltpu.VMEM((tm, d), BF16)],
        compiler_params=_params(("parallel", "arbitrary")),
        name="norm_proj",
    )(x, g.reshape(1, d), w)


Z_SPLIT = 4
Z_BLOCK = BRANCH // Z_SPLIT


def _gate_out_kernel(o_ref, *refs):
    z_refs, (w_ref, g_ref, x_ref, out_ref) = refs[:Z_SPLIT], refs[Z_SPLIT:]
    z = jnp.concatenate([r[...] for r in z_refs], axis=1).astype(F32)
    gated = (o_ref[...].astype(F32) * (z * jax.nn.sigmoid(z))).astype(BF16)
    y = jnp.dot(gated, w_ref[...], preferred_element_type=F32)
    out_ref[...] = x_ref[...] + _rmsnorm(y, g_ref[...])


def _gate_out(o, proj, z_col, w_out, g, x, tm=GATE_TM):
    m, d = x.shape
    z0 = z_col // Z_BLOCK
    z_specs = [pl.BlockSpec((tm, Z_BLOCK), lambda i, c=c: (i, z0 + c)) for c in range(Z_SPLIT)]
    return pl.pallas_call(
        _gate_out_kernel,
        grid=(m // tm,),
        in_specs=[pl.BlockSpec((tm, BRANCH), lambda i: (i, 0))] + z_specs + [
            pl.BlockSpec((BRANCH, d), lambda i: (0, 0)),
            pl.BlockSpec((1, d), lambda i: (0, 0)),
            pl.BlockSpec((tm, d), lambda i: (i, 0)),
        ],
        out_specs=pl.BlockSpec((tm, d), lambda i: (i, 0)),
        out_shape=jax.ShapeDtypeStruct((m, d), F32),
        compiler_params=_params(("parallel",)),
        name="gate_out",
    )(o, *([proj] * Z_SPLIT), w_out, g.reshape(1, d), x)


def _stack_pair(q):
    lane = lax.broadcasted_iota(jnp.int32, q.shape, 1)
    zero = jnp.zeros_like(q)
    return jnp.concatenate([jnp.where(lane < HEAD_DIM, q, zero),
                            jnp.where(lane >= HEAD_DIM, q, zero)], axis=0)


def _alibi_slopes():
    n = N_HEADS
    return (2.0 ** (-8.0 * np.arange(1, n + 1, dtype=np.float32) / n)).astype(np.float32)


def _swa_kernel(sink_ref, q_ref, kp_ref, kc_ref, vp_ref, vc_ref, o_ref, bias_ref):
    qi = lax.broadcasted_iota(jnp.int32, (BLOCK, BLOCK), 0)
    kj = lax.broadcasted_iota(jnp.int32, (BLOCK, BLOCK), 1)
    from_prev = kj > qi
    first = (pl.program_id(0) == 0) & (pl.program_id(1) == 0)

    @pl.when(first)
    def _():
        distf = jnp.where(from_prev, qi + BLOCK - kj, qi - kj).astype(F32)
        slopes = _alibi_slopes()
        for h in range(N_HEADS):
            alibi = -float(slopes[h]) * distf
            bias_ref[1, h] = alibi
            bias_ref[0, h] = jnp.where(from_prev, NEG, alibi)

    has_prev = jnp.minimum(pl.program_id(1), 1)

    lane = lax.broadcasted_iota(jnp.int32, (2 * BLOCK, LANES), 1)
    low = lane < HEAD_DIM
    low_q = lax.broadcasted_iota(jnp.int32, (BLOCK, LANES), 1) < HEAD_DIM
    for g in range(N_KV_A):
        p, half = divmod(g, 2)
        cols = slice(p * LANES, (p + 1) * LANES)
        k2 = jnp.concatenate([kp_ref[:, cols], kc_ref[:, cols]], axis=0).astype(F32)
        v2 = jnp.concatenate([vp_ref[:, cols], vc_ref[:, cols]], axis=0).astype(F32)
        k2r = pltpu.roll(k2, HEAD_DIM, 1)
        v2r = pltpu.roll(v2, HEAD_DIM, 1)
        own_low = low if half == 0 else jnp.logical_not(low)
        kd = jnp.where(own_low, k2, k2r).astype(BF16)
        vd = jnp.where(own_low, v2, v2r).astype(BF16)

        q = q_ref[:, g * GROUP_A * HEAD_DIM:(g + 1) * GROUP_A * HEAD_DIM] * SCALE
        rows = [_stack_pair(q[:, i * LANES:(i + 1) * LANES]) for i in range(GROUP_A // 2)]
        lhs = jnp.concatenate(rows, axis=0)
        s = lax.dot_general(lhs, kd, _NT, preferred_element_type=F32)
        ps, ls = [], []
        for hh in range(GROUP_A):
            h = g * GROUP_A + hh
            sb = s[hh * BLOCK:(hh + 1) * BLOCK]
            sh = jnp.where(from_prev, sb[:, :BLOCK], sb[:, BLOCK:]) + bias_ref[has_prev, h]
            sink = sink_ref[h]
            m = jnp.maximum(jnp.max(sh, axis=-1, keepdims=True), sink)
            e = jnp.exp(sh - m)
            ls.append(jnp.sum(e, axis=-1, keepdims=True) + jnp.exp(sink - m))
            ps.append(jnp.concatenate([jnp.where(from_prev, e, 0.0), jnp.where(from_prev, 0.0, e)],
                                      axis=1).astype(BF16))
        pv = jnp.dot(jnp.concatenate(ps, axis=0), vd, preferred_element_type=F32)
        for i in range(GROUP_A // 2):
            o0 = pv[(2 * i) * BLOCK:(2 * i + 1) * BLOCK] / ls[2 * i]
            o1 = pv[(2 * i + 1) * BLOCK:(2 * i + 2) * BLOCK] / ls[2 * i + 1]
            c0 = (g * GROUP_A // 2 + i) * LANES
            o_ref[:, c0:c0 + LANES] = jnp.where(low_q, o0, o1).astype(o_ref.dtype)


def _swa_attention(proj, sinks, batch, seq):
    m = proj.shape[0]
    nb = seq // BLOCK
    kcol = BRANCH // KV_A
    vcol = kcol + 1

    def cur(b, n):
        return b * nb + n

    def prev(b, n):
        return b * nb + jnp.maximum(n - 1, 0)

    return pl.pallas_call(
        _swa_kernel,
        grid=(batch, nb),
        in_specs=[
            pl.BlockSpec(memory_space=pltpu.SMEM),
            pl.BlockSpec((BLOCK, BRANCH), lambda b, n: (cur(b, n), 0)),
            pl.BlockSpec((BLOCK, KV_A), lambda b, n: (prev(b, n), kcol)),
            pl.BlockSpec((BLOCK, KV_A), lambda b, n: (cur(b, n), kcol)),
            pl.BlockSpec((BLOCK, KV_A), lambda b, n: (prev(b, n), vcol)),
            pl.BlockSpec((BLOCK, KV_A), lambda b, n: (cur(b, n), vcol)),
        ],
        out_specs=pl.BlockSpec((BLOCK, BRANCH), lambda b, n: (cur(b, n), 0)),
        out_shape=jax.ShapeDtypeStruct((m, BRANCH), BF16),
        scratch_shapes=[pltpu.VMEM((2, N_HEADS, BLOCK, BLOCK), F32)],
        compiler_params=_params(("arbitrary", "arbitrary")),
        name="swa_attention",
    )(sinks, proj, proj, proj, proj, proj)


class _Tiling(NamedTuple):
    tile: int
    pairs: int

    @property
    def groups(self):
        return N_PAIRS // self.pairs

    @property
    def heads(self):
        return 2 * self.pairs


STICK_TILING = _Tiling(tile=256, pairs=8)
FOX_TILING = _Tiling(tile=512, pairs=2)


def _head_split(x):
    lane = lax.broadcasted_iota(jnp.int32, x.shape, 1)
    zero = jnp.zeros_like(x)
    return jnp.concatenate([jnp.where(lane < HEAD_DIM, x, zero),
                            jnp.where(lane >= HEAD_DIM, x, zero)], axis=0)


def _chunk_rows(c, tile):
    return pl.ds(pl.multiple_of(c * tile, tile), tile)


def _causal_attention_call(kernel_fn, name, tiling, proj, extra_inputs, extra_specs, scratch,
                           batch, seq):
    m = proj.shape[0]
    tile, groups = tiling.tile, tiling.groups
    nq = seq // tile
    width = tiling.pairs * LANES
    return pl.pallas_call(
        functools.partial(kernel_fn, tiling=tiling),
        grid=(batch, groups, nq),
        in_specs=[
            pl.BlockSpec((tile, width), lambda b, g, i: (b * nq + i, g)),
            pl.BlockSpec((seq, width), lambda b, g, i: (b, groups + g)),
            pl.BlockSpec((seq, width), lambda b, g, i: (b, 2 * groups + g)),
        ] + extra_specs,
        out_specs=pl.BlockSpec((tile, width), lambda b, g, i: (b * nq + i, g)),
        out_shape=jax.ShapeDtypeStruct((m, BRANCH), BF16),
        scratch_shapes=scratch,
        compiler_params=_params(("arbitrary", "arbitrary", "arbitrary")),
        name=name,
    )(proj, proj, proj, *extra_inputs)


def _suffix_matrix():
    j = lax.broadcasted_iota(jnp.int32, (BLOCK, 2 * BLOCK), 0)
    s = lax.broadcasted_iota(jnp.int32, (BLOCK, 2 * BLOCK), 1)
    return jnp.where((s >= BLOCK) | (j > s), 1.0, 0.0).astype(BF16)


SOFTPLUS_CLAMP = 60.0
EXP_UNDERFLOW = 105.0


def _softplus(s):
    return jnp.maximum(s, jnp.log(1.0 + jnp.exp(jnp.minimum(s, SOFTPLUS_CLAMP))))


def _stick_kernel(q_ref, k_ref, v_ref, o_ref, acc_ref, carry_ref, *, tiling):
    tile, pairs = tiling
    nkb = tile // BLOCK
    i = pl.program_id(2)
    u = _suffix_matrix()

    row = lax.broadcasted_iota(jnp.int32, (tile, tile), 0)
    col = lax.broadcasted_iota(jnp.int32, (tile, tile), 1)
    before = jnp.concatenate([col < row, col < row], axis=1)

    def chunk(c, mask):
        rows = _chunk_rows(c, tile)
        for pr in range(pairs):
            cols = slice(pr * LANES, (pr + 1) * LANES)
            q2 = q_ref[:, cols] * SCALE
            s = lax.dot_general(q2, _head_split(k_ref[rows, cols]), _NT,
                                preferred_element_type=F32)
            sp = _softplus(s)
            if mask is not None:
                sp = jnp.where(mask, sp, 0.0)
            spb = sp.astype(BF16)
            base = s - sp
            ps = [None] * (2 * nkb)
            for h in range(2):
                carry = carry_ref[pr, h]
                for kb in reversed(range(nkb)):
                    sl = slice(h * tile + kb * BLOCK, h * tile + (kb + 1) * BLOCK)
                    cs = jnp.dot(spb[:, sl], u, preferred_element_type=F32)
                    ps[h * nkb + kb] = jnp.exp(base[:, sl] - cs[:, :BLOCK] - carry)
                    carry = carry + cs[:, BLOCK:]
                carry_ref[pr, h] = carry
            p = jnp.concatenate(ps, axis=1)
            if mask is not None:
                p = jnp.where(mask, p, 0.0)
            acc_ref[pr] += jnp.dot(p.astype(BF16), _head_split(v_ref[rows, cols]),
                                   preferred_element_type=F32)

    acc_ref[...] = jnp.zeros_like(acc_ref)
    carry_ref[...] = jnp.zeros_like(carry_ref)
    chunk(i, before)

    def weights_vanish():
        return (jnp.min(carry_ref[...]) >= EXP_UNDERFLOW).astype(jnp.int32)

    def cond(st):
        t, done = st
        return jnp.logical_and(t < i, done == 0)

    def body(st):
        t, _ = st
        chunk(i - 1 - t, None)
        return t + 1, weights_vanish()

    lax.while_loop(cond, body, (jnp.int32(0), weights_vanish()))
    for pr in range(pairs):
        o_ref[:, pr * LANES:(pr + 1) * LANES] = acc_ref[pr].astype(o_ref.dtype)


def _stick_attention(proj, batch, seq):
    tile, pairs = STICK_TILING
    scratch = [pltpu.VMEM((pairs, tile, LANES), F32),
               pltpu.VMEM((pairs, 2, tile, LANES), F32)]
    return _causal_attention_call(_stick_kernel, "stick_attention", STICK_TILING, proj, [], [],
                                  scratch, batch, seq)


def _split3(x):
    hi = x.astype(BF16)
    r = x - hi.astype(F32)
    mid = r.astype(BF16)
    lo = (r - mid.astype(F32)).astype(BF16)
    return hi, mid, lo


def _cum_kernel(fl_ref, bf_ref, cum_ref):
    x = fl_ref[...] + bf_ref[...]
    lf = jnp.minimum(x, 0.0) - jnp.log(1.0 + jnp.exp(-jnp.abs(x)))
    t = lax.broadcasted_iota(jnp.int32, (BLOCK, BLOCK), 0)
    j = lax.broadcasted_iota(jnp.int32, (BLOCK, BLOCK), 1)
    tri = jnp.where(j <= t, 1.0, 0.0).astype(BF16)
    parts = jnp.concatenate(_split3(lf), axis=1)
    total = jnp.zeros((1, LANES), F32)
    for blk in range(x.shape[0] // BLOCK):
        rows = slice(blk * BLOCK, (blk + 1) * BLOCK)
        c = jnp.dot(tri, parts[rows], preferred_element_type=F32)
        c = c[:, :LANES] + c[:, LANES:2 * LANES] + c[:, 2 * LANES:] + total
        cum_ref[rows, :] = c
        total = c[BLOCK - 1:BLOCK, :]


def _cum_log_forget(f_logit, b_f, batch, seq):
    return pl.pallas_call(
        _cum_kernel,
        grid=(batch,),
        in_specs=[pl.BlockSpec((seq, LANES), lambda b: (b, 0)),
                  pl.BlockSpec((1, LANES), lambda b: (0, 0))],
        out_specs=pl.BlockSpec((seq, LANES), lambda b: (b, 0)),
        out_shape=jax.ShapeDtypeStruct((batch * seq, LANES), F32),
        compiler_params=_params(("parallel",)),
        name="cum_log_forget",
    )(f_logit, b_f)


def _fox_kernel(q_ref, k_ref, v_ref, cq_ref, ck_ref, o_ref, u_ref, acc_ref, top_ref, mask_ref,
                ones_ref, *, tiling):
    tile, pairs = tiling
    nl = tile // LANES
    i = pl.program_id(2)
    cq_all = cq_ref[0, 0]

    @pl.when((pl.program_id(0) == 0) & (pl.program_id(1) == 0) & (i == 0))
    def _():
        row = lax.broadcasted_iota(jnp.int32, (BLOCK, BLOCK), 0)
        col = lax.broadcasted_iota(jnp.int32, (BLOCK, BLOCK), 1)
        mask_ref[...] = jnp.where(col <= row, 0.0, NEG)
        r2 = lax.broadcasted_iota(jnp.int32, (2 * tile, LANES), 0)
        l2 = lax.broadcasted_iota(jnp.int32, (2 * tile, LANES), 1)
        ones_ref[...] = jnp.where((r2 < tile) == (l2 < HEAD_DIM), 1.0, 0.0).astype(BF16)

    def lane_max(top, u):
        for b in range(u.shape[1] // LANES):
            top = jnp.maximum(top, u[:, b * LANES:(b + 1) * LANES])
        return top

    def pair_scores(c, pr):
        rows = _chunk_rows(c, tile)
        cols = slice(pr * LANES, (pr + 1) * LANES)
        return lax.dot_general(q_ref[:, cols] * SCALE, _head_split(k_ref[rows, cols]), _NT,
                               preferred_element_type=F32)

    def scores(c, _):
        ck_all = ck_ref[0, c, 0]
        for pr in range(pairs):
            s = pair_scores(c, pr)
            for h in range(2):
                hd = 2 * pr + h
                u = (s[:, h * tile:(h + 1) * tile] - ck_all[hd:hd + 1, :]) * LOG2E
                u_ref[pr, c, :, h * tile:(h + 1) * tile] = u
                top_ref[hd] = lane_max(top_ref[hd], u)
        return 0

    def diagonal_scores():
        ck_all = ck_ref[0, i, 0]
        for pr in range(pairs):
            s = pair_scores(i, pr)
            for h in range(2):
                hd = 2 * pr + h
                for rb in range(nl):
                    r = slice(rb * BLOCK, (rb + 1) * BLOCK)
                    vis = (rb + 1) * BLOCK
                    u = (s[r, h * tile:h * tile + vis] - ck_all[hd:hd + 1, :vis]) * LOG2E
                    last = u[:, rb * BLOCK:] + mask_ref[...]
                    u = jnp.concatenate([u[:, :rb * BLOCK], last], axis=1) if rb else last
                    u_ref[pr, i, r, h * tile:h * tile + vis] = u
                    top_ref[hd, r, :] = lane_max(top_ref[hd, r, :], u)

    top_ref[...] = jnp.full_like(top_ref, NEG)
    lax.fori_loop(0, i, scores, 0)
    diagonal_scores()

    shifts = []
    for hd in range(2 * pairs):
        cq = cq_all[:, hd:hd + 1] * LOG2E
        m = jnp.max(top_ref[hd], axis=-1, keepdims=True) + cq
        shifts.append(jnp.concatenate([jnp.broadcast_to(m - cq, (tile, LANES))] * nl, axis=1))

    def weighted_values(c, pr, ps):
        rows = _chunk_rows(c, tile)
        rhs = jnp.concatenate([_head_split(v_ref[rows, pr * LANES:(pr + 1) * LANES]), ones_ref[...]],
                              axis=1)
        return jnp.dot(jnp.concatenate(ps, axis=1), rhs, preferred_element_type=F32)

    def diagonal_weigh():
        for pr in range(pairs):
            ps = []
            for h in range(2):
                blocks = []
                for rb in range(nl):
                    r = slice(rb * BLOCK, (rb + 1) * BLOCK)
                    vis = (rb + 1) * BLOCK
                    p = jnp.exp2(u_ref[pr, i, r, h * tile:h * tile + vis] - shifts[2 * pr + h][r, :vis])
                    if vis < tile:
                        p = jnp.concatenate([p, jnp.zeros((BLOCK, tile - vis), F32)], axis=1)
                    blocks.append(p.astype(BF16))
                ps.append(jnp.concatenate(blocks, axis=0))
            acc_ref[pr] = weighted_values(i, pr, ps)

    def weigh(c, _):
        for pr in range(pairs):
            ps = [jnp.exp2(u_ref[pr, c, :, h * tile:(h + 1) * tile] - shifts[2 * pr + h]).astype(BF16)
                  for h in range(2)]
            acc_ref[pr] += weighted_values(c, pr, ps)
        return 0

    diagonal_weigh()
    lax.fori_loop(0, i, weigh, 0)
    for pr in range(pairs):
        acc = acc_ref[pr]
        o_ref[:, pr * LANES:(pr + 1) * LANES] = (acc[:, :LANES] / acc[:, LANES:]).astype(o_ref.dtype)


def _fox_attention(proj, cum, batch, seq):
    tile, groups, heads = FOX_TILING.tile, FOX_TILING.groups, FOX_TILING.heads
    nc = seq // tile
    c = cum.reshape(batch, seq, LANES)[:, :, :N_HEADS]
    cq = c.reshape(batch, seq, groups, heads).transpose(0, 2, 1, 3)
    ck = c.reshape(batch, nc, tile, groups, heads).transpose(0, 1, 3, 4, 2)
    specs = [
        pl.BlockSpec((1, 1, tile, heads), lambda b, g, i: (b, g, i, 0)),
        pl.BlockSpec((1, nc, 1, heads, tile), lambda b, g, i: (b, 0, g, 0, 0)),
    ]
    pairs = FOX_TILING.pairs
    scratch = [pltpu.VMEM((pairs, nc, tile, 2 * tile), F32),
               pltpu.VMEM((pairs, tile, 2 * LANES), F32),
               pltpu.VMEM((heads, tile, LANES), F32),
               pltpu.VMEM((BLOCK, BLOCK), F32),
               pltpu.VMEM((2 * tile, LANES), BF16)]
    return _causal_attention_call(_fox_kernel, "fox_attention", FOX_TILING, proj, [cq, ck], specs,
                                  scratch, batch, seq)


def kernel(x, g_pre, g_post, w_in_a, w_out_a, sinks_a, w_in_b, w_out_b, w_in_c, b_f_c, w_out_c):
    batch, seq, d = x.shape
    depth = g_pre.shape[0]
    xf = x.reshape(batch * seq, d)
    wa, wb, wc = w_in_a.astype(BF16), w_in_b.astype(BF16), w_in_c.astype(BF16)
    for i in range(depth):
        kind, j = i % 3, i // 3
        if kind == 0:
            proj = _norm_proj(xf, g_pre[i], wa, j, BRANCH + 2 * KV_A + BRANCH, BF16, PROJ_TN_A)
            o = _swa_attention(proj, sinks_a[j], batch, seq)
            z_col, w_out = BRANCH + 2 * KV_A, w_out_a[j]
        elif kind == 1:
            proj = _norm_proj(xf, g_pre[i], wb, j, 4 * BRANCH, BF16, PROJ_TN)
            o = _stick_attention(proj, batch, seq)
            z_col, w_out = 3 * BRANCH, w_out_b[j]
        else:
            proj = _norm_proj(xf, g_pre[i], wc, j, 4 * BRANCH, BF16, PROJ_TN)
            w_f = jnp.pad(wc[j, :, 4 * BRANCH:], ((0, 0), (0, LANES - N_HEADS)))[None]
            f_logit = _norm_proj(xf, g_pre[i], w_f, 0, LANES, F32, LANES)
            b_f = jnp.pad(b_f_c[j], (0, LANES - N_HEADS)).reshape(1, LANES)
            cum = _cum_log_forget(f_logit, b_f, batch, seq)
            o = _fox_attention(proj, cum, batch, seq)
            z_col, w_out = 3 * BRANCH, w_out_c[j]
        xf = _gate_out(o, proj, z_col, w_out.astype(BF16), g_post[i], xf)
    return xf.reshape(batch, seq, d)
```

```python
import functools
from typing import NamedTuple

import jax
import jax.numpy as jnp
import numpy as np
from jax import lax
from jax.experimental import pallas as pl
from jax.experimental.pallas import tpu as pltpu

HEAD_DIM = 64
N_HEADS = 32
BRANCH = N_HEADS * HEAD_DIM
N_KV_A = 4
KV_A = N_KV_A * HEAD_DIM
GROUP_A = N_HEADS // N_KV_A
BLOCK = 128
LANES = 128
NORM_EPS = 1e-6
NEG = -1e30
SCALE = HEAD_DIM ** -0.5
LOG2E = 1.4426950408889634
N_PAIRS = N_HEADS // 2

F32 = jnp.float32
BF16 = jnp.bfloat16

VMEM_LIMIT = 52 * 1024 * 1024
PROJ_TM = 1024
PROJ_TN_A, PROJ_TN = 1536, 2048
GATE_TM = 512

_NT = (((1,), (1,)), ((), ()))


def _params(sem):
    return pltpu.CompilerParams(dimension_semantics=sem, vmem_limit_bytes=VMEM_LIMIT)


def _rmsnorm(x, g):
    r = lax.rsqrt(jnp.mean(x * x, axis=-1, keepdims=True) + NORM_EPS)
    return x * r * g


def _norm_proj_kernel(x_ref, g_ref, w_ref, o_ref, h_ref):
    @pl.when(pl.program_id(1) == 0)
    def _():
        h_ref[...] = _rmsnorm(x_ref[...], g_ref[...]).astype(BF16)

    o_ref[...] = jnp.dot(h_ref[...], w_ref[...], preferred_element_type=F32).astype(o_ref.dtype)


def _norm_proj(x, g, w, layer, n, out_dtype, tn, tm=PROJ_TM):
    m, d = x.shape
    return pl.pallas_call(
        _norm_proj_kernel,
        grid=(m // tm, n // tn),
        in_specs=[
            pl.BlockSpec((tm, d), lambda i, j: (i, 0)),
            pl.BlockSpec((1, d), lambda i, j: (0, 0)),
            pl.BlockSpec((None, d, tn), lambda i, j: (layer, 0, j)),
        ],
        out_specs=pl.BlockSpec((tm, tn), lambda i, j: (i, j)),
        out_shape=jax.ShapeDtypeStruct((m, n), out_dtype),
        scratch_shapes=[pltpu.VMEM((tm, d), BF16)],
        compiler_params=_params(("parallel", "arbitrary")),
        name="norm_proj",
    )(x, g.reshape(1, d), w)


Z_SPLIT = 4
Z_BLOCK = BRANCH // Z_SPLIT


def _gate_out_kernel(o_ref, *refs):
    z_refs, (w_ref, g_ref, x_ref, out_ref) = refs[:Z_SPLIT], refs[Z_SPLIT:]
    z = jnp.concatenate([r[...] for r in z_refs], axis=1).astype(F32)
    gated = (o_ref[...].astype(F32) * (z * jax.nn.sigmoid(z))).astype(BF16)
    y = jnp.dot(gated, w_ref[...], preferred_element_type=F32)
    out_ref[...] = x_ref[...] + _rmsnorm(y, g_ref[...])


def _gate_out(o, proj, z_col, w_out, g, x, tm=GATE_TM):
    m, d = x.shape
    z0 = z_col // Z_BLOCK
    z_specs = [pl.BlockSpec((tm, Z_BLOCK), lambda i, c=c: (i, z0 + c)) for c in range(Z_SPLIT)]
    return pl.pallas_call(
        _gate_out_kernel,
        grid=(m // tm,),
        in_specs=[pl.BlockSpec((tm, BRANCH), lambda i: (i, 0))] + z_specs + [
            pl.BlockSpec((BRANCH, d), lambda i: (0, 0)),
            pl.BlockSpec((1, d), lambda i: (0, 0)),
            pl.BlockSpec((tm, d), lambda i: (i, 0)),
        ],
        out_specs=pl.BlockSpec((tm, d), lambda i: (i, 0)),
        out_shape=jax.ShapeDtypeStruct((m, d), F32),
        compiler_params=_params(("parallel",)),
        name="gate_out",
    )(o, *([proj] * Z_SPLIT), w_out, g.reshape(1, d), x)


def _stack_pair(q):
    lane = lax.broadcasted_iota(jnp.int32, q.shape, 1)
    zero = jnp.zeros_like(q)
    return jnp.concatenate([jnp.where(lane < HEAD_DIM, q, zero),
                            jnp.where(lane >= HEAD_DIM, q, zero)], axis=0)


def _alibi_slopes():
    n = N_HEADS
    return (2.0 ** (-8.0 * np.arange(1, n + 1, dtype=np.float32) / n)).astype(np.float32)


def _swa_kernel(sink_ref, q_ref, kp_ref, kc_ref, vp_ref, vc_ref, o_ref, bias_ref):
    qi = lax.broadcasted_iota(jnp.int32, (BLOCK, BLOCK), 0)
    kj = lax.broadcasted_iota(jnp.int32, (BLOCK, BLOCK), 1)
    from_prev = kj > qi
    first = (pl.program_id(0) == 0) & (pl.program_id(1) == 0)

    @pl.when(first)
    def _():
        distf = jnp.where(from_prev, qi + BLOCK - kj, qi - kj).astype(F32)
        slopes = _alibi_slopes()
        for h in range(N_HEADS):
            alibi = -float(slopes[h]) * distf
            bias_ref[1, h] = alibi
            bias_ref[0, h] = jnp.where(from_prev, NEG, alibi)

    has_prev = jnp.minimum(pl.program_id(1), 1)

    lane = lax.broadcasted_iota(jnp.int32, (2 * BLOCK, LANES), 1)
    low = lane < HEAD_DIM
    low_q = lax.broadcasted_iota(jnp.int32, (BLOCK, LANES), 1) < HEAD_DIM
    for g in range(N_KV_A):
        p, half = divmod(g, 2)
        cols = slice(p * LANES, (p + 1) * LANES)
        k2 = jnp.concatenate([kp_ref[:, cols], kc_ref[:, cols]], axis=0).astype(F32)
        v2 = jnp.concatenate([vp_ref[:, cols], vc_ref[:, cols]], axis=0).astype(F32)
        k2r = pltpu.roll(k2, HEAD_DIM, 1)
        v2r = pltpu.roll(v2, HEAD_DIM, 1)
        own_low = low if half == 0 else jnp.logical_not(low)
        kd = jnp.where(own_low, k2, k2r).astype(BF16)
        vd = jnp.where(own_low, v2, v2r).astype(BF16)

        q = q_ref[:, g * GROUP_A * HEAD_DIM:(g + 1) * GROUP_A * HEAD_DIM] * SCALE
        rows = [_stack_pair(q[:, i * LANES:(i + 1) * LANES]) for i in range(GROUP_A // 2)]
        lhs = jnp.concatenate(rows, axis=0)
        s = lax.dot_general(lhs, kd, _NT, preferred_element_type=F32)
        ps, ls = [], []
        for hh in range(GROUP_A):
            h = g * GROUP_A + hh
            sb = s[hh * BLOCK:(hh + 1) * BLOCK]
            sh = jnp.where(from_prev, sb[:, :BLOCK], sb[:, BLOCK:]) + bias_ref[has_prev, h]
            sink = sink_ref[h]
            m = jnp.maximum(jnp.max(sh, axis=-1, keepdims=True), sink)
            e = jnp.exp(sh - m)
            ls.append(jnp.sum(e, axis=-1, keepdims=True) + jnp.exp(sink - m))
            ps.append(jnp.concatenate([jnp.where(from_prev, e, 0.0), jnp.where(from_prev, 0.0, e)],
                                      axis=1).astype(BF16))
        pv = jnp.dot(jnp.concatenate(ps, axis=0), vd, preferred_element_type=F32)
        for i in range(GROUP_A // 2):
            o0 = pv[(2 * i) * BLOCK:(2 * i + 1) * BLOCK] / ls[2 * i]
            o1 = pv[(2 * i + 1) * BLOCK:(2 * i + 2) * BLOCK] / ls[2 * i + 1]
            c0 = (g * GROUP_A // 2 + i) * LANES
            o_ref[:, c0:c0 + LANES] = jnp.where(low_q, o0, o1).astype(o_ref.dtype)


def _swa_attention(proj, sinks, batch, seq):
    m = proj.shape[0]
    nb = seq // BLOCK
    kcol = BRANCH // KV_A
    vcol = kcol + 1

    def cur(b, n):
        return b * nb + n

    def prev(b, n):
        return b * nb + jnp.maximum(n - 1, 0)

    return pl.pallas_call(
        _swa_kernel,
        grid=(batch, nb),
        in_specs=[
            pl.BlockSpec(memory_space=pltpu.SMEM),
            pl.BlockSpec((BLOCK, BRANCH), lambda b, n: (cur(b, n), 0)),
            pl.BlockSpec((BLOCK, KV_A), lambda b, n: (prev(b, n), kcol)),
            pl.BlockSpec((BLOCK, KV_A), lambda b, n: (cur(b, n), kcol)),
            pl.BlockSpec((BLOCK, KV_A), lambda b, n: (prev(b, n), vcol)),
            pl.BlockSpec((BLOCK, KV_A), lambda b, n: (cur(b, n), vcol)),
        ],
        out_specs=pl.BlockSpec((BLOCK, BRANCH), lambda b, n: (cur(b, n), 0)),
        out_shape=jax.ShapeDtypeStruct((m, BRANCH), BF16),
        scratch_shapes=[pltpu.VMEM((2, N_HEADS, BLOCK, BLOCK), F32)],
        compiler_params=_params(("arbitrary", "arbitrary")),
        name="swa_attention",
    )(sinks, proj, proj, proj, proj, proj)


class _Tiling(NamedTuple):
    tile: int
    pairs: int

    @property
    def groups(self):
        return N_PAIRS // self.pairs

    @property
    def heads(self):
        return 2 * self.pairs


STICK_TILING = _Tiling(tile=256, pairs=8)
FOX_TILING = _Tiling(tile=512, pairs=2)


def _head_split(x):
    lane = lax.broadcasted_iota(jnp.int32, x.shape, 1)
    zero = jnp.zeros_like(x)
    return jnp.concatenate([jnp.where(lane < HEAD_DIM, x, zero),
                            jnp.where(lane >= HEAD_DIM, x, zero)], axis=0)


def _chunk_rows(c, tile):
    if isinstance(c, int):
        return pl.ds(c * tile, tile)
    return pl.ds(pl.multiple_of(c * tile, tile), tile)


def _causal_attention_call(kernel_fn, name, tiling, proj, extra_inputs, extra_specs, scratch,
                           batch, seq):
    m = proj.shape[0]
    tile, groups = tiling.tile, tiling.groups
    nq = seq // tile
    width = tiling.pairs * LANES
    return pl.pallas_call(
        functools.partial(kernel_fn, tiling=tiling),
        grid=(batch, groups, nq),
        in_specs=[
            pl.BlockSpec((tile, width), lambda b, g, i: (b * nq + i, g)),
            pl.BlockSpec((seq, width), lambda b, g, i: (b, groups + g)),
            pl.BlockSpec((seq, width), lambda b, g, i: (b, 2 * groups + g)),
        ] + extra_specs,
        out_specs=pl.BlockSpec((tile, width), lambda b, g, i: (b * nq + i, g)),
        out_shape=jax.ShapeDtypeStruct((m, BRANCH), BF16),
        scratch_shapes=scratch,
        compiler_params=_params(("arbitrary", "arbitrary", "arbitrary")),
        name=name,
    )(proj, proj, proj, *extra_inputs)


def _suffix_matrix():
    j = lax.broadcasted_iota(jnp.int32, (BLOCK, 2 * BLOCK), 0)
    s = lax.broadcasted_iota(jnp.int32, (BLOCK, 2 * BLOCK), 1)
    return jnp.where((s >= BLOCK) | (j > s), 1.0, 0.0).astype(BF16)


SOFTPLUS_CLAMP = 60.0
EXP_UNDERFLOW = 105.0


def _softplus(s):
    return jnp.maximum(s, jnp.log(1.0 + jnp.exp(jnp.minimum(s, SOFTPLUS_CLAMP))))


def _stick_kernel(q_ref, k_ref, v_ref, o_ref, acc_ref, carry_ref, *, tiling):
    tile, pairs = tiling
    nkb = tile // BLOCK
    i = pl.program_id(2)
    u = _suffix_matrix()

    row = lax.broadcasted_iota(jnp.int32, (tile, tile), 0)
    col = lax.broadcasted_iota(jnp.int32, (tile, tile), 1)
    before = jnp.concatenate([col < row, col < row], axis=1)

    def chunk(c, mask):
        rows = _chunk_rows(c, tile)
        for pr in range(pairs):
            cols = slice(pr * LANES, (pr + 1) * LANES)
            q2 = q_ref[:, cols] * SCALE
            s = lax.dot_general(q2, _head_split(k_ref[rows, cols]), _NT,
                                preferred_element_type=F32)
            sp = _softplus(s)
            if mask is not None:
                sp = jnp.where(mask, sp, 0.0)
            spb = sp.astype(BF16)
            base = s - sp
            ps = [None] * (2 * nkb)
            for h in range(2):
                carry = carry_ref[pr, h]
                for kb in reversed(range(nkb)):
                    sl = slice(h * tile + kb * BLOCK, h * tile + (kb + 1) * BLOCK)
                    cs = jnp.dot(spb[:, sl], u, preferred_element_type=F32)
                    ps[h * nkb + kb] = jnp.exp(base[:, sl] - cs[:, :BLOCK] - carry)
                    carry = carry + cs[:, BLOCK:]
                carry_ref[pr, h] = carry
            p = jnp.concatenate(ps, axis=1)
            if mask is not None:
                p = jnp.where(mask, p, 0.0)
            acc_ref[pr] += jnp.dot(p.astype(BF16), _head_split(v_ref[rows, cols]),
                                   preferred_element_type=F32)

    acc_ref[...] = jnp.zeros_like(acc_ref)
    carry_ref[...] = jnp.zeros_like(carry_ref)
    chunk(i, before)

    def weights_vanish():
        return (jnp.min(carry_ref[...]) >= EXP_UNDERFLOW).astype(jnp.int32)

    def cond(st):
        t, done = st
        return jnp.logical_and(t < i, done == 0)

    def body(st):
        t, _ = st
        chunk(i - 1 - t, None)
        return t + 1, weights_vanish()

    lax.while_loop(cond, body, (jnp.int32(0), weights_vanish()))
    for pr in range(pairs):
        o_ref[:, pr * LANES:(pr + 1) * LANES] = acc_ref[pr].astype(o_ref.dtype)


def _stick_attention(proj, batch, seq):
    tile, pairs = STICK_TILING
    scratch = [pltpu.VMEM((pairs, tile, LANES), F32),
               pltpu.VMEM((pairs, 2, tile, LANES), F32)]
    return _causal_attention_call(_stick_kernel, "stick_attention", STICK_TILING, proj, [], [],
                                  scratch, batch, seq)


def _split3(x):
    hi = x.astype(BF16)
    r = x - hi.astype(F32)
    mid = r.astype(BF16)
    lo = (r - mid.astype(F32)).astype(BF16)
    return hi, mid, lo


def _cum_kernel(fl_ref, bf_ref, cum_ref):
    x = fl_ref[...] + bf_ref[...]
    lf = jnp.minimum(x, 0.0) - jnp.log(1.0 + jnp.exp(-jnp.abs(x)))
    t = lax.broadcasted_iota(jnp.int32, (BLOCK, BLOCK), 0)
    j = lax.broadcasted_iota(jnp.int32, (BLOCK, BLOCK), 1)
    tri = jnp.where(j <= t, 1.0, 0.0).astype(BF16)
    parts = jnp.concatenate(_split3(lf), axis=1)
    total = jnp.zeros((1, LANES), F32)
    for blk in range(x.shape[0] // BLOCK):
        rows = slice(blk * BLOCK, (blk + 1) * BLOCK)
        c = jnp.dot(tri, parts[rows], preferred_element_type=F32)
        c = c[:, :LANES] + c[:, LANES:2 * LANES] + c[:, 2 * LANES:] + total
        cum_ref[rows, :] = c
        total = c[BLOCK - 1:BLOCK, :]


def _cum_log_forget(f_logit, b_f, batch, seq):
    return pl.pallas_call(
        _cum_kernel,
        grid=(batch,),
        in_specs=[pl.BlockSpec((seq, LANES), lambda b: (b, 0)),
                  pl.BlockSpec((1, LANES), lambda b: (0, 0))],
        out_specs=pl.BlockSpec((seq, LANES), lambda b: (b, 0)),
        out_shape=jax.ShapeDtypeStruct((batch * seq, LANES), F32),
        compiler_params=_params(("parallel",)),
        name="cum_log_forget",
    )(f_logit, b_f)


def _fox_kernel(q_ref, k_ref, v_ref, cq_ref, ck_ref, o_ref, u_ref, acc_ref, top_ref, mask_ref,
                ones_ref, *, tiling):
    tile, pairs = tiling
    nl = tile // LANES
    i = pl.program_id(2)
    cq_all = cq_ref[0, 0]

    @pl.when((pl.program_id(0) == 0) & (pl.program_id(1) == 0) & (i == 0))
    def _():
        row = lax.broadcasted_iota(jnp.int32, (BLOCK, BLOCK), 0)
        col = lax.broadcasted_iota(jnp.int32, (BLOCK, BLOCK), 1)
        mask_ref[...] = jnp.where(col <= row, 0.0, NEG)
        r2 = lax.broadcasted_iota(jnp.int32, (2 * tile, LANES), 0)
        l2 = lax.broadcasted_iota(jnp.int32, (2 * tile, LANES), 1)
        ones_ref[...] = jnp.where((r2 < tile) == (l2 < HEAD_DIM), 1.0, 0.0).astype(BF16)

    def lane_max(top, u):
        for b in range(u.shape[1] // LANES):
            top = jnp.maximum(top, u[:, b * LANES:(b + 1) * LANES])
        return top

    def pair_scores(c, pr):
        rows = _chunk_rows(c, tile)
        cols = slice(pr * LANES, (pr + 1) * LANES)
        return lax.dot_general(q_ref[:, cols] * SCALE, _head_split(k_ref[rows, cols]), _NT,
                               preferred_element_type=F32)

    def scores(c, _):
        ck_all = ck_ref[0, c, 0]
        for pr in range(pairs):
            s = pair_scores(c, pr)
            for h in range(2):
                hd = 2 * pr + h
                u = (s[:, h * tile:(h + 1) * tile] - ck_all[hd:hd + 1, :]) * LOG2E
                u_ref[pr, c, :, h * tile:(h + 1) * tile] = u
                top_ref[hd] = lane_max(top_ref[hd], u)
        return 0

    def diagonal_scores(d):
        ck_all = ck_ref[0, d, 0]
        for pr in range(pairs):
            s = pair_scores(d, pr)
            for h in range(2):
                hd = 2 * pr + h
                for rb in range(nl):
                    r = slice(rb * BLOCK, (rb + 1) * BLOCK)
                    vis = (rb + 1) * BLOCK
                    u = (s[r, h * tile:h * tile + vis] - ck_all[hd:hd + 1, :vis]) * LOG2E
                    last = u[:, rb * BLOCK:] + mask_ref[...]
                    u = jnp.concatenate([u[:, :rb * BLOCK], last], axis=1) if rb else last
                    u_ref[pr, d, r, h * tile:h * tile + vis] = u
                    top_ref[hd, r, :] = lane_max(top_ref[hd, r, :], u)

    def weighted_values(c, pr, ps):
        rows = _chunk_rows(c, tile)
        rhs = jnp.concatenate([_head_split(v_ref[rows, pr * LANES:(pr + 1) * LANES]), ones_ref[...]],
                              axis=1)
        return jnp.dot(jnp.concatenate(ps, axis=1), rhs, preferred_element_type=F32)

    def diagonal_weigh(d, shifts):
        for pr in range(pairs):
            ps = []
            for h in range(2):
                blocks = []
                for rb in range(nl):
                    r = slice(rb * BLOCK, (rb + 1) * BLOCK)
                    vis = (rb + 1) * BLOCK
                    p = jnp.exp2(u_ref[pr, d, r, h * tile:h * tile + vis] - shifts[2 * pr + h][r, :vis])
                    if vis < tile:
                        p = jnp.concatenate([p, jnp.zeros((BLOCK, tile - vis), F32)], axis=1)
                    blocks.append(p.astype(BF16))
                ps.append(jnp.concatenate(blocks, axis=0))
            acc_ref[pr] = weighted_values(d, pr, ps)

    def weigh(c, shifts):
        for pr in range(pairs):
            ps = [jnp.exp2(u_ref[pr, c, :, h * tile:(h + 1) * tile] - shifts[2 * pr + h]).astype(BF16)
                  for h in range(2)]
            acc_ref[pr] += weighted_values(c, pr, ps)

    def query_tile(d):
        top_ref[...] = jnp.full_like(top_ref, NEG)
        for c in range(d):
            scores(c, 0)
        diagonal_scores(d)
        shifts = []
        for hd in range(2 * pairs):
            cq = cq_all[:, hd:hd + 1] * LOG2E
            m = jnp.max(top_ref[hd], axis=-1, keepdims=True) + cq
            shifts.append(jnp.concatenate([jnp.broadcast_to(m - cq, (tile, LANES))] * nl, axis=1))
        diagonal_weigh(d, shifts)
        for c in range(d):
            weigh(c, shifts)
        for pr in range(pairs):
            acc = acc_ref[pr]
            o_ref[:, pr * LANES:(pr + 1) * LANES] = (
                acc[:, :LANES] / acc[:, LANES:]).astype(o_ref.dtype)

    for d in range(u_ref.shape[1]):
        pl.when(i == d)(functools.partial(query_tile, d))


def _fox_attention(proj, cum, batch, seq):
    tile, groups, heads = FOX_TILING.tile, FOX_TILING.groups, FOX_TILING.heads
    nc = seq // tile
    c = cum.reshape(batch, seq, LANES)[:, :, :N_HEADS]
    cq = c.reshape(batch, seq, groups, heads).transpose(0, 2, 1, 3)
    ck = c.reshape(batch, nc, tile, groups, heads).transpose(0, 1, 3, 4, 2)
    specs = [
        pl.BlockSpec((1, 1, tile, heads), lambda b, g, i: (b, g, i, 0)),
        pl.BlockSpec((1, nc, 1, heads, tile), lambda b, g, i: (b, 0, g, 0, 0)),
    ]
    pairs = FOX_TILING.pairs
    scratch = [pltpu.VMEM((pairs, nc, tile, 2 * tile), F32),
               pltpu.VMEM((pairs, tile, 2 * LANES), F32),
               pltpu.VMEM((heads, tile, LANES), F32),
               pltpu.VMEM((BLOCK, BLOCK), F32),
               pltpu.VMEM((2 * tile, LANES), BF16)]
    return _causal_attention_call(_fox_kernel, "fox_attention", FOX_TILING, proj, [cq, ck], specs,
                                  scratch, batch, seq)


def kernel(x, g_pre, g_post, w_in_a, w_out_a, sinks_a, w_in_b, w_out_b, w_in_c, b_f_c, w_out_c):
    batch, seq, d = x.shape
    depth = g_pre.shape[0]
    xf = x.reshape(batch * seq, d)
    wa, wb, wc = w_in_a.astype(BF16), w_in_b.astype(BF16), w_in_c.astype(BF16)
    for i in range(depth):
        kind, j = i % 3, i // 3
        if kind == 0:
            proj = _norm_proj(xf, g_pre[i], wa, j, BRANCH + 2 * KV_A + BRANCH, BF16, PROJ_TN_A)
            o = _swa_attention(proj, sinks_a[j], batch, seq)
            z_col, w_out = BRANCH + 2 * KV_A, w_out_a[j]
        elif kind == 1:
            proj = _norm_proj(xf, g_pre[i], wb, j, 4 * BRANCH, BF16, PROJ_TN)
            o = _stick_attention(proj, batch, seq)
            z_col, w_out = 3 * BRANCH, w_out_b[j]
        else:
            proj = _norm_proj(xf, g_pre[i], wc, j, 4 * BRANCH, BF16, PROJ_TN)
            w_f = jnp.pad(wc[j, :, 4 * BRANCH:], ((0, 0), (0, LANES - N_HEADS)))[None]
            f_logit = _norm_proj(xf, g_pre[i], w_f, 0, LANES, F32, LANES)
            b_f = jnp.pad(b_f_c[j], (0, LANES - N_HEADS)).reshape(1, LANES)
            cum = _cum_log_forget(f_logit, b_f, batch, seq)
            o = _fox_attention(proj, cum, batch, seq)
            z_col, w_out = 3 * BRANCH, w_out_c[j]
        xf = _gate_out(o, proj, z_col, w_out.astype(BF16), g_post[i], xf)
    return xf.reshape(batch, seq, d)
```

```python
import functools
from typing import NamedTuple

import jax
import jax.numpy as jnp
import numpy as np
from jax import lax
from jax.experimental import pallas as pl
from jax.experimental.pallas import tpu as pltpu

HEAD_DIM = 64
N_HEADS = 32
BRANCH = N_HEADS * HEAD_DIM
N_KV_A = 4
KV_A = N_KV_A * HEAD_DIM
GROUP_A = N_HEADS // N_KV_A
BLOCK = 128
LANES = 128
NORM_EPS = 1e-6
NEG = -1e30
SCALE = HEAD_DIM ** -0.5
LOG2E = 1.4426950408889634
N_PAIRS = N_HEADS // 2

F32 = jnp.float32
BF16 = jnp.bfloat16

VMEM_LIMIT = 52 * 1024 * 1024
PROJ_TM = 1024
PROJ_TN_A, PROJ_TN = 1536, 2048
GATE_TM = 512

_NT = (((1,), (1,)), ((), ()))


def _params(sem):
    return pltpu.CompilerParams(dimension_semantics=sem, vmem_limit_bytes=VMEM_LIMIT)


def _rmsnorm(x, g):
    r = lax.rsqrt(jnp.mean(x * x, axis=-1, keepdims=True) + NORM_EPS)
    return x * r * g


def _norm_proj_kernel(x_ref, g_ref, w_ref, o_ref, h_ref):
    @pl.when(pl.program_id(1) == 0)
    def _():
        h_ref[...] = _rmsnorm(x_ref[...], g_ref[...]).astype(BF16)

    o_ref[...] = jnp.dot(h_ref[...], w_ref[...], preferred_element_type=F32).astype(o_ref.dtype)


def _norm_proj(x, g, w, layer, n, out_dtype, tn, tm=PROJ_TM):
    m, d = x.shape
    return pl.pallas_call(
        _norm_proj_kernel,
        grid=(m // tm, n // tn),
        in_specs=[
            pl.BlockSpec((tm, d), lambda i, j: (i, 0)),
            pl.BlockSpec((1, d), lambda i, j: (0, 0)),
            pl.BlockSpec((None, d, tn), lambda i, j: (layer, 0, j)),
        ],
        out_specs=pl.BlockSpec((tm, tn), lambda i, j: (i, j)),
        out_shape=jax.ShapeDtypeStruct((m, n), out_dtype),
        scratch_shapes=[pltpu.VMEM((tm, d), BF16)],
        compiler_params=_params(("parallel", "arbitrary")),
        name="norm_proj",
    )(x, g.reshape(1, d), w)


Z_SPLIT = 4
Z_BLOCK = BRANCH // Z_SPLIT


def _gate_out_kernel(o_ref, *refs):
    z_refs, (w_ref, g_ref, x_ref, out_ref) = refs[:Z_SPLIT], refs[Z_SPLIT:]
    z = jnp.concatenate([r[...] for r in z_refs], axis=1).astype(F32)
    gated = (o_ref[...].astype(F32) * (z * jax.nn.sigmoid(z))).astype(BF16)
    y = jnp.dot(gated, w_ref[...], preferred_element_type=F32)
    out_ref[...] = x_ref[...] + _rmsnorm(y, g_ref[...])


def _gate_out(o, proj, z_col, w_out, g, x, tm=GATE_TM):
    m, d = x.shape
    z0 = z_col // Z_BLOCK
    z_specs = [pl.BlockSpec((tm, Z_BLOCK), lambda i, c=c: (i, z0 + c)) for c in range(Z_SPLIT)]
    return pl.pallas_call(
        _gate_out_kernel,
        grid=(m // tm,),
        in_specs=[pl.BlockSpec((tm, BRANCH), lambda i: (i, 0))] + z_specs + [
            pl.BlockSpec((BRANCH, d), lambda i: (0, 0)),
            pl.BlockSpec((1, d), lambda i: (0, 0)),
            pl.BlockSpec((tm, d), lambda i: (i, 0)),
        ],
        out_specs=pl.BlockSpec((tm, d), lambda i: (i, 0)),
        out_shape=jax.ShapeDtypeStruct((m, d), F32),
        compiler_params=_params(("parallel",)),
        name="gate_out",
    )(o, *([proj] * Z_SPLIT), w_out, g.reshape(1, d), x)


def _stack_pair(q):
    lane = lax.broadcasted_iota(jnp.int32, q.shape, 1)
    zero = jnp.zeros_like(q)
    return jnp.concatenate([jnp.where(lane < HEAD_DIM, q, zero),
                            jnp.where(lane >= HEAD_DIM, q, zero)], axis=0)


def _alibi_slopes():
    n = N_HEADS
    return (2.0 ** (-8.0 * np.arange(1, n + 1, dtype=np.float32) / n)).astype(np.float32)


def _swa_kernel(sink_ref, q_ref, kp_ref, kc_ref, vp_ref, vc_ref, o_ref, bias_ref):
    qi = lax.broadcasted_iota(jnp.int32, (BLOCK, BLOCK), 0)
    kj = lax.broadcasted_iota(jnp.int32, (BLOCK, BLOCK), 1)
    from_prev = kj > qi
    first = (pl.program_id(0) == 0) & (pl.program_id(1) == 0)

    @pl.when(first)
    def _():
        distf = jnp.where(from_prev, qi + BLOCK - kj, qi - kj).astype(F32)
        slopes = _alibi_slopes()
        for h in range(N_HEADS):
            alibi = -float(slopes[h]) * distf
            bias_ref[1, h] = alibi
            bias_ref[0, h] = jnp.where(from_prev, NEG, alibi)

    has_prev = jnp.minimum(pl.program_id(1), 1)

    lane = lax.broadcasted_iota(jnp.int32, (2 * BLOCK, LANES), 1)
    low = lane < HEAD_DIM
    low_q = lax.broadcasted_iota(jnp.int32, (BLOCK, LANES), 1) < HEAD_DIM
    for g in range(N_KV_A):
        p, half = divmod(g, 2)
        cols = slice(p * LANES, (p + 1) * LANES)
        k2 = jnp.concatenate([kp_ref[:, cols], kc_ref[:, cols]], axis=0).astype(F32)
        v2 = jnp.concatenate([vp_ref[:, cols], vc_ref[:, cols]], axis=0).astype(F32)
        k2r = pltpu.roll(k2, HEAD_DIM, 1)
        v2r = pltpu.roll(v2, HEAD_DIM, 1)
        own_low = low if half == 0 else jnp.logical_not(low)
        kd = jnp.where(own_low, k2, k2r).astype(BF16)
        vd = jnp.where(own_low, v2, v2r).astype(BF16)

        q = q_ref[:, g * GROUP_A * HEAD_DIM:(g + 1) * GROUP_A * HEAD_DIM] * SCALE
        rows = [_stack_pair(q[:, i * LANES:(i + 1) * LANES]) for i in range(GROUP_A // 2)]
        lhs = jnp.concatenate(rows, axis=0)
        s = lax.dot_general(lhs, kd, _NT, preferred_element_type=F32)
        ps, ls = [], []
        for hh in range(GROUP_A):
            h = g * GROUP_A + hh
            sb = s[hh * BLOCK:(hh + 1) * BLOCK]
            sh = jnp.where(from_prev, sb[:, :BLOCK], sb[:, BLOCK:]) + bias_ref[has_prev, h]
            sink = sink_ref[h]
            m = jnp.maximum(jnp.max(sh, axis=-1, keepdims=True), sink)
            e = jnp.exp(sh - m)
            ls.append(jnp.sum(e, axis=-1, keepdims=True) + jnp.exp(sink - m))
            ps.append(jnp.concatenate([jnp.where(from_prev, e, 0.0), jnp.where(from_prev, 0.0, e)],
                                      axis=1).astype(BF16))
        pv = jnp.dot(jnp.concatenate(ps, axis=0), vd, preferred_element_type=F32)
        for i in range(GROUP_A // 2):
            o0 = pv[(2 * i) * BLOCK:(2 * i + 1) * BLOCK] / ls[2 * i]
            o1 = pv[(2 * i + 1) * BLOCK:(2 * i + 2) * BLOCK] / ls[2 * i + 1]
            c0 = (g * GROUP_A // 2 + i) * LANES
            o_ref[:, c0:c0 + LANES] = jnp.where(low_q, o0, o1).astype(o_ref.dtype)


def _swa_attention(proj, sinks, batch, seq):
    m = proj.shape[0]
    nb = seq // BLOCK
    kcol = BRANCH // KV_A
    vcol = kcol + 1

    def cur(b, n):
        return b * nb + n

    def prev(b, n):
        return b * nb + jnp.maximum(n - 1, 0)

    return pl.pallas_call(
        _swa_kernel,
        grid=(batch, nb),
        in_specs=[
            pl.BlockSpec(memory_space=pltpu.SMEM),
            pl.BlockSpec((BLOCK, BRANCH), lambda b, n: (cur(b, n), 0)),
            pl.BlockSpec((BLOCK, KV_A), lambda b, n: (prev(b, n), kcol)),
            pl.BlockSpec((BLOCK, KV_A), lambda b, n: (cur(b, n), kcol)),
            pl.BlockSpec((BLOCK, KV_A), lambda b, n: (prev(b, n), vcol)),
            pl.BlockSpec((BLOCK, KV_A), lambda b, n: (cur(b, n), vcol)),
        ],
        out_specs=pl.BlockSpec((BLOCK, BRANCH), lambda b, n: (cur(b, n), 0)),
        out_shape=jax.ShapeDtypeStruct((m, BRANCH), BF16),
        scratch_shapes=[pltpu.VMEM((2, N_HEADS, BLOCK, BLOCK), F32)],
        compiler_params=_params(("arbitrary", "arbitrary")),
        name="swa_attention",
    )(sinks, proj, proj, proj, proj, proj)


class _Tiling(NamedTuple):
    tile: int
    pairs: int

    @property
    def groups(self):
        return N_PAIRS // self.pairs

    @property
    def heads(self):
        return 2 * self.pairs


STICK_TILING = _Tiling(tile=256, pairs=8)
FOX_TILING = _Tiling(tile=512, pairs=2)


def _head_split(x):
    lane = lax.broadcasted_iota(jnp.int32, x.shape, 1)
    zero = jnp.zeros_like(x)
    return jnp.concatenate([jnp.where(lane < HEAD_DIM, x, zero),
                            jnp.where(lane >= HEAD_DIM, x, zero)], axis=0)


def _chunk_rows(c, tile):
    if isinstance(c, int):
        return pl.ds(c * tile, tile)
    return pl.ds(pl.multiple_of(c * tile, tile), tile)


def _causal_attention_call(kernel_fn, name, tiling, proj, extra_inputs, extra_specs, scratch,
                           batch, seq):
    m = proj.shape[0]
    tile, groups = tiling.tile, tiling.groups
    nq = seq // tile
    width = tiling.pairs * LANES
    return pl.pallas_call(
        functools.partial(kernel_fn, tiling=tiling),
        grid=(batch, groups, nq),
        in_specs=[
            pl.BlockSpec((tile, width), lambda b, g, i: (b * nq + i, g)),
            pl.BlockSpec((seq, width), lambda b, g, i: (b, groups + g)),
            pl.BlockSpec((seq, width), lambda b, g, i: (b, 2 * groups + g)),
        ] + extra_specs,
        out_specs=pl.BlockSpec((tile, width), lambda b, g, i: (b * nq + i, g)),
        out_shape=jax.ShapeDtypeStruct((m, BRANCH), BF16),
        scratch_shapes=scratch,
        compiler_params=_params(("arbitrary", "arbitrary", "arbitrary")),
        name=name,
    )(proj, proj, proj, *extra_inputs)


def _suffix_matrix():
    j = lax.broadcasted_iota(jnp.int32, (BLOCK, 2 * BLOCK), 0)
    s = lax.broadcasted_iota(jnp.int32, (BLOCK, 2 * BLOCK), 1)
    return jnp.where((s >= BLOCK) | (j > s), 1.0, 0.0).astype(BF16)


SOFTPLUS_CLAMP = 60.0
EXP_UNDERFLOW = 105.0


def _softplus(s):
    return jnp.maximum(s, jnp.log(1.0 + jnp.exp(jnp.minimum(s, SOFTPLUS_CLAMP))))


def _stick_kernel(q_ref, k_ref, v_ref, o_ref, acc_ref, carry_ref, *, tiling):
    tile, pairs = tiling
    nkb = tile // BLOCK
    i = pl.program_id(2)
    u = _suffix_matrix()

    row = lax.broadcasted_iota(jnp.int32, (tile, tile), 0)
    col = lax.broadcasted_iota(jnp.int32, (tile, tile), 1)
    before = jnp.concatenate([col < row, col < row], axis=1)

    def chunk(c, mask):
        rows = _chunk_rows(c, tile)
        for pr in range(pairs):
            cols = slice(pr * LANES, (pr + 1) * LANES)
            q2 = q_ref[:, cols] * SCALE
            s = lax.dot_general(q2, _head_split(k_ref[rows, cols]), _NT,
                                preferred_element_type=F32)
            sp = _softplus(s)
            if mask is not None:
                sp = jnp.where(mask, sp, 0.0)
            spb = sp.astype(BF16)
            base = s - sp
            ps = [None] * (2 * nkb)
            for h in range(2):
                carry = carry_ref[pr, h]
                for kb in reversed(range(nkb)):
                    sl = slice(h * tile + kb * BLOCK, h * tile + (kb + 1) * BLOCK)
                    cs = jnp.dot(spb[:, sl], u, preferred_element_type=F32)
                    ps[h * nkb + kb] = jnp.exp(base[:, sl] - cs[:, :BLOCK] - carry)
                    carry = carry + cs[:, BLOCK:]
                carry_ref[pr, h] = carry
            p = jnp.concatenate(ps, axis=1)
            if mask is not None:
                p = jnp.where(mask, p, 0.0)
            acc_ref[pr] += jnp.dot(p.astype(BF16), _head_split(v_ref[rows, cols]),
                                   preferred_element_type=F32)

    acc_ref[...] = jnp.zeros_like(acc_ref)
    carry_ref[...] = jnp.zeros_like(carry_ref)

    @pl.when(i == 0)
    def _():
        chunk(i, before)

    @pl.when(i > 0)
    def _():
        chunk(i, before)
        chunk(i - 1, None)

    def weights_vanish():
        return (jnp.min(carry_ref[...]) >= EXP_UNDERFLOW).astype(jnp.int32)

    def cond(st):
        t, done = st
        return jnp.logical_and(t < i, done == 0)

    def body(st):
        t, _ = st
        chunk(i - 1 - t, None)
        return t + 1, weights_vanish()

    lax.while_loop(cond, body, (jnp.int32(1), weights_vanish()))
    for pr in range(pairs):
        o_ref[:, pr * LANES:(pr + 1) * LANES] = acc_ref[pr].astype(o_ref.dtype)


def _stick_attention(proj, batch, seq):
    tile, pairs = STICK_TILING
    scratch = [pltpu.VMEM((pairs, tile, LANES), F32),
               pltpu.VMEM((pairs, 2, tile, LANES), F32)]
    return _causal_attention_call(_stick_kernel, "stick_attention", STICK_TILING, proj, [], [],
                                  scratch, batch, seq)


def _split3(x):
    hi = x.astype(BF16)
    r = x - hi.astype(F32)
    mid = r.astype(BF16)
    lo = (r - mid.astype(F32)).astype(BF16)
    return hi, mid, lo


def _cum_kernel(fl_ref, bf_ref, cum_ref):
    x = fl_ref[...] + bf_ref[...]
    lf = jnp.minimum(x, 0.0) - jnp.log(1.0 + jnp.exp(-jnp.abs(x)))
    t = lax.broadcasted_iota(jnp.int32, (BLOCK, BLOCK), 0)
    j = lax.broadcasted_iota(jnp.int32, (BLOCK, BLOCK), 1)
    tri = jnp.where(j <= t, 1.0, 0.0).astype(BF16)
    parts = jnp.concatenate(_split3(lf), axis=1)
    total = jnp.zeros((1, LANES), F32)
    for blk in range(x.shape[0] // BLOCK):
        rows = slice(blk * BLOCK, (blk + 1) * BLOCK)
        c = jnp.dot(tri, parts[rows], preferred_element_type=F32)
        c = c[:, :LANES] + c[:, LANES:2 * LANES] + c[:, 2 * LANES:] + total
        cum_ref[rows, :] = c
        total = c[BLOCK - 1:BLOCK, :]


def _cum_log_forget(f_logit, b_f, batch, seq):
    return pl.pallas_call(
        _cum_kernel,
        grid=(batch,),
        in_specs=[pl.BlockSpec((seq, LANES), lambda b: (b, 0)),
                  pl.BlockSpec((1, LANES), lambda b: (0, 0))],
        out_specs=pl.BlockSpec((seq, LANES), lambda b: (b, 0)),
        out_shape=jax.ShapeDtypeStruct((batch * seq, LANES), F32),
        compiler_params=_params(("parallel",)),
        name="cum_log_forget",
    )(f_logit, b_f)


def _fox_kernel(q_ref, k_ref, v_ref, cq_ref, ck_ref, o_ref, u_ref, acc_ref, top_ref, mask_ref,
                ones_ref, *, tiling):
    tile, pairs = tiling
    nl = tile // LANES
    i = pl.program_id(2)
    cq_all = cq_ref[0, 0]

    @pl.when((pl.program_id(0) == 0) & (pl.program_id(1) == 0) & (i == 0))
    def _():
        row = lax.broadcasted_iota(jnp.int32, (BLOCK, BLOCK), 0)
        col = lax.broadcasted_iota(jnp.int32, (BLOCK, BLOCK), 1)
        mask_ref[...] = jnp.where(col <= row, 0.0, NEG)
        r2 = lax.broadcasted_iota(jnp.int32, (2 * tile, LANES), 0)
        l2 = lax.broadcasted_iota(jnp.int32, (2 * tile, LANES), 1)
        ones_ref[...] = jnp.where((r2 < tile) == (l2 < HEAD_DIM), 1.0, 0.0).astype(BF16)

    def lane_max(top, u):
        for b in range(u.shape[1] // LANES):
            top = jnp.maximum(top, u[:, b * LANES:(b + 1) * LANES])
        return top

    def pair_scores(c, pr):
        rows = _chunk_rows(c, tile)
        cols = slice(pr * LANES, (pr + 1) * LANES)
        return lax.dot_general(q_ref[:, cols] * SCALE, _head_split(k_ref[rows, cols]), _NT,
                               preferred_element_type=F32)

    def scores(c, _):
        ck_all = ck_ref[0, c, 0]
        for pr in range(pairs):
            s = pair_scores(c, pr)
            for h in range(2):
                hd = 2 * pr + h
                u = (s[:, h * tile:(h + 1) * tile] - ck_all[hd:hd + 1, :]) * LOG2E
                u_ref[pr, c, :, h * tile:(h + 1) * tile] = u
                top_ref[hd] = lane_max(top_ref[hd], u)
        return 0

    def diagonal_scores(d):
        ck_all = ck_ref[0, d, 0]
        for pr in range(pairs):
            s = pair_scores(d, pr)
            for h in range(2):
                hd = 2 * pr + h
                for rb in range(nl):
                    r = slice(rb * BLOCK, (rb + 1) * BLOCK)
                    vis = (rb + 1) * BLOCK
                    u = (s[r, h * tile:h * tile + vis] - ck_all[hd:hd + 1, :vis]) * LOG2E
                    last = u[:, rb * BLOCK:] + mask_ref[...]
                    u = jnp.concatenate([u[:, :rb * BLOCK], last], axis=1) if rb else last
                    u_ref[pr, d, r, h * tile:h * tile + vis] = u
                    top_ref[hd, r, :] = lane_max(top_ref[hd, r, :], u)

    def weighted_values(c, pr, ps):
        rows = _chunk_rows(c, tile)
        rhs = jnp.concatenate([_head_split(v_ref[rows, pr * LANES:(pr + 1) * LANES]), ones_ref[...]],
                              axis=1)
        return jnp.dot(jnp.concatenate(ps, axis=1), rhs, preferred_element_type=F32)

    def diagonal_weigh(d, shifts):
        for pr in range(pairs):
            ps = []
            for h in range(2):
                blocks = []
                for rb in range(nl):
                    r = slice(rb * BLOCK, (rb + 1) * BLOCK)
                    vis = (rb + 1) * BLOCK
                    p = jnp.exp2(u_ref[pr, d, r, h * tile:h * tile + vis] - shifts[2 * pr + h][r, :vis])
                    if vis < tile:
                        p = jnp.concatenate([p, jnp.zeros((BLOCK, tile - vis), F32)], axis=1)
                    blocks.append(p.astype(BF16))
                ps.append(jnp.concatenate(blocks, axis=0))
            acc_ref[pr] = weighted_values(d, pr, ps)

    def weigh(c, shifts):
        for pr in range(pairs):
            ps = [jnp.exp2(u_ref[pr, c, :, h * tile:(h + 1) * tile] - shifts[2 * pr + h]).astype(BF16)
                  for h in range(2)]
            acc_ref[pr] += weighted_values(c, pr, ps)

    def query_tile(d):
        top_ref[...] = jnp.full_like(top_ref, NEG)
        for c in range(d):
            scores(c, 0)
        diagonal_scores(d)
        shifts = []
        for hd in range(2 * pairs):
            cq = cq_all[:, hd:hd + 1] * LOG2E
            m = jnp.max(top_ref[hd], axis=-1, keepdims=True) + cq
            shifts.append(jnp.concatenate([jnp.broadcast_to(m - cq, (tile, LANES))] * nl, axis=1))
        diagonal_weigh(d, shifts)
        for c in range(d):
            weigh(c, shifts)
        for pr in range(pairs):
            acc = acc_ref[pr]
            o_ref[:, pr * LANES:(pr + 1) * LANES] = (
                acc[:, :LANES] / acc[:, LANES:]).astype(o_ref.dtype)

    for d in range(u_ref.shape[1]):
        pl.when(i == d)(functools.partial(query_tile, d))


def _fox_attention(proj, cum, batch, seq):
    tile, groups, heads = FOX_TILING.tile, FOX_TILING.groups, FOX_TILING.heads
    nc = seq // tile
    c = cum.reshape(batch, seq, LANES)[:, :, :N_HEADS]
    cq = c.reshape(batch, seq, groups, heads).transpose(0, 2, 1, 3)
    ck = c.reshape(batch, nc, tile, groups, heads).transpose(0, 1, 3, 4, 2)
    specs = [
        pl.BlockSpec((1, 1, tile, heads), lambda b, g, i: (b, g, i, 0)),
        pl.BlockSpec((1, nc, 1, heads, tile), lambda b, g, i: (b, 0, g, 0, 0)),
    ]
    pairs = FOX_TILING.pairs
    scratch = [pltpu.VMEM((pairs, nc, tile, 2 * tile), F32),
               pltpu.VMEM((pairs, tile, 2 * LANES), F32),
               pltpu.VMEM((heads, tile, LANES), F32),
               pltpu.VMEM((BLOCK, BLOCK), F32),
               pltpu.VMEM((2 * tile, LANES), BF16)]
    return _causal_attention_call(_fox_kernel, "fox_attention", FOX_TILING, proj, [cq, ck], specs,
                                  scratch, batch, seq)


def kernel(x, g_pre, g_post, w_in_a, w_out_a, sinks_a, w_in_b, w_out_b, w_in_c, b_f_c, w_out_c):
    batch, seq, d = x.shape
    depth = g_pre.shape[0]
    xf = x.reshape(batch * seq, d)
    wa, wb, wc = w_in_a.astype(BF16), w_in_b.astype(BF16), w_in_c.astype(BF16)
    for i in range(depth):
        kind, j = i % 3, i // 3
        if kind == 0:
            proj = _norm_proj(xf, g_pre[i], wa, j, BRANCH + 2 * KV_A + BRANCH, BF16, PROJ_TN_A)
            o = _swa_attention(proj, sinks_a[j], batch, seq)
            z_col, w_out = BRANCH + 2 * KV_A, w_out_a[j]
        elif kind == 1:
            proj = _norm_proj(xf, g_pre[i], wb, j, 4 * BRANCH, BF16, PROJ_TN)
            o = _stick_attention(proj, batch, seq)
            z_col, w_out = 3 * BRANCH, w_out_b[j]
        else:
            proj = _norm_proj(xf, g_pre[i], wc, j, 4 * BRANCH, BF16, PROJ_TN)
            w_f = jnp.pad(wc[j, :, 4 * BRANCH:], ((0, 0), (0, LANES - N_HEADS)))[None]
            f_logit = _norm_proj(xf, g_pre[i], w_f, 0, LANES, F32, LANES)
            b_f = jnp.pad(b_f_c[j], (0, LANES - N_HEADS)).reshape(1, LANES)
            cum = _cum_log_forget(f_logit, b_f, batch, seq)
            o = _fox_attention(proj, cum, batch, seq)
            z_col, w_out = 3 * BRANCH, w_out_c[j]
        xf = _gate_out(o, proj, z_col, w_out.astype(BF16), g_post[i], xf)
    return xf.reshape(batch, seq, d)
```

```python
import functools
from typing import NamedTuple

import jax
import jax.numpy as jnp
import numpy as np
from jax import lax
from jax.experimental import pallas as pl
from jax.experimental.pallas import tpu as pltpu

HEAD_DIM = 64
N_HEADS = 32
BRANCH = N_HEADS * HEAD_DIM
N_KV_A = 4
KV_A = N_KV_A * HEAD_DIM
GROUP_A = N_HEADS // N_KV_A
BLOCK = 128
LANES = 128
NORM_EPS = 1e-6
NEG = -1e30
SCALE = HEAD_DIM ** -0.5
LOG2E = 1.4426950408889634
N_PAIRS = N_HEADS // 2

F32 = jnp.float32
BF16 = jnp.bfloat16

VMEM_LIMIT = 52 * 1024 * 1024
PROJ_TM = 1024
PROJ_TN_A, PROJ_TN = 1536, 2048
GATE_TM = 512

_NT = (((1,), (1,)), ((), ()))


def _params(sem):
    return pltpu.CompilerParams(dimension_semantics=sem, vmem_limit_bytes=VMEM_LIMIT)


def _rmsnorm(x, g):
    r = lax.rsqrt(jnp.mean(x * x, axis=-1, keepdims=True) + NORM_EPS)
    return x * r * g


def _norm_proj_kernel(x_ref, g_ref, w_ref, o_ref, h_ref):
    @pl.when(pl.program_id(1) == 0)
    def _():
        h_ref[...] = _rmsnorm(x_ref[...], g_ref[...]).astype(BF16)

    o_ref[...] = jnp.dot(h_ref[...], w_ref[...], preferred_element_type=F32).astype(o_ref.dtype)


def _norm_proj(x, g, w, layer, n, out_dtype, tn, tm=PROJ_TM):
    m, d = x.shape
    return pl.pallas_call(
        _norm_proj_kernel,
        grid=(m // tm, n // tn),
        in_specs=[
            pl.BlockSpec((tm, d), lambda i, j: (i, 0)),
            pl.BlockSpec((1, d), lambda i, j: (0, 0)),
            pl.BlockSpec((None, d, tn), lambda i, j: (layer, 0, j)),
        ],
        out_specs=pl.BlockSpec((tm, tn), lambda i, j: (i, j)),
        out_shape=jax.ShapeDtypeStruct((m, n), out_dtype),
        scratch_shapes=[pltpu.VMEM((tm, d), BF16)],
        compiler_params=_params(("parallel", "arbitrary")),
        name="norm_proj",
    )(x, g.reshape(1, d), w)


Z_SPLIT = 4
Z_BLOCK = BRANCH // Z_SPLIT


def _gate_out_kernel(o_ref, *refs):
    z_refs, (w_ref, g_ref, x_ref, out_ref) = refs[:Z_SPLIT], refs[Z_SPLIT:]
    z = jnp.concatenate([r[...] for r in z_refs], axis=1).astype(F32)
    gated = (o_ref[...].astype(F32) * (z * jax.nn.sigmoid(z))).astype(BF16)
    y = jnp.dot(gated, w_ref[...], preferred_element_type=F32)
    out_ref[...] = x_ref[...] + _rmsnorm(y, g_ref[...])


def _gate_out(o, proj, z_col, w_out, g, x, tm=GATE_TM):
    m, d = x.shape
    z0 = z_col // Z_BLOCK
    z_specs = [pl.BlockSpec((tm, Z_BLOCK), lambda i, c=c: (i, z0 + c)) for c in range(Z_SPLIT)]
    return pl.pallas_call(
        _gate_out_kernel,
        grid=(m // tm,),
        in_specs=[pl.BlockSpec((tm, BRANCH), lambda i: (i, 0))] + z_specs + [
            pl.BlockSpec((BRANCH, d), lambda i: (0, 0)),
            pl.BlockSpec((1, d), lambda i: (0, 0)),
            pl.BlockSpec((tm, d), lambda i: (i, 0)),
        ],
        out_specs=pl.BlockSpec((tm, d), lambda i: (i, 0)),
        out_shape=jax.ShapeDtypeStruct((m, d), F32),
        compiler_params=_params(("parallel",)),
        name="gate_out",
    )(o, *([proj] * Z_SPLIT), w_out, g.reshape(1, d), x)


def _head_split(x):
    lane = lax.broadcasted_iota(jnp.int32, x.shape, 1)
    zero = jnp.zeros_like(x)
    return jnp.concatenate([jnp.where(lane < HEAD_DIM, x, zero),
                            jnp.where(lane >= HEAD_DIM, x, zero)], axis=0)


def _alibi_slopes():
    n = N_HEADS
    return (2.0 ** (-8.0 * np.arange(1, n + 1, dtype=np.float32) / n)).astype(np.float32)


def _swa_kernel(sink_ref, q_ref, kp_ref, kc_ref, vp_ref, vc_ref, o_ref, bias_ref):
    qi = lax.broadcasted_iota(jnp.int32, (BLOCK, BLOCK), 0)
    kj = lax.broadcasted_iota(jnp.int32, (BLOCK, BLOCK), 1)
    from_prev = kj > qi
    first = (pl.program_id(0) == 0) & (pl.program_id(1) == 0)

    @pl.when(first)
    def _():
        distf = jnp.where(from_prev, qi + BLOCK - kj, qi - kj).astype(F32)
        slopes = _alibi_slopes()
        for h in range(N_HEADS):
            alibi = -float(slopes[h]) * distf
            bias_ref[1, h] = alibi
            bias_ref[0, h] = jnp.where(from_prev, NEG, alibi)

    has_prev = jnp.minimum(pl.program_id(1), 1)

    low_q = lax.broadcasted_iota(jnp.int32, (BLOCK, LANES), 1) < HEAD_DIM
    src = lax.broadcasted_iota(jnp.int32, (LANES, LANES), 0)
    dst = lax.broadcasted_iota(jnp.int32, (LANES, LANES), 1)
    for g in range(N_KV_A):
        p, half = divmod(g, 2)
        cols = slice(p * LANES, (p + 1) * LANES)
        k2 = jnp.concatenate([kp_ref[:, cols], kc_ref[:, cols]], axis=0)
        v2 = jnp.concatenate([vp_ref[:, cols], vc_ref[:, cols]], axis=0)
        pick = jnp.where(src == dst % HEAD_DIM + half * HEAD_DIM, 1.0, 0.0).astype(BF16)
        kd = jnp.dot(k2, pick, preferred_element_type=F32).astype(BF16)
        vd = jnp.dot(v2, pick, preferred_element_type=F32).astype(BF16)

        q = q_ref[:, g * GROUP_A * HEAD_DIM:(g + 1) * GROUP_A * HEAD_DIM] * SCALE
        rows = [_head_split(q[:, i * LANES:(i + 1) * LANES]) for i in range(GROUP_A // 2)]
        lhs = jnp.concatenate(rows, axis=0)
        s = lax.dot_general(lhs, kd, _NT, preferred_element_type=F32)
        ps, ls = [], []
        for hh in range(GROUP_A):
            h = g * GROUP_A + hh
            sb = s[hh * BLOCK:(hh + 1) * BLOCK]
            sh = jnp.where(from_prev, sb[:, :BLOCK], sb[:, BLOCK:]) + bias_ref[has_prev, h]
            sink = sink_ref[h]
            m = jnp.maximum(jnp.max(sh, axis=-1, keepdims=True), sink)
            e = jnp.exp(sh - m)
            ls.append(jnp.sum(e, axis=-1, keepdims=True) + jnp.exp(sink - m))
            ps.append(jnp.concatenate([jnp.where(from_prev, e, 0.0), jnp.where(from_prev, 0.0, e)],
                                      axis=1).astype(BF16))
        pv = jnp.dot(jnp.concatenate(ps, axis=0), vd, preferred_element_type=F32)
        for i in range(GROUP_A // 2):
            o0 = pv[(2 * i) * BLOCK:(2 * i + 1) * BLOCK] / ls[2 * i]
            o1 = pv[(2 * i + 1) * BLOCK:(2 * i + 2) * BLOCK] / ls[2 * i + 1]
            c0 = (g * GROUP_A // 2 + i) * LANES
            o_ref[:, c0:c0 + LANES] = jnp.where(low_q, o0, o1).astype(o_ref.dtype)


def _swa_attention(proj, sinks, batch, seq):
    m = proj.shape[0]
    nb = seq // BLOCK
    kcol = BRANCH // KV_A
    vcol = kcol + 1

    def cur(b, n):
        return b * nb + n

    def prev(b, n):
        return b * nb + jnp.maximum(n - 1, 0)

    return pl.pallas_call(
        _swa_kernel,
        grid=(batch, nb),
        in_specs=[
            pl.BlockSpec(memory_space=pltpu.SMEM),
            pl.BlockSpec((BLOCK, BRANCH), lambda b, n: (cur(b, n), 0)),
            pl.BlockSpec((BLOCK, KV_A), lambda b, n: (prev(b, n), kcol)),
            pl.BlockSpec((BLOCK, KV_A), lambda b, n: (cur(b, n), kcol)),
            pl.BlockSpec((BLOCK, KV_A), lambda b, n: (prev(b, n), vcol)),
            pl.BlockSpec((BLOCK, KV_A), lambda b, n: (cur(b, n), vcol)),
        ],
        out_specs=pl.BlockSpec((BLOCK, BRANCH), lambda b, n: (cur(b, n), 0)),
        out_shape=jax.ShapeDtypeStruct((m, BRANCH), BF16),
        scratch_shapes=[pltpu.VMEM((2, N_HEADS, BLOCK, BLOCK), F32)],
        compiler_params=_params(("arbitrary", "arbitrary")),
        name="swa_attention",
    )(sinks, proj, proj, proj, proj, proj)


class _Tiling(NamedTuple):
    tile: int
    pairs: int

    @property
    def groups(self):
        return N_PAIRS // self.pairs

    @property
    def heads(self):
        return 2 * self.pairs


STICK_TILING = _Tiling(tile=256, pairs=8)
FOX_TILING = _Tiling(tile=512, pairs=2)


def _chunk_rows(c, tile):
    if isinstance(c, int):
        return pl.ds(c * tile, tile)
    return pl.ds(pl.multiple_of(c * tile, tile), tile)


def _causal_attention_call(kernel_fn, name, tiling, proj, extra_inputs, extra_specs, scratch,
                           batch, seq):
    m = proj.shape[0]
    tile, groups = tiling.tile, tiling.groups
    nq = seq // tile
    width = tiling.pairs * LANES
    return pl.pallas_call(
        functools.partial(kernel_fn, tiling=tiling),
        grid=(batch, groups, nq),
        in_specs=[
            pl.BlockSpec((tile, width), lambda b, g, i: (b * nq + i, g)),
            pl.BlockSpec((seq, width), lambda b, g, i: (b, groups + g)),
            pl.BlockSpec((seq, width), lambda b, g, i: (b, 2 * groups + g)),
        ] + extra_specs,
        out_specs=pl.BlockSpec((tile, width), lambda b, g, i: (b * nq + i, g)),
        out_shape=jax.ShapeDtypeStruct((m, BRANCH), BF16),
        scratch_shapes=scratch,
        compiler_params=_params(("arbitrary", "arbitrary", "arbitrary")),
        name=name,
    )(proj, proj, proj, *extra_inputs)


def _suffix_matrix():
    j = lax.broadcasted_iota(jnp.int32, (BLOCK, 2 * BLOCK), 0)
    s = lax.broadcasted_iota(jnp.int32, (BLOCK, 2 * BLOCK), 1)
    return jnp.where((s >= BLOCK) | (j > s), 1.0, 0.0).astype(BF16)


SOFTPLUS_CLAMP = 60.0
EXP_UNDERFLOW = 105.0


def _softplus(s):
    return jnp.maximum(s, jnp.log(1.0 + jnp.exp(jnp.minimum(s, SOFTPLUS_CLAMP))))


def _stick_kernel(q_ref, k_ref, v_ref, o_ref, acc_ref, carry_ref, *, tiling):
    tile, pairs = tiling
    nkb = tile // BLOCK
    i = pl.program_id(2)
    u = _suffix_matrix()

    row = lax.broadcasted_iota(jnp.int32, (tile, tile), 0)
    col = lax.broadcasted_iota(jnp.int32, (tile, tile), 1)
    before = jnp.concatenate([col < row, col < row], axis=1)

    def chunk(c, mask):
        rows = _chunk_rows(c, tile)
        for pr in range(pairs):
            cols = slice(pr * LANES, (pr + 1) * LANES)
            q2 = q_ref[:, cols] * SCALE
            s = lax.dot_general(q2, _head_split(k_ref[rows, cols]), _NT,
                                preferred_element_type=F32)
            sp = _softplus(s)
            if mask is not None:
                sp = jnp.where(mask, sp, 0.0)
            spb = sp.astype(BF16)
            base = s - sp
            ps = [None] * (2 * nkb)
            for h in range(2):
                carry = carry_ref[pr, h]
                for kb in reversed(range(nkb)):
                    sl = slice(h * tile + kb * BLOCK, h * tile + (kb + 1) * BLOCK)
                    cs = jnp.dot(spb[:, sl], u, preferred_element_type=F32)
                    ps[h * nkb + kb] = jnp.exp(base[:, sl] - cs[:, :BLOCK] - carry)
                    carry = carry + cs[:, BLOCK:]
                carry_ref[pr, h] = carry
            p = jnp.concatenate(ps, axis=1)
            if mask is not None:
                p = jnp.where(mask, p, 0.0)
            acc_ref[pr] += jnp.dot(p.astype(BF16), _head_split(v_ref[rows, cols]),
                                   preferred_element_type=F32)

    acc_ref[...] = jnp.zeros_like(acc_ref)
    carry_ref[...] = jnp.zeros_like(carry_ref)
    chunk(i, before)

    def weights_vanish():
        return (jnp.min(carry_ref[...]) >= EXP_UNDERFLOW).astype(jnp.int32)

    def cond(st):
        t, done = st
        return jnp.logical_and(t < i, done == 0)

    def body(st):
        t, _ = st
        chunk(i - 1 - t, None)
        return t + 1, weights_vanish()

    lax.while_loop(cond, body, (jnp.int32(0), weights_vanish()))
    for pr in range(pairs):
        o_ref[:, pr * LANES:(pr + 1) * LANES] = acc_ref[pr].astype(o_ref.dtype)


def _stick_attention(proj, batch, seq):
    tile, pairs = STICK_TILING
    scratch = [pltpu.VMEM((pairs, tile, LANES), F32),
               pltpu.VMEM((pairs, 2, tile, LANES), F32)]
    return _causal_attention_call(_stick_kernel, "stick_attention", STICK_TILING, proj, [], [],
                                  scratch, batch, seq)


def _split3(x):
    hi = x.astype(BF16)
    r = x - hi.astype(F32)
    mid = r.astype(BF16)
    lo = (r - mid.astype(F32)).astype(BF16)
    return hi, mid, lo


def _cum_kernel(fl_ref, bf_ref, cum_ref):
    x = fl_ref[...] + bf_ref[...]
    lf = jnp.minimum(x, 0.0) - jnp.log(1.0 + jnp.exp(-jnp.abs(x)))
    t = lax.broadcasted_iota(jnp.int32, (BLOCK, BLOCK), 0)
    j = lax.broadcasted_iota(jnp.int32, (BLOCK, BLOCK), 1)
    tri = jnp.where(j <= t, 1.0, 0.0).astype(BF16)
    parts = jnp.concatenate(_split3(lf), axis=1)
    total = jnp.zeros((1, LANES), F32)
    for blk in range(x.shape[0] // BLOCK):
        rows = slice(blk * BLOCK, (blk + 1) * BLOCK)
        c = jnp.dot(tri, parts[rows], preferred_element_type=F32)
        c = c[:, :LANES] + c[:, LANES:2 * LANES] + c[:, 2 * LANES:] + total
        cum_ref[rows, :] = c
        total = c[BLOCK - 1:BLOCK, :]


def _cum_log_forget(f_logit, b_f, batch, seq):
    return pl.pallas_call(
        _cum_kernel,
        grid=(batch,),
        in_specs=[pl.BlockSpec((seq, LANES), lambda b: (b, 0)),
                  pl.BlockSpec((1, LANES), lambda b: (0, 0))],
        out_specs=pl.BlockSpec((seq, LANES), lambda b: (b, 0)),
        out_shape=jax.ShapeDtypeStruct((batch * seq, LANES), F32),
        compiler_params=_params(("parallel",)),
        name="cum_log_forget",
    )(f_logit, b_f)


def _fox_kernel(q_ref, k_ref, v_ref, cq_ref, ck_ref, o_ref, u_ref, acc_ref, top_ref, mask_ref,
                ones_ref, *, tiling):
    tile, pairs = tiling
    nl = tile // LANES
    i = pl.program_id(2)
    cq_all = cq_ref[0, 0]

    @pl.when((pl.program_id(0) == 0) & (pl.program_id(1) == 0) & (i == 0))
    def _():
        row = lax.broadcasted_iota(jnp.int32, (BLOCK, BLOCK), 0)
        col = lax.broadcasted_iota(jnp.int32, (BLOCK, BLOCK), 1)
        mask_ref[...] = jnp.where(col <= row, 0.0, NEG)
        r2 = lax.broadcasted_iota(jnp.int32, (2 * tile, LANES), 0)
        l2 = lax.broadcasted_iota(jnp.int32, (2 * tile, LANES), 1)
        ones_ref[...] = jnp.where((r2 < tile) == (l2 < HEAD_DIM), 1.0, 0.0).astype(BF16)

    def lane_max(top, u):
        for b in range(u.shape[1] // LANES):
            top = jnp.maximum(top, u[:, b * LANES:(b + 1) * LANES])
        return top

    def pair_scores(c, pr):
        rows = _chunk_rows(c, tile)
        cols = slice(pr * LANES, (pr + 1) * LANES)
        return lax.dot_general(q_ref[:, cols] * SCALE, _head_split(k_ref[rows, cols]), _NT,
                               preferred_element_type=F32)

    def scores(c, _):
        ck_all = ck_ref[0, c, 0]
        for pr in range(pairs):
            s = pair_scores(c, pr)
            for h in range(2):
                hd = 2 * pr + h
                u = (s[:, h * tile:(h + 1) * tile] - ck_all[hd:hd + 1, :]) * LOG2E
                u_ref[pr, c, :, h * tile:(h + 1) * tile] = u
                top_ref[hd] = lane_max(top_ref[hd], u)
        return 0

    def diagonal_scores(d):
        ck_all = ck_ref[0, d, 0]
        for pr in range(pairs):
            s = pair_scores(d, pr)
            for h in range(2):
                hd = 2 * pr + h
                for rb in range(nl):
                    r = slice(rb * BLOCK, (rb + 1) * BLOCK)
                    vis = (rb + 1) * BLOCK
                    u = (s[r, h * tile:h * tile + vis] - ck_all[hd:hd + 1, :vis]) * LOG2E
                    last = u[:, rb * BLOCK:] + mask_ref[...]
                    u = jnp.concatenate([u[:, :rb * BLOCK], last], axis=1) if rb else last
                    u_ref[pr, d, r, h * tile:h * tile + vis] = u
                    top_ref[hd, r, :] = lane_max(top_ref[hd, r, :], u)

    def weighted_values(c, pr, ps):
        rows = _chunk_rows(c, tile)
        rhs = jnp.concatenate([_head_split(v_ref[rows, pr * LANES:(pr + 1) * LANES]), ones_ref[...]],
                              axis=1)
        return jnp.dot(jnp.concatenate(ps, axis=1), rhs, preferred_element_type=F32)

    def diagonal_weigh(d, shifts):
        for pr in range(pairs):
            ps = []
            for h in range(2):
                blocks = []
                for rb in range(nl):
                    r = slice(rb * BLOCK, (rb + 1) * BLOCK)
                    vis = (rb + 1) * BLOCK
                    p = jnp.exp2(u_ref[pr, d, r, h * tile:h * tile + vis] - shifts[2 * pr + h][r, :vis])
                    if vis < tile:
                        p = jnp.concatenate([p, jnp.zeros((BLOCK, tile - vis), F32)], axis=1)
                    blocks.append(p.astype(BF16))
                ps.append(jnp.concatenate(blocks, axis=0))
            acc_ref[pr] = weighted_values(d, pr, ps)

    def weigh(c, shifts):
        for pr in range(pairs):
            ps = [jnp.exp2(u_ref[pr, c, :, h * tile:(h + 1) * tile] - shifts[2 * pr + h]).astype(BF16)
                  for h in range(2)]
            acc_ref[pr] += weighted_values(c, pr, ps)

    def query_tile(d):
        top_ref[...] = jnp.full_like(top_ref, NEG)
        for c in range(d):
            scores(c, 0)
        diagonal_scores(d)
        shifts = []
        for hd in range(2 * pairs):
            cq = cq_all[:, hd:hd + 1] * LOG2E
            m = jnp.max(top_ref[hd], axis=-1, keepdims=True) + cq
            shifts.append(jnp.concatenate([jnp.broadcast_to(m - cq, (tile, LANES))] * nl, axis=1))
        diagonal_weigh(d, shifts)
        for c in range(d):
            weigh(c, shifts)
        for pr in range(pairs):
            acc = acc_ref[pr]
            o_ref[:, pr * LANES:(pr + 1) * LANES] = (
                acc[:, :LANES] / acc[:, LANES:]).astype(o_ref.dtype)

    for d in range(u_ref.shape[1]):
        pl.when(i == d)(functools.partial(query_tile, d))


def _fox_attention(proj, cum, batch, seq):
    tile, groups, heads = FOX_TILING.tile, FOX_TILING.groups, FOX_TILING.heads
    nc = seq // tile
    c = cum.reshape(batch, seq, LANES)[:, :, :N_HEADS]
    cq = c.reshape(batch, seq, groups, heads).transpose(0, 2, 1, 3)
    ck = c.reshape(batch, nc, tile, groups, heads).transpose(0, 1, 3, 4, 2)
    specs = [
        pl.BlockSpec((1, 1, tile, heads), lambda b, g, i: (b, g, i, 0)),
        pl.BlockSpec((1, nc, 1, heads, tile), lambda b, g, i: (b, 0, g, 0, 0)),
    ]
    pairs = FOX_TILING.pairs
    scratch = [pltpu.VMEM((pairs, nc, tile, 2 * tile), F32),
               pltpu.VMEM((pairs, tile, 2 * LANES), F32),
               pltpu.VMEM((heads, tile, LANES), F32),
               pltpu.VMEM((BLOCK, BLOCK), F32),
               pltpu.VMEM((2 * tile, LANES), BF16)]
    return _causal_attention_call(_fox_kernel, "fox_attention", FOX_TILING, proj, [cq, ck], specs,
                                  scratch, batch, seq)


def kernel(x, g_pre, g_post, w_in_a, w_out_a, sinks_a, w_in_b, w_out_b, w_in_c, b_f_c, w_out_c):
    batch, seq, d = x.shape
    depth = g_pre.shape[0]
    xf = x.reshape(batch * seq, d)
    wa, wb, wc = w_in_a.astype(BF16), w_in_b.astype(BF16), w_in_c.astype(BF16)
    for i in range(depth):
        kind, j = i % 3, i // 3
        if kind == 0:
            proj = _norm_proj(xf, g_pre[i], wa, j, BRANCH + 2 * KV_A + BRANCH, BF16, PROJ_TN_A)
            o = _swa_attention(proj, sinks_a[j], batch, seq)
            z_col, w_out = BRANCH + 2 * KV_A, w_out_a[j]
        elif kind == 1:
            proj = _norm_proj(xf, g_pre[i], wb, j, 4 * BRANCH, BF16, PROJ_TN)
            o = _stick_attention(proj, batch, seq)
            z_col, w_out = 3 * BRANCH, w_out_b[j]
        else:
            proj = _norm_proj(xf, g_pre[i], wc, j, 4 * BRANCH, BF16, PROJ_TN)
            w_f = jnp.pad(wc[j, :, 4 * BRANCH:], ((0, 0), (0, LANES - N_HEADS)))[None]
            f_logit = _norm_proj(xf, g_pre[i], w_f, 0, LANES, F32, LANES)
            b_f = jnp.pad(b_f_c[j], (0, LANES - N_HEADS)).reshape(1, LANES)
            cum = _cum_log_forget(f_logit, b_f, batch, seq)
            o = _fox_attention(proj, cum, batch, seq)
            z_col, w_out = 3 * BRANCH, w_out_c[j]
        xf = _gate_out(o, proj, z_col, w_out.astype(BF16), g_post[i], xf)
    return xf.reshape(batch, seq, d)
```

```python
import functools
from typing import NamedTuple

import jax
import jax.numpy as jnp
import numpy as np
from jax import lax
from jax.experimental import pallas as pl
from jax.experimental.pallas import tpu as pltpu

HEAD_DIM = 64
N_HEADS = 32
BRANCH = N_HEADS * HEAD_DIM
N_KV_A = 4
KV_A = N_KV_A * HEAD_DIM
GROUP_A = N_HEADS // N_KV_A
BLOCK = 128
LANES = 128
NORM_EPS = 1e-6
NEG = -1e30
SCALE = HEAD_DIM ** -0.5
LOG2E = 1.4426950408889634
N_PAIRS = N_HEADS // 2

F32 = jnp.float32
BF16 = jnp.bfloat16

VMEM_LIMIT = 52 * 1024 * 1024
PROJ_TM = 1024
PROJ_TN_A, PROJ_TN = 1536, 2048
GATE_TM = 512

_NT = (((1,), (1,)), ((), ()))


def _params(sem):
    return pltpu.CompilerParams(dimension_semantics=sem, vmem_limit_bytes=VMEM_LIMIT)


def _rmsnorm(x, g):
    r = lax.rsqrt(jnp.mean(x * x, axis=-1, keepdims=True) + NORM_EPS)
    return x * r * g


def _norm_proj_kernel(x_ref, g_ref, w_ref, o_ref, h_ref):
    @pl.when(pl.program_id(1) == 0)
    def _():
        h_ref[...] = _rmsnorm(x_ref[...], g_ref[...]).astype(BF16)

    o_ref[...] = jnp.dot(h_ref[...], w_ref[...], preferred_element_type=F32).astype(o_ref.dtype)


def _norm_proj(x, g, w, layer, n, out_dtype, tn, tm=PROJ_TM):
    m, d = x.shape
    return pl.pallas_call(
        _norm_proj_kernel,
        grid=(m // tm, n // tn),
        in_specs=[
            pl.BlockSpec((tm, d), lambda i, j: (i, 0)),
            pl.BlockSpec((1, d), lambda i, j: (0, 0)),
            pl.BlockSpec((None, d, tn), lambda i, j: (layer, 0, j)),
        ],
        out_specs=pl.BlockSpec((tm, tn), lambda i, j: (i, j)),
        out_shape=jax.ShapeDtypeStruct((m, n), out_dtype),
        scratch_shapes=[pltpu.VMEM((tm, d), BF16)],
        compiler_params=_params(("parallel", "arbitrary")),
        name="norm_proj",
    )(x, g.reshape(1, d), w)


Z_SPLIT = 4
Z_BLOCK = BRANCH // Z_SPLIT


def _gate_out_kernel(o_ref, *refs):
    z_refs, (w_ref, g_ref, x_ref, out_ref) = refs[:Z_SPLIT], refs[Z_SPLIT:]
    z = jnp.concatenate([r[...] for r in z_refs], axis=1).astype(F32)
    gated = (o_ref[...].astype(F32) * (z * jax.nn.sigmoid(z))).astype(BF16)
    y = jnp.dot(gated, w_ref[...], preferred_element_type=F32)
    out_ref[...] = x_ref[...] + _rmsnorm(y, g_ref[...])


def _gate_out(o, proj, z_col, w_out, g, x, tm=GATE_TM):
    m, d = x.shape
    z0 = z_col // Z_BLOCK
    z_specs = [pl.BlockSpec((tm, Z_BLOCK), lambda i, c=c: (i, z0 + c)) for c in range(Z_SPLIT)]
    return pl.pallas_call(
        _gate_out_kernel,
        grid=(m // tm,),
        in_specs=[pl.BlockSpec((tm, BRANCH), lambda i: (i, 0))] + z_specs + [
            pl.BlockSpec((BRANCH, d), lambda i: (0, 0)),
            pl.BlockSpec((1, d), lambda i: (0, 0)),
            pl.BlockSpec((tm, d), lambda i: (i, 0)),
        ],
        out_specs=pl.BlockSpec((tm, d), lambda i: (i, 0)),
        out_shape=jax.ShapeDtypeStruct((m, d), F32),
        compiler_params=_params(("parallel",)),
        name="gate_out",
    )(o, *([proj] * Z_SPLIT), w_out, g.reshape(1, d), x)


def _head_split(x):
    lane = lax.broadcasted_iota(jnp.int32, x.shape, 1)
    zero = jnp.zeros_like(x)
    return jnp.concatenate([jnp.where(lane < HEAD_DIM, x, zero),
                            jnp.where(lane >= HEAD_DIM, x, zero)], axis=0)


SWA_BLOCKS = 2


def _alibi_slopes():
    n = N_HEADS
    return (2.0 ** (-8.0 * np.arange(1, n + 1, dtype=np.float32) / n)).astype(np.float32)


def _swa_kernel(sink_ref, q_ref, kp_ref, kc_ref, vp_ref, vc_ref, o_ref, bias_ref):
    qi = lax.broadcasted_iota(jnp.int32, (BLOCK, BLOCK), 0)
    kj = lax.broadcasted_iota(jnp.int32, (BLOCK, BLOCK), 1)
    from_prev = kj > qi
    first = (pl.program_id(0) == 0) & (pl.program_id(1) == 0)

    @pl.when(first)
    def _():
        distf = jnp.where(from_prev, qi + BLOCK - kj, qi - kj).astype(F32)
        slopes = _alibi_slopes()
        for h in range(N_HEADS):
            alibi = -float(slopes[h]) * distf
            bias_ref[1, h] = alibi
            bias_ref[0, h] = jnp.where(from_prev, NEG, alibi)

    low_q = lax.broadcasted_iota(jnp.int32, (BLOCK, LANES), 1) < HEAD_DIM
    src = lax.broadcasted_iota(jnp.int32, (LANES, LANES), 0)
    dst = lax.broadcasted_iota(jnp.int32, (LANES, LANES), 1)
    for sub in range(SWA_BLOCKS):
        rows = slice(sub * BLOCK, (sub + 1) * BLOCK)
        before = slice((sub - 1) * BLOCK, sub * BLOCK)
        has_prev = 1 if sub else jnp.minimum(pl.program_id(1), 1)
        for g in range(N_KV_A):
            p, half = divmod(g, 2)
            cols = slice(p * LANES, (p + 1) * LANES)
            k_prev = kc_ref[before, cols] if sub else kp_ref[:, cols]
            v_prev = vc_ref[before, cols] if sub else vp_ref[:, cols]
            k2 = jnp.concatenate([k_prev, kc_ref[rows, cols]], axis=0)
            v2 = jnp.concatenate([v_prev, vc_ref[rows, cols]], axis=0)
            pick = jnp.where(src == dst % HEAD_DIM + half * HEAD_DIM, 1.0, 0.0).astype(BF16)
            kd = jnp.dot(k2, pick, preferred_element_type=F32).astype(BF16)
            vd = jnp.dot(v2, pick, preferred_element_type=F32).astype(BF16)

            q = q_ref[rows, g * GROUP_A * HEAD_DIM:(g + 1) * GROUP_A * HEAD_DIM] * SCALE
            lhs = jnp.concatenate([_head_split(q[:, i * LANES:(i + 1) * LANES])
                                   for i in range(GROUP_A // 2)], axis=0)
            s = lax.dot_general(lhs, kd, _NT, preferred_element_type=F32)
            ps, ls = [], []
            for hh in range(GROUP_A):
                h = g * GROUP_A + hh
                sb = s[hh * BLOCK:(hh + 1) * BLOCK]
                sh = jnp.where(from_prev, sb[:, :BLOCK], sb[:, BLOCK:]) + bias_ref[has_prev, h]
                sink = sink_ref[h]
                m = jnp.maximum(jnp.max(sh, axis=-1, keepdims=True), sink)
                e = jnp.exp(sh - m)
                ls.append(jnp.sum(e, axis=-1, keepdims=True) + jnp.exp(sink - m))
                ps.append(jnp.concatenate([jnp.where(from_prev, e, 0.0),
                                           jnp.where(from_prev, 0.0, e)], axis=1).astype(BF16))
            pv = jnp.dot(jnp.concatenate(ps, axis=0), vd, preferred_element_type=F32)
            for i in range(GROUP_A // 2):
                o0 = pv[(2 * i) * BLOCK:(2 * i + 1) * BLOCK] / ls[2 * i]
                o1 = pv[(2 * i + 1) * BLOCK:(2 * i + 2) * BLOCK] / ls[2 * i + 1]
                c0 = (g * GROUP_A // 2 + i) * LANES
                o_ref[rows, c0:c0 + LANES] = jnp.where(low_q, o0, o1).astype(o_ref.dtype)


def _swa_attention(proj, sinks, batch, seq):
    m = proj.shape[0]
    step = SWA_BLOCKS * BLOCK
    ns = seq // step
    kcol = BRANCH // KV_A
    vcol = kcol + 1

    def cur(b, n):
        return b * ns + n

    def prev(b, n):
        return b * ns * SWA_BLOCKS + jnp.maximum(n * SWA_BLOCKS - 1, 0)

    return pl.pallas_call(
        _swa_kernel,
        grid=(batch, ns),
        in_specs=[
            pl.BlockSpec(memory_space=pltpu.SMEM),
            pl.BlockSpec((step, BRANCH), lambda b, n: (cur(b, n), 0)),
            pl.BlockSpec((BLOCK, KV_A), lambda b, n: (prev(b, n), kcol)),
            pl.BlockSpec((step, KV_A), lambda b, n: (cur(b, n), kcol)),
            pl.BlockSpec((BLOCK, KV_A), lambda b, n: (prev(b, n), vcol)),
            pl.BlockSpec((step, KV_A), lambda b, n: (cur(b, n), vcol)),
        ],
        out_specs=pl.BlockSpec((step, BRANCH), lambda b, n: (cur(b, n), 0)),
        out_shape=jax.ShapeDtypeStruct((m, BRANCH), BF16),
        scratch_shapes=[pltpu.VMEM((2, N_HEADS, BLOCK, BLOCK), F32)],
        compiler_params=_params(("arbitrary", "arbitrary")),
        name="swa_attention",
    )(sinks, proj, proj, proj, proj, proj)


class _Tiling(NamedTuple):
    tile: int
    pairs: int

    @property
    def groups(self):
        return N_PAIRS // self.pairs

    @property
    def heads(self):
        return 2 * self.pairs


STICK_TILING = _Tiling(tile=256, pairs=8)
FOX_TILING = _Tiling(tile=512, pairs=2)


def _chunk_rows(c, tile):
    if isinstance(c, int):
        return pl.ds(c * tile, tile)
    return pl.ds(pl.multiple_of(c * tile, tile), tile)


def _causal_attention_call(kernel_fn, name, tiling, proj, extra_inputs, extra_specs, scratch,
                           batch, seq):
    m = proj.shape[0]
    tile, groups = tiling.tile, tiling.groups
    nq = seq // tile
    width = tiling.pairs * LANES
    return pl.pallas_call(
        functools.partial(kernel_fn, tiling=tiling),
        grid=(batch, groups, nq),
        in_specs=[
            pl.BlockSpec((tile, width), lambda b, g, i: (b * nq + i, g)),
            pl.BlockSpec((seq, width), lambda b, g, i: (b, groups + g)),
            pl.BlockSpec((seq, width), lambda b, g, i: (b, 2 * groups + g)),
        ] + extra_specs,
        out_specs=pl.BlockSpec((tile, width), lambda b, g, i: (b * nq + i, g)),
        out_shape=jax.ShapeDtypeStruct((m, BRANCH), BF16),
        scratch_shapes=scratch,
        compiler_params=_params(("arbitrary", "arbitrary", "arbitrary")),
        name=name,
    )(proj, proj, proj, *extra_inputs)


def _suffix_matrix():
    j = lax.broadcasted_iota(jnp.int32, (BLOCK, 2 * BLOCK), 0)
    s = lax.broadcasted_iota(jnp.int32, (BLOCK, 2 * BLOCK), 1)
    return jnp.where((s >= BLOCK) | (j > s), 1.0, 0.0).astype(BF16)


SOFTPLUS_CLAMP = 60.0
EXP_UNDERFLOW = 105.0


def _softplus(s):
    return jnp.maximum(s, jnp.log(1.0 + jnp.exp(jnp.minimum(s, SOFTPLUS_CLAMP))))


def _stick_kernel(q_ref, k_ref, v_ref, o_ref, acc_ref, carry_ref, *, tiling):
    tile, pairs = tiling
    nkb = tile // BLOCK
    i = pl.program_id(2)
    u = _suffix_matrix()

    row = lax.broadcasted_iota(jnp.int32, (tile, tile), 0)
    col = lax.broadcasted_iota(jnp.int32, (tile, tile), 1)
    before = jnp.concatenate([col < row, col < row], axis=1)

    def chunk(c, mask):
        rows = _chunk_rows(c, tile)
        for pr in range(pairs):
            cols = slice(pr * LANES, (pr + 1) * LANES)
            q2 = q_ref[:, cols] * SCALE
            s = lax.dot_general(q2, _head_split(k_ref[rows, cols]), _NT,
                                preferred_element_type=F32)
            sp = _softplus(s)
            if mask is not None:
                sp = jnp.where(mask, sp, 0.0)
            spb = sp.astype(BF16)
            base = s - sp
            ps = [None] * (2 * nkb)
            for h in range(2):
                carry = carry_ref[pr, h]
                for kb in reversed(range(nkb)):
                    sl = slice(h * tile + kb * BLOCK, h * tile + (kb + 1) * BLOCK)
                    cs = jnp.dot(spb[:, sl], u, preferred_element_type=F32)
                    ps[h * nkb + kb] = jnp.exp(base[:, sl] - cs[:, :BLOCK] - carry)
                    carry = carry + cs[:, BLOCK:]
                carry_ref[pr, h] = carry
            p = jnp.concatenate(ps, axis=1)
            if mask is not None:
                p = jnp.where(mask, p, 0.0)
            acc_ref[pr] += jnp.dot(p.astype(BF16), _head_split(v_ref[rows, cols]),
                                   preferred_element_type=F32)

    acc_ref[...] = jnp.zeros_like(acc_ref)
    carry_ref[...] = jnp.zeros_like(carry_ref)
    chunk(i, before)

    def weights_vanish():
        return (jnp.min(carry_ref[...]) >= EXP_UNDERFLOW).astype(jnp.int32)

    def cond(st):
        t, done = st
        return jnp.logical_and(t < i, done == 0)

    def body(st):
        t, _ = st
        chunk(i - 1 - t, None)
        return t + 1, weights_vanish()

    lax.while_loop(cond, body, (jnp.int32(0), weights_vanish()))
    for pr in range(pairs):
        o_ref[:, pr * LANES:(pr + 1) * LANES] = acc_ref[pr].astype(o_ref.dtype)


def _stick_attention(proj, batch, seq):
    tile, pairs = STICK_TILING
    scratch = [pltpu.VMEM((pairs, tile, LANES), F32),
               pltpu.VMEM((pairs, 2, tile, LANES), F32)]
    return _causal_attention_call(_stick_kernel, "stick_attention", STICK_TILING, proj, [], [],
                                  scratch, batch, seq)


def _split3(x):
    hi = x.astype(BF16)
    r = x - hi.astype(F32)
    mid = r.astype(BF16)
    lo = (r - mid.astype(F32)).astype(BF16)
    return hi, mid, lo


def _cum_kernel(fl_ref, bf_ref, cum_ref):
    x = fl_ref[...] + bf_ref[...]
    lf = jnp.minimum(x, 0.0) - jnp.log(1.0 + jnp.exp(-jnp.abs(x)))
    t = lax.broadcasted_iota(jnp.int32, (BLOCK, BLOCK), 0)
    j = lax.broadcasted_iota(jnp.int32, (BLOCK, BLOCK), 1)
    tri = jnp.where(j <= t, 1.0, 0.0).astype(BF16)
    parts = jnp.concatenate(_split3(lf), axis=1)
    total = jnp.zeros((1, LANES), F32)
    for blk in range(x.shape[0] // BLOCK):
        rows = slice(blk * BLOCK, (blk + 1) * BLOCK)
        c = jnp.dot(tri, parts[rows], preferred_element_type=F32)
        c = c[:, :LANES] + c[:, LANES:2 * LANES] + c[:, 2 * LANES:] + total
        cum_ref[rows, :] = c
        total = c[BLOCK - 1:BLOCK, :]


def _cum_log_forget(f_logit, b_f, batch, seq):
    return pl.pallas_call(
        _cum_kernel,
        grid=(batch,),
        in_specs=[pl.BlockSpec((seq, LANES), lambda b: (b, 0)),
                  pl.BlockSpec((1, LANES), lambda b: (0, 0))],
        out_specs=pl.BlockSpec((seq, LANES), lambda b: (b, 0)),
        out_shape=jax.ShapeDtypeStruct((batch * seq, LANES), F32),
        compiler_params=_params(("parallel",)),
        name="cum_log_forget",
    )(f_logit, b_f)


def _fox_kernel(q_ref, k_ref, v_ref, cq_ref, ck_ref, o_ref, u_ref, acc_ref, top_ref, mask_ref,
                ones_ref, *, tiling):
    tile, pairs = tiling
    nl = tile // LANES
    i = pl.program_id(2)
    cq_all = cq_ref[0, 0]

    @pl.when((pl.program_id(0) == 0) & (pl.program_id(1) == 0) & (i == 0))
    def _():
        row = lax.broadcasted_iota(jnp.int32, (BLOCK, BLOCK), 0)
        col = lax.broadcasted_iota(jnp.int32, (BLOCK, BLOCK), 1)
        mask_ref[...] = jnp.where(col <= row, 0.0, NEG)
        r2 = lax.broadcasted_iota(jnp.int32, (2 * tile, LANES), 0)
        l2 = lax.broadcasted_iota(jnp.int32, (2 * tile, LANES), 1)
        ones_ref[...] = jnp.where((r2 < tile) == (l2 < HEAD_DIM), 1.0, 0.0).astype(BF16)

    def lane_max(top, u):
        for b in range(u.shape[1] // LANES):
            top = jnp.maximum(top, u[:, b * LANES:(b + 1) * LANES])
        return top

    def pair_scores(c, pr):
        rows = _chunk_rows(c, tile)
        cols = slice(pr * LANES, (pr + 1) * LANES)
        return lax.dot_general(q_ref[:, cols] * SCALE, _head_split(k_ref[rows, cols]), _NT,
                               preferred_element_type=F32)

    def scores(c, _):
        ck_all = ck_ref[0, c, 0]
        for pr in range(pairs):
            s = pair_scores(c, pr)
            for h in range(2):
                hd = 2 * pr + h
                u = (s[:, h * tile:(h + 1) * tile] - ck_all[hd:hd + 1, :]) * LOG2E
                u_ref[pr, c, :, h * tile:(h + 1) * tile] = u
                top_ref[hd] = lane_max(top_ref[hd], u)
        return 0

    def diagonal_scores(d):
        ck_all = ck_ref[0, d, 0]
        for pr in range(pairs):
            s = pair_scores(d, pr)
            for h in range(2):
                hd = 2 * pr + h
                for rb in range(nl):
                    r = slice(rb * BLOCK, (rb + 1) * BLOCK)
                    vis = (rb + 1) * BLOCK
                    u = (s[r, h * tile:h * tile + vis] - ck_all[hd:hd + 1, :vis]) * LOG2E
                    last = u[:, rb * BLOCK:] + mask_ref[...]
                    u = jnp.concatenate([u[:, :rb * BLOCK], last], axis=1) if rb else last
                    u_ref[pr, d, r, h * tile:h * tile + vis] = u
                    top_ref[hd, r, :] = lane_max(top_ref[hd, r, :], u)

    def weighted_values(c, pr, ps):
        rows = _chunk_rows(c, tile)
        rhs = jnp.concatenate([_head_split(v_ref[rows, pr * LANES:(pr + 1) * LANES]), ones_ref[...]],
                              axis=1)
        return jnp.dot(jnp.concatenate(ps, axis=1), rhs, preferred_element_type=F32)

    def diagonal_weigh(d, shifts):
        for pr in range(pairs):
            ps = []
            for h in range(2):
                blocks = []
                for rb in range(nl):
                    r = slice(rb * BLOCK, (rb + 1) * BLOCK)
                    vis = (rb + 1) * BLOCK
                    p = jnp.exp2(u_ref[pr, d, r, h * tile:h * tile + vis] - shifts[2 * pr + h][r, :vis])
                    if vis < tile:
                        p = jnp.concatenate([p, jnp.zeros((BLOCK, tile - vis), F32)], axis=1)
                    blocks.append(p.astype(BF16))
                ps.append(jnp.concatenate(blocks, axis=0))
            acc_ref[pr] = weighted_values(d, pr, ps)

    def weigh(c, shifts):
        for pr in range(pairs):
            ps = [jnp.exp2(u_ref[pr, c, :, h * tile:(h + 1) * tile] - shifts[2 * pr + h]).astype(BF16)
                  for h in range(2)]
            acc_ref[pr] += weighted_values(c, pr, ps)

    def query_tile(d):
        top_ref[...] = jnp.full_like(top_ref, NEG)
        for c in range(d):
            scores(c, 0)
        diagonal_scores(d)
        shifts = []
        for hd in range(2 * pairs):
            cq = cq_all[:, hd:hd + 1] * LOG2E
            m = jnp.max(top_ref[hd], axis=-1, keepdims=True) + cq
            shifts.append(jnp.concatenate([jnp.broadcast_to(m - cq, (tile, LANES))] * nl, axis=1))
        diagonal_weigh(d, shifts)
        for c in range(d):
            weigh(c, shifts)
        for pr in range(pairs):
            acc = acc_ref[pr]
            o_ref[:, pr * LANES:(pr + 1) * LANES] = (
                acc[:, :LANES] / acc[:, LANES:]).astype(o_ref.dtype)

    for d in range(u_ref.shape[1]):
        pl.when(i == d)(functools.partial(query_tile, d))


def _fox_attention(proj, cum, batch, seq):
    tile, groups, heads = FOX_TILING.tile, FOX_TILING.groups, FOX_TILING.heads
    nc = seq // tile
    c = cum.reshape(batch, seq, LANES)[:, :, :N_HEADS]
    cq = c.reshape(batch, seq, groups, heads).transpose(0, 2, 1, 3)
    ck = c.reshape(batch, nc, tile, groups, heads).transpose(0, 1, 3, 4, 2)
    specs = [
        pl.BlockSpec((1, 1, tile, heads), lambda b, g, i: (b, g, i, 0)),
        pl.BlockSpec((1, nc, 1, heads, tile), lambda b, g, i: (b, 0, g, 0, 0)),
    ]
    pairs = FOX_TILING.pairs
    scratch = [pltpu.VMEM((pairs, nc, tile, 2 * tile), F32),
               pltpu.VMEM((pairs, tile, 2 * LANES), F32),
               pltpu.VMEM((heads, tile, LANES), F32),
               pltpu.VMEM((BLOCK, BLOCK), F32),
               pltpu.VMEM((2 * tile, LANES), BF16)]
    return _causal_attention_call(_fox_kernel, "fox_attention", FOX_TILING, proj, [cq, ck], specs,
                                  scratch, batch, seq)


def kernel(x, g_pre, g_post, w_in_a, w_out_a, sinks_a, w_in_b, w_out_b, w_in_c, b_f_c, w_out_c):
    batch, seq, d = x.shape
    depth = g_pre.shape[0]
    xf = x.reshape(batch * seq, d)
    wa, wb, wc = w_in_a.astype(BF16), w_in_b.astype(BF16), w_in_c.astype(BF16)
    for i in range(depth):
        kind, j = i % 3, i // 3
        if kind == 0:
            proj = _norm_proj(xf, g_pre[i], wa, j, BRANCH + 2 * KV_A + BRANCH, BF16, PROJ_TN_A)
            o = _swa_attention(proj, sinks_a[j], batch, seq)
            z_col, w_out = BRANCH + 2 * KV_A, w_out_a[j]
        elif kind == 1:
            proj = _norm_proj(xf, g_pre[i], wb, j, 4 * BRANCH, BF16, PROJ_TN)
            o = _stick_attention(proj, batch, seq)
            z_col, w_out = 3 * BRANCH, w_out_b[j]
        else:
            proj = _norm_proj(xf, g_pre[i], wc, j, 4 * BRANCH, BF16, PROJ_TN)
            w_f = jnp.pad(wc[j, :, 4 * BRANCH:], ((0, 0), (0, LANES - N_HEADS)))[None]
            f_logit = _norm_proj(xf, g_pre[i], w_f, 0, LANES, F32, LANES)
            b_f = jnp.pad(b_f_c[j], (0, LANES - N_HEADS)).reshape(1, LANES)
            cum = _cum_log_forget(f_logit, b_f, batch, seq)
            o = _fox_attention(proj, cum, batch, seq)
            z_col, w_out = 3 * BRANCH, w_out_c[j]
        xf = _gate_out(o, proj, z_col, w_out.astype(BF16), g_post[i], xf)
    return xf.reshape(batch, seq, d)
```

```python
import functools
from typing import NamedTuple

import jax
import jax.numpy as jnp
import numpy as np
from jax import lax
from jax.experimental import pallas as pl
from jax.experimental.pallas import tpu as pltpu

HEAD_DIM = 64
N_HEADS = 32
BRANCH = N_HEADS * HEAD_DIM
N_KV_A = 4
KV_A = N_KV_A * HEAD_DIM
GROUP_A = N_HEADS // N_KV_A
BLOCK = 128
LANES = 128
NORM_EPS = 1e-6
NEG = -1e30
SCALE = HEAD_DIM ** -0.5
LOG2E = 1.4426950408889634
N_PAIRS = N_HEADS // 2

F32 = jnp.float32
BF16 = jnp.bfloat16

VMEM_LIMIT = 52 * 1024 * 1024
PROJ_TM = 1024
PROJ_TN_A, PROJ_TN = 1536, 2048
GATE_TM = 512

_NT = (((1,), (1,)), ((), ()))


def _params(sem):
    return pltpu.CompilerParams(dimension_semantics=sem, vmem_limit_bytes=VMEM_LIMIT)


def _rmsnorm(x, g):
    r = lax.rsqrt(jnp.mean(x * x, axis=-1, keepdims=True) + NORM_EPS)
    return x * r * g


def _norm_proj_kernel(x_ref, g_ref, w_ref, o_ref, h_ref):
    @pl.when(pl.program_id(1) == 0)
    def _():
        h_ref[...] = _rmsnorm(x_ref[...], g_ref[...]).astype(BF16)

    o_ref[...] = jnp.dot(h_ref[...], w_ref[...], preferred_element_type=F32).astype(o_ref.dtype)


def _norm_proj(x, g, w, layer, n, out_dtype, tn, tm=PROJ_TM):
    m, d = x.shape
    return pl.pallas_call(
        _norm_proj_kernel,
        grid=(m // tm, n // tn),
        in_specs=[
            pl.BlockSpec((tm, d), lambda i, j: (i, 0)),
            pl.BlockSpec((1, d), lambda i, j: (0, 0)),
            pl.BlockSpec((None, d, tn), lambda i, j: (layer, 0, j)),
        ],
        out_specs=pl.BlockSpec((tm, tn), lambda i, j: (i, j)),
        out_shape=jax.ShapeDtypeStruct((m, n), out_dtype),
        scratch_shapes=[pltpu.VMEM((tm, d), BF16)],
        compiler_params=_params(("parallel", "arbitrary")),
        name="norm_proj",
    )(x, g.reshape(1, d), w)


Z_SPLIT = 4
Z_BLOCK = BRANCH // Z_SPLIT


def _gate_out_kernel(o_ref, *refs):
    z_refs, (w_ref, g_ref, x_ref, out_ref) = refs[:Z_SPLIT], refs[Z_SPLIT:]
    z = jnp.concatenate([r[...] for r in z_refs], axis=1).astype(F32)
    gated = (o_ref[...].astype(F32) * (z * jax.nn.sigmoid(z))).astype(BF16)
    y = jnp.dot(gated, w_ref[...], preferred_element_type=F32)
    out_ref[...] = x_ref[...] + _rmsnorm(y, g_ref[...])


def _gate_out(o, proj, z_col, w_out, g, x, tm=GATE_TM):
    m, d = x.shape
    z0 = z_col // Z_BLOCK
    z_specs = [pl.BlockSpec((tm, Z_BLOCK), lambda i, c=c: (i, z0 + c)) for c in range(Z_SPLIT)]
    return pl.pallas_call(
        _gate_out_kernel,
        grid=(m // tm,),
        in_specs=[pl.BlockSpec((tm, BRANCH), lambda i: (i, 0))] + z_specs + [
            pl.BlockSpec((BRANCH, d), lambda i: (0, 0)),
            pl.BlockSpec((1, d), lambda i: (0, 0)),
            pl.BlockSpec((tm, d), lambda i: (i, 0)),
        ],
        out_specs=pl.BlockSpec((tm, d), lambda i: (i, 0)),
        out_shape=jax.ShapeDtypeStruct((m, d), F32),
        compiler_params=_params(("parallel",)),
        name="gate_out",
    )(o, *([proj] * Z_SPLIT), w_out, g.reshape(1, d), x)


def _head_split(x):
    lane = lax.broadcasted_iota(jnp.int32, x.shape, 1)
    zero = jnp.zeros_like(x)
    return jnp.concatenate([jnp.where(lane < HEAD_DIM, x, zero),
                            jnp.where(lane >= HEAD_DIM, x, zero)], axis=0)


SWA_BLOCKS = 4


def _alibi_slopes():
    n = N_HEADS
    return (2.0 ** (-8.0 * np.arange(1, n + 1, dtype=np.float32) / n)).astype(np.float32)


def _swa_kernel(sink_ref, q_ref, kp_ref, kc_ref, vp_ref, vc_ref, o_ref, bias_ref):
    qi = lax.broadcasted_iota(jnp.int32, (BLOCK, BLOCK), 0)
    kj = lax.broadcasted_iota(jnp.int32, (BLOCK, BLOCK), 1)
    from_prev = kj > qi
    first = (pl.program_id(0) == 0) & (pl.program_id(1) == 0)

    @pl.when(first)
    def _():
        distf = jnp.where(from_prev, qi + BLOCK - kj, qi - kj).astype(F32)
        slopes = _alibi_slopes()
        for h in range(N_HEADS):
            alibi = -float(slopes[h]) * distf
            bias_ref[1, h] = alibi
            bias_ref[0, h] = jnp.where(from_prev, NEG, alibi)

    low_q = lax.broadcasted_iota(jnp.int32, (BLOCK, LANES), 1) < HEAD_DIM
    src = lax.broadcasted_iota(jnp.int32, (LANES, LANES), 0)
    dst = lax.broadcasted_iota(jnp.int32, (LANES, LANES), 1)
    for sub in range(SWA_BLOCKS):
        rows = slice(sub * BLOCK, (sub + 1) * BLOCK)
        before = slice((sub - 1) * BLOCK, sub * BLOCK)
        has_prev = 1 if sub else jnp.minimum(pl.program_id(1), 1)
        for g in range(N_KV_A):
            p, half = divmod(g, 2)
            cols = slice(p * LANES, (p + 1) * LANES)
            k_prev = kc_ref[before, cols] if sub else kp_ref[:, cols]
            v_prev = vc_ref[before, cols] if sub else vp_ref[:, cols]
            k2 = jnp.concatenate([k_prev, kc_ref[rows, cols]], axis=0)
            v2 = jnp.concatenate([v_prev, vc_ref[rows, cols]], axis=0)
            pick = jnp.where(src == dst % HEAD_DIM + half * HEAD_DIM, 1.0, 0.0).astype(BF16)
            kd = jnp.dot(k2, pick, preferred_element_type=F32).astype(BF16)
            vd = jnp.dot(v2, pick, preferred_element_type=F32).astype(BF16)

            q = q_ref[rows, g * GROUP_A * HEAD_DIM:(g + 1) * GROUP_A * HEAD_DIM] * SCALE
            lhs = jnp.concatenate([_head_split(q[:, i * LANES:(i + 1) * LANES])
                                   for i in range(GROUP_A // 2)], axis=0)
            s = lax.dot_general(lhs, kd, _NT, preferred_element_type=F32)
            ps, ls = [], []
            for hh in range(GROUP_A):
                h = g * GROUP_A + hh
                sb = s[hh * BLOCK:(hh + 1) * BLOCK]
                sh = jnp.where(from_prev, sb[:, :BLOCK], sb[:, BLOCK:]) + bias_ref[has_prev, h]
                sink = sink_ref[h]
                m = jnp.maximum(jnp.max(sh, axis=-1, keepdims=True), sink)
                e = jnp.exp(sh - m)
                ls.append(jnp.sum(e, axis=-1, keepdims=True) + jnp.exp(sink - m))
                ps.append(jnp.concatenate([jnp.where(from_prev, e, 0.0),
                                           jnp.where(from_prev, 0.0, e)], axis=1).astype(BF16))
            pv = jnp.dot(jnp.concatenate(ps, axis=0), vd, preferred_element_type=F32)
            for i in range(GROUP_A // 2):
                o0 = pv[(2 * i) * BLOCK:(2 * i + 1) * BLOCK] / ls[2 * i]
                o1 = pv[(2 * i + 1) * BLOCK:(2 * i + 2) * BLOCK] / ls[2 * i + 1]
                c0 = (g * GROUP_A // 2 + i) * LANES
                o_ref[rows, c0:c0 + LANES] = jnp.where(low_q, o0, o1).astype(o_ref.dtype)


def _swa_attention(proj, sinks, batch, seq):
    m = proj.shape[0]
    step = SWA_BLOCKS * BLOCK
    ns = seq // step
    kcol = BRANCH // KV_A
    vcol = kcol + 1

    def cur(b, n):
        return b * ns + n

    def prev(b, n):
        return b * ns * SWA_BLOCKS + jnp.maximum(n * SWA_BLOCKS - 1, 0)

    return pl.pallas_call(
        _swa_kernel,
        grid=(batch, ns),
        in_specs=[
            pl.BlockSpec(memory_space=pltpu.SMEM),
            pl.BlockSpec((step, BRANCH), lambda b, n: (cur(b, n), 0)),
            pl.BlockSpec((BLOCK, KV_A), lambda b, n: (prev(b, n), kcol)),
            pl.BlockSpec((step, KV_A), lambda b, n: (cur(b, n), kcol)),
            pl.BlockSpec((BLOCK, KV_A), lambda b, n: (prev(b, n), vcol)),
            pl.BlockSpec((step, KV_A), lambda b, n: (cur(b, n), vcol)),
        ],
        out_specs=pl.BlockSpec((step, BRANCH), lambda b, n: (cur(b, n), 0)),
        out_shape=jax.ShapeDtypeStruct((m, BRANCH), BF16),
        scratch_shapes=[pltpu.VMEM((2, N_HEADS, BLOCK, BLOCK), F32)],
        compiler_params=_params(("arbitrary", "arbitrary")),
        name="swa_attention",
    )(sinks, proj, proj, proj, proj, proj)


class _Tiling(NamedTuple):
    tile: int
    pairs: int

    @property
    def groups(self):
        return N_PAIRS // self.pairs

    @property
    def heads(self):
        return 2 * self.pairs


STICK_TILING = _Tiling(tile=256, pairs=8)
FOX_TILING = _Tiling(tile=512, pairs=2)


def _chunk_rows(c, tile):
    if isinstance(c, int):
        return pl.ds(c * tile, tile)
    return pl.ds(pl.multiple_of(c * tile, tile), tile)


def _causal_attention_call(kernel_fn, name, tiling, proj, extra_inputs, extra_specs, scratch,
                           batch, seq):
    m = proj.shape[0]
    tile, groups = tiling.tile, tiling.groups
    nq = seq // tile
    width = tiling.pairs * LANES
    return pl.pallas_call(
        functools.partial(kernel_fn, tiling=tiling),
        grid=(batch, groups, nq),
        in_specs=[
            pl.BlockSpec((tile, width), lambda b, g, i: (b * nq + i, g)),
            pl.BlockSpec((seq, width), lambda b, g, i: (b, groups + g)),
            pl.BlockSpec((seq, width), lambda b, g, i: (b, 2 * groups + g)),
        ] + extra_specs,
        out_specs=pl.BlockSpec((tile, width), lambda b, g, i: (b * nq + i, g)),
        out_shape=jax.ShapeDtypeStruct((m, BRANCH), BF16),
        scratch_shapes=scratch,
        compiler_params=_params(("arbitrary", "arbitrary", "arbitrary")),
        name=name,
    )(proj, proj, proj, *extra_inputs)


def _suffix_matrix():
    j = lax.broadcasted_iota(jnp.int32, (BLOCK, 2 * BLOCK), 0)
    s = lax.broadcasted_iota(jnp.int32, (BLOCK, 2 * BLOCK), 1)
    return jnp.where((s >= BLOCK) | (j > s), 1.0, 0.0).astype(BF16)


SOFTPLUS_CLAMP = 60.0
EXP_UNDERFLOW = 105.0


def _softplus(s):
    return jnp.maximum(s, jnp.log(1.0 + jnp.exp(jnp.minimum(s, SOFTPLUS_CLAMP))))


def _stick_kernel(q_ref, k_ref, v_ref, o_ref, acc_ref, carry_ref, *, tiling):
    tile, pairs = tiling
    nkb = tile // BLOCK
    i = pl.program_id(2)
    u = _suffix_matrix()

    row = lax.broadcasted_iota(jnp.int32, (tile, tile), 0)
    col = lax.broadcasted_iota(jnp.int32, (tile, tile), 1)
    before = jnp.concatenate([col < row, col < row], axis=1)

    def chunk(c, mask):
        rows = _chunk_rows(c, tile)
        for pr in range(pairs):
            cols = slice(pr * LANES, (pr + 1) * LANES)
            q2 = q_ref[:, cols] * SCALE
            s = lax.dot_general(q2, _head_split(k_ref[rows, cols]), _NT,
                                preferred_element_type=F32)
            sp = _softplus(s)
            if mask is not None:
                sp = jnp.where(mask, sp, 0.0)
            spb = sp.astype(BF16)
            base = s - sp
            ps = [None] * (2 * nkb)
            for h in range(2):
                carry = carry_ref[pr, h]
                for kb in reversed(range(nkb)):
                    sl = slice(h * tile + kb * BLOCK, h * tile + (kb + 1) * BLOCK)
                    cs = jnp.dot(spb[:, sl], u, preferred_element_type=F32)
                    ps[h * nkb + kb] = jnp.exp(base[:, sl] - cs[:, :BLOCK] - carry)
                    carry = carry + cs[:, BLOCK:]
                carry_ref[pr, h] = carry
            p = jnp.concatenate(ps, axis=1)
            if mask is not None:
                p = jnp.where(mask, p, 0.0)
            acc_ref[pr] += jnp.dot(p.astype(BF16), _head_split(v_ref[rows, cols]),
                                   preferred_element_type=F32)

    acc_ref[...] = jnp.zeros_like(acc_ref)
    carry_ref[...] = jnp.zeros_like(carry_ref)
    chunk(i, before)

    def weights_vanish():
        return (jnp.min(carry_ref[...]) >= EXP_UNDERFLOW).astype(jnp.int32)

    def cond(st):
        t, done = st
        return jnp.logical_and(t < i, done == 0)

    def body(st):
        t, _ = st
        chunk(i - 1 - t, None)
        return t + 1, weights_vanish()

    lax.while_loop(cond, body, (jnp.int32(0), weights_vanish()))
    for pr in range(pairs):
        o_ref[:, pr * LANES:(pr + 1) * LANES] = acc_ref[pr].astype(o_ref.dtype)


def _stick_attention(proj, batch, seq):
    tile, pairs = STICK_TILING
    scratch = [pltpu.VMEM((pairs, tile, LANES), F32),
               pltpu.VMEM((pairs, 2, tile, LANES), F32)]
    return _causal_attention_call(_stick_kernel, "stick_attention", STICK_TILING, proj, [], [],
                                  scratch, batch, seq)


def _split3(x):
    hi = x.astype(BF16)
    r = x - hi.astype(F32)
    mid = r.astype(BF16)
    lo = (r - mid.astype(F32)).astype(BF16)
    return hi, mid, lo


def _cum_kernel(fl_ref, bf_ref, cum_ref):
    x = fl_ref[...] + bf_ref[...]
    lf = jnp.minimum(x, 0.0) - jnp.log(1.0 + jnp.exp(-jnp.abs(x)))
    t = lax.broadcasted_iota(jnp.int32, (BLOCK, BLOCK), 0)
    j = lax.broadcasted_iota(jnp.int32, (BLOCK, BLOCK), 1)
    tri = jnp.where(j <= t, 1.0, 0.0).astype(BF16)
    parts = jnp.concatenate(_split3(lf), axis=1)
    total = jnp.zeros((1, LANES), F32)
    for blk in range(x.shape[0] // BLOCK):
        rows = slice(blk * BLOCK, (blk + 1) * BLOCK)
        c = jnp.dot(tri, parts[rows], preferred_element_type=F32)
        c = c[:, :LANES] + c[:, LANES:2 * LANES] + c[:, 2 * LANES:] + total
        cum_ref[rows, :] = c
        total = c[BLOCK - 1:BLOCK, :]


def _cum_log_forget(f_logit, b_f, batch, seq):
    return pl.pallas_call(
        _cum_kernel,
        grid=(batch,),
        in_specs=[pl.BlockSpec((seq, LANES), lambda b: (b, 0)),
                  pl.BlockSpec((1, LANES), lambda b: (0, 0))],
        out_specs=pl.BlockSpec((seq, LANES), lambda b: (b, 0)),
        out_shape=jax.ShapeDtypeStruct((batch * seq, LANES), F32),
        compiler_params=_params(("parallel",)),
        name="cum_log_forget",
    )(f_logit, b_f)


def _fox_kernel(q_ref, k_ref, v_ref, cq_ref, ck_ref, o_ref, u_ref, acc_ref, top_ref, mask_ref,
                ones_ref, *, tiling):
    tile, pairs = tiling
    nl = tile // LANES
    i = pl.program_id(2)
    cq_all = cq_ref[0, 0]

    @pl.when((pl.program_id(0) == 0) & (pl.program_id(1) == 0) & (i == 0))
    def _():
        row = lax.broadcasted_iota(jnp.int32, (BLOCK, BLOCK), 0)
        col = lax.broadcasted_iota(jnp.int32, (BLOCK, BLOCK), 1)
        mask_ref[...] = jnp.where(col <= row, 0.0, NEG)
        r2 = lax.broadcasted_iota(jnp.int32, (2 * tile, LANES), 0)
        l2 = lax.broadcasted_iota(jnp.int32, (2 * tile, LANES), 1)
        ones_ref[...] = jnp.where((r2 < tile) == (l2 < HEAD_DIM), 1.0, 0.0).astype(BF16)

    def lane_max(top, u):
        for b in range(u.shape[1] // LANES):
            top = jnp.maximum(top, u[:, b * LANES:(b + 1) * LANES])
        return top

    def pair_scores(c, pr):
        rows = _chunk_rows(c, tile)
        cols = slice(pr * LANES, (pr + 1) * LANES)
        return lax.dot_general(q_ref[:, cols] * SCALE, _head_split(k_ref[rows, cols]), _NT,
                               preferred_element_type=F32)

    def scores(c, _):
        ck_all = ck_ref[0, c, 0]
        for pr in range(pairs):
            s = pair_scores(c, pr)
            for h in range(2):
                hd = 2 * pr + h
                u = (s[:, h * tile:(h + 1) * tile] - ck_all[hd:hd + 1, :]) * LOG2E
                u_ref[pr, c, :, h * tile:(h + 1) * tile] = u
                top_ref[hd] = lane_max(top_ref[hd], u)
        return 0

    def diagonal_scores(d):
        ck_all = ck_ref[0, d, 0]
        for pr in range(pairs):
            s = pair_scores(d, pr)
            for h in range(2):
                hd = 2 * pr + h
                for rb in range(nl):
                    r = slice(rb * BLOCK, (rb + 1) * BLOCK)
                    vis = (rb + 1) * BLOCK
                    u = (s[r, h * tile:h * tile + vis] - ck_all[hd:hd + 1, :vis]) * LOG2E
                    last = u[:, rb * BLOCK:] + mask_ref[...]
                    u = jnp.concatenate([u[:, :rb * BLOCK], last], axis=1) if rb else last
                    u_ref[pr, d, r, h * tile:h * tile + vis] = u
                    top_ref[hd, r, :] = lane_max(top_ref[hd, r, :], u)

    def weighted_values(c, pr, ps):
        rows = _chunk_rows(c, tile)
        rhs = jnp.concatenate([_head_split(v_ref[rows, pr * LANES:(pr + 1) * LANES]), ones_ref[...]],
                              axis=1)
        return jnp.dot(jnp.concatenate(ps, axis=1), rhs, preferred_element_type=F32)

    def diagonal_weigh(d, shifts):
        for pr in range(pairs):
            ps = []
            for h in range(2):
                blocks = []
                for rb in range(nl):
                    r = slice(rb * BLOCK, (rb + 1) * BLOCK)
                    vis = (rb + 1) * BLOCK
                    p = jnp.exp2(u_ref[pr, d, r, h * tile:h * tile + vis] - shifts[2 * pr + h][r, :vis])
                    if vis < tile:
                        p = jnp.concatenate([p, jnp.zeros((BLOCK, tile - vis), F32)], axis=1)
                    blocks.append(p.astype(BF16))
                ps.append(jnp.concatenate(blocks, axis=0))
            acc_ref[pr] = weighted_values(d, pr, ps)

    def weigh(c, shifts):
        for pr in range(pairs):
            ps = [jnp.exp2(u_ref[pr, c, :, h * tile:(h + 1) * tile] - shifts[2 * pr + h]).astype(BF16)
                  for h in range(2)]
            acc_ref[pr] += weighted_values(c, pr, ps)

    def query_tile(d):
        top_ref[...] = jnp.full_like(top_ref, NEG)
        for c in range(d):
            scores(c, 0)
        diagonal_scores(d)
        shifts = []
        for hd in range(2 * pairs):
            cq = cq_all[:, hd:hd + 1] * LOG2E
            m = jnp.max(top_ref[hd], axis=-1, keepdims=True) + cq
            shifts.append(jnp.concatenate([jnp.broadcast_to(m - cq, (tile, LANES))] * nl, axis=1))
        diagonal_weigh(d, shifts)
        for c in range(d):
            weigh(c, shifts)
        for pr in range(pairs):
            acc = acc_ref[pr]
            o_ref[:, pr * LANES:(pr + 1) * LANES] = (
                acc[:, :LANES] / acc[:, LANES:]).astype(o_ref.dtype)

    for d in range(u_ref.shape[1]):
        pl.when(i == d)(functools.partial(query_tile, d))


def _fox_attention(proj, cum, batch, seq):
    tile, groups, heads = FOX_TILING.tile, FOX_TILING.groups, FOX_TILING.heads
    nc = seq // tile
    c = cum.reshape(batch, seq, LANES)[:, :, :N_HEADS]
    cq = c.reshape(batch, seq, groups, heads).transpose(0, 2, 1, 3)
    ck = c.reshape(batch, nc, tile, groups, heads).transpose(0, 1, 3, 4, 2)
    specs = [
        pl.BlockSpec((1, 1, tile, heads), lambda b, g, i: (b, g, i, 0)),
        pl.BlockSpec((1, nc, 1, heads, tile), lambda b, g, i: (b, 0, g, 0, 0)),
    ]
    pairs = FOX_TILING.pairs
    scratch = [pltpu.VMEM((pairs, nc, tile, 2 * tile), F32),
               pltpu.VMEM((pairs, tile, 2 * LANES), F32),
               pltpu.VMEM((heads, tile, LANES), F32),
               pltpu.VMEM((BLOCK, BLOCK), F32),
               pltpu.VMEM((2 * tile, LANES), BF16)]
    return _causal_attention_call(_fox_kernel, "fox_attention", FOX_TILING, proj, [cq, ck], specs,
                                  scratch, batch, seq)


def kernel(x, g_pre, g_post, w_in_a, w_out_a, sinks_a, w_in_b, w_out_b, w_in_c, b_f_c, w_out_c):
    batch, seq, d = x.shape
    depth = g_pre.shape[0]
    xf = x.reshape(batch * seq, d)
    wa, wb, wc = w_in_a.astype(BF16), w_in_b.astype(BF16), w_in_c.astype(BF16)
    for i in range(depth):
        kind, j = i % 3, i // 3
        if kind == 0:
            proj = _norm_proj(xf, g_pre[i], wa, j, BRANCH + 2 * KV_A + BRANCH, BF16, PROJ_TN_A)
            o = _swa_attention(proj, sinks_a[j], batch, seq)
            z_col, w_out = BRANCH + 2 * KV_A, w_out_a[j]
        elif kind == 1:
            proj = _norm_proj(xf, g_pre[i], wb, j, 4 * BRANCH, BF16, PROJ_TN)
            o = _stick_attention(proj, batch, seq)
            z_col, w_out = 3 * BRANCH, w_out_b[j]
        else:
            proj = _norm_proj(xf, g_pre[i], wc, j, 4 * BRANCH, BF16, PROJ_TN)
            w_f = jnp.pad(wc[j, :, 4 * BRANCH:], ((0, 0), (0, LANES - N_HEADS)))[None]
            f_logit = _norm_proj(xf, g_pre[i], w_f, 0, LANES, F32, LANES)
            b_f = jnp.pad(b_f_c[j], (0, LANES - N_HEADS)).reshape(1, LANES)
            cum = _cum_log_forget(f_logit, b_f, batch, seq)
            o = _fox_attention(proj, cum, batch, seq)
            z_col, w_out = 3 * BRANCH, w_out_c[j]
        xf = _gate_out(o, proj, z_col, w_out.astype(BF16), g_post[i], xf)
    return xf.reshape(batch, seq, d)
```

```python
import functools
from typing import NamedTuple

import jax
import jax.numpy as jnp
import numpy as np
from jax import lax
from jax.experimental import pallas as pl
from jax.experimental.pallas import tpu as pltpu

HEAD_DIM = 64
N_HEADS = 32
BRANCH = N_HEADS * HEAD_DIM
N_KV_A = 4
KV_A = N_KV_A * HEAD_DIM
GROUP_A = N_HEADS // N_KV_A
BLOCK = 128
LANES = 128
NORM_EPS = 1e-6
NEG = -1e30
SCALE = HEAD_DIM ** -0.5
LOG2E = 1.4426950408889634
N_PAIRS = N_HEADS // 2

F32 = jnp.float32
BF16 = jnp.bfloat16

VMEM_LIMIT = 52 * 1024 * 1024
PROJ_TM = 1024
PROJ_TN_A, PROJ_TN = 1536, 2048
GATE_TM = 512

_NT = (((1,), (1,)), ((), ()))


def _params(sem):
    return pltpu.CompilerParams(dimension_semantics=sem, vmem_limit_bytes=VMEM_LIMIT)


def _rmsnorm(x, g):
    r = lax.rsqrt(jnp.mean(x * x, axis=-1, keepdims=True) + NORM_EPS)
    return x * r * g


def _norm_proj_kernel(x_ref, g_ref, w_ref, o_ref, h_ref):
    @pl.when(pl.program_id(1) == 0)
    def _():
        h_ref[...] = _rmsnorm(x_ref[...], g_ref[...]).astype(BF16)

    o_ref[...] = jnp.dot(h_ref[...], w_ref[...], preferred_element_type=F32).astype(o_ref.dtype)


def _norm_proj(x, g, w, layer, n, out_dtype, tn, tm=PROJ_TM):
    m, d = x.shape
    return pl.pallas_call(
        _norm_proj_kernel,
        grid=(m // tm, n // tn),
        in_specs=[
            pl.BlockSpec((tm, d), lambda i, j: (i, 0)),
            pl.BlockSpec((1, d), lambda i, j: (0, 0)),
            pl.BlockSpec((None, d, tn), lambda i, j: (layer, 0, j)),
        ],
        out_specs=pl.BlockSpec((tm, tn), lambda i, j: (i, j)),
        out_shape=jax.ShapeDtypeStruct((m, n), out_dtype),
        scratch_shapes=[pltpu.VMEM((tm, d), BF16)],
        compiler_params=_params(("parallel", "arbitrary")),
        name="norm_proj",
    )(x, g.reshape(1, d), w)


Z_SPLIT = 4
Z_BLOCK = BRANCH // Z_SPLIT


def _gate_out_kernel(o_ref, *refs):
    z_refs, (w_ref, g_ref, x_ref, out_ref) = refs[:Z_SPLIT], refs[Z_SPLIT:]
    z = jnp.concatenate([r[...] for r in z_refs], axis=1).astype(F32)
    gated = (o_ref[...].astype(F32) * (z * jax.nn.sigmoid(z))).astype(BF16)
    y = jnp.dot(gated, w_ref[...], preferred_element_type=F32)
    out_ref[...] = x_ref[...] + _rmsnorm(y, g_ref[...])


def _gate_out(o, proj, z_col, w_out, g, x, tm=GATE_TM):
    m, d = x.shape
    z0 = z_col // Z_BLOCK
    z_specs = [pl.BlockSpec((tm, Z_BLOCK), lambda i, c=c: (i, z0 + c)) for c in range(Z_SPLIT)]
    return pl.pallas_call(
        _gate_out_kernel,
        grid=(m // tm,),
        in_specs=[pl.BlockSpec((tm, BRANCH), lambda i: (i, 0))] + z_specs + [
            pl.BlockSpec((BRANCH, d), lambda i: (0, 0)),
            pl.BlockSpec((1, d), lambda i: (0, 0)),
            pl.BlockSpec((tm, d), lambda i: (i, 0)),
        ],
        out_specs=pl.BlockSpec((tm, d), lambda i: (i, 0)),
        out_shape=jax.ShapeDtypeStruct((m, d), F32),
        compiler_params=_params(("parallel",)),
        name="gate_out",
    )(o, *([proj] * Z_SPLIT), w_out, g.reshape(1, d), x)


def _head_split(x):
    lane = lax.broadcasted_iota(jnp.int32, x.shape, 1)
    zero = jnp.zeros_like(x)
    return jnp.concatenate([jnp.where(lane < HEAD_DIM, x, zero),
                            jnp.where(lane >= HEAD_DIM, x, zero)], axis=0)


SWA_BLOCKS = 4


def _alibi_slopes():
    n = N_HEADS
    return (2.0 ** (-8.0 * np.arange(1, n + 1, dtype=np.float32) / n)).astype(np.float32)


def _swa_kernel(sink_ref, q_ref, kp_ref, kc_ref, vp_ref, vc_ref, o_ref, bias_ref):
    qi = lax.broadcasted_iota(jnp.int32, (BLOCK, BLOCK), 0)
    kj = lax.broadcasted_iota(jnp.int32, (BLOCK, BLOCK), 1)
    from_prev = kj > qi
    first = (pl.program_id(0) == 0) & (pl.program_id(1) == 0)

    @pl.when(first)
    def _():
        distf = jnp.where(from_prev, qi + BLOCK - kj, qi - kj).astype(F32)
        slopes = _alibi_slopes()
        for h in range(N_HEADS):
            alibi = -float(slopes[h]) * distf
            bias_ref[1, h] = alibi
            bias_ref[0, h] = jnp.where(from_prev, NEG, alibi)

    low_q = lax.broadcasted_iota(jnp.int32, (BLOCK, LANES), 1) < HEAD_DIM
    src = lax.broadcasted_iota(jnp.int32, (LANES, LANES), 0)
    dst = lax.broadcasted_iota(jnp.int32, (LANES, LANES), 1)
    for sub in range(SWA_BLOCKS):
        rows = slice(sub * BLOCK, (sub + 1) * BLOCK)
        before = slice((sub - 1) * BLOCK, sub * BLOCK)
        has_prev = 1 if sub else jnp.minimum(pl.program_id(1), 1)
        for g in range(N_KV_A):
            p, half = divmod(g, 2)
            cols = slice(p * LANES, (p + 1) * LANES)
            k_prev = kc_ref[before, cols] if sub else kp_ref[:, cols]
            v_prev = vc_ref[before, cols] if sub else vp_ref[:, cols]
            k2 = jnp.concatenate([k_prev, kc_ref[rows, cols]], axis=0)
            v2 = jnp.concatenate([v_prev, vc_ref[rows, cols]], axis=0)
            pick = jnp.where(src == dst % HEAD_DIM + half * HEAD_DIM, 1.0, 0.0).astype(BF16)
            kd = jnp.dot(k2, pick, preferred_element_type=F32).astype(BF16)
            vd = jnp.dot(v2, pick, preferred_element_type=F32).astype(BF16)

            q = q_ref[rows, g * GROUP_A * HEAD_DIM:(g + 1) * GROUP_A * HEAD_DIM] * SCALE
            lhs = jnp.concatenate([_head_split(q[:, i * LANES:(i + 1) * LANES])
                                   for i in range(GROUP_A // 2)], axis=0)
            s = lax.dot_general(lhs, kd, _NT, preferred_element_type=F32)
            ps, ls = [], []
            for hh in range(GROUP_A):
                h = g * GROUP_A + hh
                sb = s[hh * BLOCK:(hh + 1) * BLOCK]
                sh = jnp.where(from_prev, sb[:, :BLOCK], sb[:, BLOCK:]) + bias_ref[has_prev, h]
                sink = sink_ref[h]
                m = jnp.maximum(jnp.max(sh, axis=-1, keepdims=True), sink)
                e = jnp.exp(sh - m)
                ls.append(jnp.sum(e, axis=-1, keepdims=True) + jnp.exp(sink - m))
                ps.append(jnp.concatenate([jnp.where(from_prev, e, 0.0),
                                           jnp.where(from_prev, 0.0, e)], axis=1).astype(BF16))
            pv = jnp.dot(jnp.concatenate(ps, axis=0), vd, preferred_element_type=F32)
            for i in range(GROUP_A // 2):
                o0 = pv[(2 * i) * BLOCK:(2 * i + 1) * BLOCK] / ls[2 * i]
                o1 = pv[(2 * i + 1) * BLOCK:(2 * i + 2) * BLOCK] / ls[2 * i + 1]
                c0 = (g * GROUP_A // 2 + i) * LANES
                o_ref[rows, c0:c0 + LANES] = jnp.where(low_q, o0, o1).astype(o_ref.dtype)


def _swa_attention(proj, sinks, batch, seq):
    m = proj.shape[0]
    step = SWA_BLOCKS * BLOCK
    ns = seq // step
    kcol = BRANCH // KV_A
    vcol = kcol + 1

    def cur(b, n):
        return b * ns + n

    def prev(b, n):
        return b * ns * SWA_BLOCKS + jnp.maximum(n * SWA_BLOCKS - 1, 0)

    return pl.pallas_call(
        _swa_kernel,
        grid=(batch, ns),
        in_specs=[
            pl.BlockSpec(memory_space=pltpu.SMEM),
            pl.BlockSpec((step, BRANCH), lambda b, n: (cur(b, n), 0)),
            pl.BlockSpec((BLOCK, KV_A), lambda b, n: (prev(b, n), kcol)),
            pl.BlockSpec((step, KV_A), lambda b, n: (cur(b, n), kcol)),
            pl.BlockSpec((BLOCK, KV_A), lambda b, n: (prev(b, n), vcol)),
            pl.BlockSpec((step, KV_A), lambda b, n: (cur(b, n), vcol)),
        ],
        out_specs=pl.BlockSpec((step, BRANCH), lambda b, n: (cur(b, n), 0)),
        out_shape=jax.ShapeDtypeStruct((m, BRANCH), BF16),
        scratch_shapes=[pltpu.VMEM((2, N_HEADS, BLOCK, BLOCK), F32)],
        compiler_params=_params(("arbitrary", "arbitrary")),
        name="swa_attention",
    )(sinks, proj, proj, proj, proj, proj)


class _Tiling(NamedTuple):
    tile: int
    pairs: int

    @property
    def groups(self):
        return N_PAIRS // self.pairs

    @property
    def heads(self):
        return 2 * self.pairs


STICK_TILING = _Tiling(tile=256, pairs=16)
FOX_TILING = _Tiling(tile=512, pairs=2)


def _chunk_rows(c, tile):
    if isinstance(c, int):
        return pl.ds(c * tile, tile)
    return pl.ds(pl.multiple_of(c * tile, tile), tile)


def _causal_attention_call(kernel_fn, name, tiling, proj, extra_inputs, extra_specs, scratch,
                           batch, seq):
    m = proj.shape[0]
    tile, groups = tiling.tile, tiling.groups
    nq = seq // tile
    width = tiling.pairs * LANES
    return pl.pallas_call(
        functools.partial(kernel_fn, tiling=tiling),
        grid=(batch, groups, nq),
        in_specs=[
            pl.BlockSpec((tile, width), lambda b, g, i: (b * nq + i, g)),
            pl.BlockSpec((seq, width), lambda b, g, i: (b, groups + g)),
            pl.BlockSpec((seq, width), lambda b, g, i: (b, 2 * groups + g)),
        ] + extra_specs,
        out_specs=pl.BlockSpec((tile, width), lambda b, g, i: (b * nq + i, g)),
        out_shape=jax.ShapeDtypeStruct((m, BRANCH), BF16),
        scratch_shapes=scratch,
        compiler_params=_params(("arbitrary", "arbitrary", "arbitrary")),
        name=name,
    )(proj, proj, proj, *extra_inputs)


def _suffix_matrix():
    j = lax.broadcasted_iota(jnp.int32, (BLOCK, 2 * BLOCK), 0)
    s = lax.broadcasted_iota(jnp.int32, (BLOCK, 2 * BLOCK), 1)
    return jnp.where((s >= BLOCK) | (j > s), 1.0, 0.0).astype(BF16)


SOFTPLUS_CLAMP = 60.0
EXP_UNDERFLOW = 105.0


def _softplus(s):
    return jnp.maximum(s, jnp.log(1.0 + jnp.exp(jnp.minimum(s, SOFTPLUS_CLAMP))))


def _stick_kernel(q_ref, k_ref, v_ref, o_ref, acc_ref, carry_ref, *, tiling):
    tile, pairs = tiling
    nkb = tile // BLOCK
    i = pl.program_id(2)
    u = _suffix_matrix()

    row = lax.broadcasted_iota(jnp.int32, (tile, tile), 0)
    col = lax.broadcasted_iota(jnp.int32, (tile, tile), 1)
    before = jnp.concatenate([col < row, col < row], axis=1)

    def chunk(c, mask):
        rows = _chunk_rows(c, tile)
        for pr in range(pairs):
            cols = slice(pr * LANES, (pr + 1) * LANES)
            q2 = q_ref[:, cols] * SCALE
            s = lax.dot_general(q2, _head_split(k_ref[rows, cols]), _NT,
                                preferred_element_type=F32)
            sp = _softplus(s)
            if mask is not None:
                sp = jnp.where(mask, sp, 0.0)
            spb = sp.astype(BF16)
            base = s - sp
            ps = [None] * (2 * nkb)
            for h in range(2):
                carry = carry_ref[pr, h]
                for kb in reversed(range(nkb)):
                    sl = slice(h * tile + kb * BLOCK, h * tile + (kb + 1) * BLOCK)
                    cs = jnp.dot(spb[:, sl], u, preferred_element_type=F32)
                    ps[h * nkb + kb] = jnp.exp(base[:, sl] - cs[:, :BLOCK] - carry)
                    carry = carry + cs[:, BLOCK:]
                carry_ref[pr, h] = carry
            p = jnp.concatenate(ps, axis=1)
            if mask is not None:
                p = jnp.where(mask, p, 0.0)
            acc_ref[pr] += jnp.dot(p.astype(BF16), _head_split(v_ref[rows, cols]),
                                   preferred_element_type=F32)

    acc_ref[...] = jnp.zeros_like(acc_ref)
    carry_ref[...] = jnp.zeros_like(carry_ref)
    chunk(i, before)

    def weights_vanish():
        return (jnp.min(carry_ref[...]) >= EXP_UNDERFLOW).astype(jnp.int32)

    def cond(st):
        t, done = st
        return jnp.logical_and(t < i, done == 0)

    def body(st):
        t, _ = st
        chunk(i - 1 - t, None)
        return t + 1, weights_vanish()

    lax.while_loop(cond, body, (jnp.int32(0), weights_vanish()))
    for pr in range(pairs):
        o_ref[:, pr * LANES:(pr + 1) * LANES] = acc_ref[pr].astype(o_ref.dtype)


def _stick_attention(proj, batch, seq):
    tile, pairs = STICK_TILING
    scratch = [pltpu.VMEM((pairs, tile, LANES), F32),
               pltpu.VMEM((pairs, 2, tile, LANES), F32)]
    return _causal_attention_call(_stick_kernel, "stick_attention", STICK_TILING, proj, [], [],
                                  scratch, batch, seq)


def _split3(x):
    hi = x.astype(BF16)
    r = x - hi.astype(F32)
    mid = r.astype(BF16)
    lo = (r - mid.astype(F32)).astype(BF16)
    return hi, mid, lo


def _cum_kernel(fl_ref, bf_ref, cum_ref):
    x = fl_ref[...] + bf_ref[...]
    lf = jnp.minimum(x, 0.0) - jnp.log(1.0 + jnp.exp(-jnp.abs(x)))
    t = lax.broadcasted_iota(jnp.int32, (BLOCK, BLOCK), 0)
    j = lax.broadcasted_iota(jnp.int32, (BLOCK, BLOCK), 1)
    tri = jnp.where(j <= t, 1.0, 0.0).astype(BF16)
    parts = jnp.concatenate(_split3(lf), axis=1)
    total = jnp.zeros((1, LANES), F32)
    for blk in range(x.shape[0] // BLOCK):
        rows = slice(blk * BLOCK, (blk + 1) * BLOCK)
        c = jnp.dot(tri, parts[rows], preferred_element_type=F32)
        c = c[:, :LANES] + c[:, LANES:2 * LANES] + c[:, 2 * LANES:] + total
        cum_ref[rows, :] = c
        total = c[BLOCK - 1:BLOCK, :]


def _cum_log_forget(f_logit, b_f, batch, seq):
    return pl.pallas_call(
        _cum_kernel,
        grid=(batch,),
        in_specs=[pl.BlockSpec((seq, LANES), lambda b: (b, 0)),
                  pl.BlockSpec((1, LANES), lambda b: (0, 0))],
        out_specs=pl.BlockSpec((seq, LANES), lambda b: (b, 0)),
        out_shape=jax.ShapeDtypeStruct((batch * seq, LANES), F32),
        compiler_params=_params(("parallel",)),
        name="cum_log_forget",
    )(f_logit, b_f)


def _fox_kernel(q_ref, k_ref, v_ref, cq_ref, ck_ref, o_ref, u_ref, acc_ref, top_ref, mask_ref,
                ones_ref, *, tiling):
    tile, pairs = tiling
    nl = tile // LANES
    i = pl.program_id(2)
    cq_all = cq_ref[0, 0]

    @pl.when((pl.program_id(0) == 0) & (pl.program_id(1) == 0) & (i == 0))
    def _():
        row = lax.broadcasted_iota(jnp.int32, (BLOCK, BLOCK), 0)
        col = lax.broadcasted_iota(jnp.int32, (BLOCK, BLOCK), 1)
        mask_ref[...] = jnp.where(col <= row, 0.0, NEG)
        r2 = lax.broadcasted_iota(jnp.int32, (2 * tile, LANES), 0)
        l2 = lax.broadcasted_iota(jnp.int32, (2 * tile, LANES), 1)
        ones_ref[...] = jnp.where((r2 < tile) == (l2 < HEAD_DIM), 1.0, 0.0).astype(BF16)

    def lane_max(top, u):
        for b in range(u.shape[1] // LANES):
            top = jnp.maximum(top, u[:, b * LANES:(b + 1) * LANES])
        return top

    def pair_scores(c, pr):
        rows = _chunk_rows(c, tile)
        cols = slice(pr * LANES, (pr + 1) * LANES)
        return lax.dot_general(q_ref[:, cols] * SCALE, _head_split(k_ref[rows, cols]), _NT,
                               preferred_element_type=F32)

    def scores(c, _):
        ck_all = ck_ref[0, c, 0]
        for pr in range(pairs):
            s = pair_scores(c, pr)
            for h in range(2):
                hd = 2 * pr + h
                u = (s[:, h * tile:(h + 1) * tile] - ck_all[hd:hd + 1, :]) * LOG2E
                u_ref[pr, c, :, h * tile:(h + 1) * tile] = u
                top_ref[hd] = lane_max(top_ref[hd], u)
        return 0

    def diagonal_scores(d):
        ck_all = ck_ref[0, d, 0]
        for pr in range(pairs):
            s = pair_scores(d, pr)
            for h in range(2):
                hd = 2 * pr + h
                for rb in range(nl):
                    r = slice(rb * BLOCK, (rb + 1) * BLOCK)
                    vis = (rb + 1) * BLOCK
                    u = (s[r, h * tile:h * tile + vis] - ck_all[hd:hd + 1, :vis]) * LOG2E
                    last = u[:, rb * BLOCK:] + mask_ref[...]
                    u = jnp.concatenate([u[:, :rb * BLOCK], last], axis=1) if rb else last
                    u_ref[pr, d, r, h * tile:h * tile + vis] = u
                    top_ref[hd, r, :] = lane_max(top_ref[hd, r, :], u)

    def weighted_values(c, pr, ps):
        rows = _chunk_rows(c, tile)
        rhs = jnp.concatenate([_head_split(v_ref[rows, pr * LANES:(pr + 1) * LANES]), ones_ref[...]],
                              axis=1)
        return jnp.dot(jnp.concatenate(ps, axis=1), rhs, preferred_element_type=F32)

    def diagonal_weigh(d, shifts):
        for pr in range(pairs):
            ps = []
            for h in range(2):
                blocks = []
                for rb in range(nl):
                    r = slice(rb * BLOCK, (rb + 1) * BLOCK)
                    vis = (rb + 1) * BLOCK
                    p = jnp.exp2(u_ref[pr, d, r, h * tile:h * tile + vis] - shifts[2 * pr + h][r, :vis])
                    if vis < tile:
                        p = jnp.concatenate([p, jnp.zeros((BLOCK, tile - vis), F32)], axis=1)
                    blocks.append(p.astype(BF16))
                ps.append(jnp.concatenate(blocks, axis=0))
            acc_ref[pr] = weighted_values(d, pr, ps)

    def weigh(c, shifts):
        for pr in range(pairs):
            ps = [jnp.exp2(u_ref[pr, c, :, h * tile:(h + 1) * tile] - shifts[2 * pr + h]).astype(BF16)
                  for h in range(2)]
            acc_ref[pr] += weighted_values(c, pr, ps)

    def query_tile(d):
        top_ref[...] = jnp.full_like(top_ref, NEG)
        for c in range(d):
            scores(c, 0)
        diagonal_scores(d)
        shifts = []
        for hd in range(2 * pairs):
            cq = cq_all[:, hd:hd + 1] * LOG2E
            m = jnp.max(top_ref[hd], axis=-1, keepdims=True) + cq
            shifts.append(jnp.concatenate([jnp.broadcast_to(m - cq, (tile, LANES))] * nl, axis=1))
        diagonal_weigh(d, shifts)
        for c in range(d):
            weigh(c, shifts)
        for pr in range(pairs):
            acc = acc_ref[pr]
            o_ref[:, pr * LANES:(pr + 1) * LANES] = (
                acc[:, :LANES] / acc[:, LANES:]).astype(o_ref.dtype)

    for d in range(u_ref.shape[1]):
        pl.when(i == d)(functools.partial(query_tile, d))


def _fox_attention(proj, cum, batch, seq):
    tile, groups, heads = FOX_TILING.tile, FOX_TILING.groups, FOX_TILING.heads
    nc = seq // tile
    c = cum.reshape(batch, seq, LANES)[:, :, :N_HEADS]
    cq = c.reshape(batch, seq, groups, heads).transpose(0, 2, 1, 3)
    ck = c.reshape(batch, nc, tile, groups, heads).transpose(0, 1, 3, 4, 2)
    specs = [
        pl.BlockSpec((1, 1, tile, heads), lambda b, g, i: (b, g, i, 0)),
        pl.BlockSpec((1, nc, 1, heads, tile), lambda b, g, i: (b, 0, g, 0, 0)),
    ]
    pairs = FOX_TILING.pairs
    scratch = [pltpu.VMEM((pairs, nc, tile, 2 * tile), F32),
               pltpu.VMEM((pairs, tile, 2 * LANES), F32),
               pltpu.VMEM((heads, tile, LANES), F32),
               pltpu.VMEM((BLOCK, BLOCK), F32),
               pltpu.VMEM((2 * tile, LANES), BF16)]
    return _causal_attention_call(_fox_kernel, "fox_attention", FOX_TILING, proj, [cq, ck], specs,
                                  scratch, batch, seq)


def kernel(x, g_pre, g_post, w_in_a, w_out_a, sinks_a, w_in_b, w_out_b, w_in_c, b_f_c, w_out_c):
    batch, seq, d = x.shape
    depth = g_pre.shape[0]
    xf = x.reshape(batch * seq, d)
    wa, wb, wc = w_in_a.astype(BF16), w_in_b.astype(BF16), w_in_c.astype(BF16)
    for i in range(depth):
        kind, j = i % 3, i // 3
        if kind == 0:
            proj = _norm_proj(xf, g_pre[i], wa, j, BRANCH + 2 * KV_A + BRANCH, BF16, PROJ_TN_A)
            o = _swa_attention(proj, sinks_a[j], batch, seq)
            z_col, w_out = BRANCH + 2 * KV_A, w_out_a[j]
        elif kind == 1:
            proj = _norm_proj(xf, g_pre[i], wb, j, 4 * BRANCH, BF16, PROJ_TN)
            o = _stick_attention(proj, batch, seq)
            z_col, w_out = 3 * BRANCH, w_out_b[j]
        else:
            proj = _norm_proj(xf, g_pre[i], wc, j, 4 * BRANCH, BF16, PROJ_TN)
            w_f = jnp.pad(wc[j, :, 4 * BRANCH:], ((0, 0), (0, LANES - N_HEADS)))[None]
            f_logit = _norm_proj(xf, g_pre[i], w_f, 0, LANES, F32, LANES)
            b_f = jnp.pad(b_f_c[j], (0, LANES - N_HEADS)).reshape(1, LANES)
            cum = _cum_log_forget(f_logit, b_f, batch, seq)
            o = _fox_attention(proj, cum, batch, seq)
            z_col, w_out = 3 * BRANCH, w_out_c[j]
        xf = _gate_out(o, proj, z_col, w_out.astype(BF16), g_post[i], xf)
    return xf.reshape(batch, seq, d)
```

```python
import functools
from typing import NamedTuple

import jax
import jax.numpy as jnp
import numpy as np
from jax import lax
from jax.experimental import pallas as pl
from jax.experimental.pallas import tpu as pltpu

HEAD_DIM = 64
N_HEADS = 32
BRANCH = N_HEADS * HEAD_DIM
N_KV_A = 4
KV_A = N_KV_A * HEAD_DIM
GROUP_A = N_HEADS // N_KV_A
BLOCK = 128
LANES = 128
NORM_EPS = 1e-6
NEG = -1e30
SCALE = HEAD_DIM ** -0.5
LOG2E = 1.4426950408889634
N_PAIRS = N_HEADS // 2

F32 = jnp.float32
BF16 = jnp.bfloat16

VMEM_LIMIT = 52 * 1024 * 1024
PROJ_TM = 1024
PROJ_TN_A, PROJ_TN = 1536, 2048
GATE_TM = 512

_NT = (((1,), (1,)), ((), ()))


def _params(sem):
    return pltpu.CompilerParams(dimension_semantics=sem, vmem_limit_bytes=VMEM_LIMIT)


def _rmsnorm(x, g):
    r = lax.rsqrt(jnp.mean(x * x, axis=-1, keepdims=True) + NORM_EPS)
    return x * r * g


def _norm_proj_kernel(x_ref, g_ref, w_ref, *refs, narrow):
    o_ref, h_ref = refs[-3 if narrow else -2], refs[-1]

    @pl.when(pl.program_id(1) == 0)
    def _():
        h_ref[...] = _rmsnorm(x_ref[...], g_ref[...]).astype(BF16)
        if narrow:
            wn_ref, on_ref = refs[0], refs[2]
            on_ref[...] = jnp.dot(h_ref[...], wn_ref[...], preferred_element_type=F32)

    o_ref[...] = jnp.dot(h_ref[...], w_ref[...], preferred_element_type=F32).astype(o_ref.dtype)


def _norm_proj(x, g, w, layer, n, tn, w_narrow=None, tm=PROJ_TM):
    m, d = x.shape
    narrow = w_narrow is not None
    row = pl.BlockSpec((tm, LANES), lambda i, j: (i, 0))
    outs = pl.pallas_call(
        functools.partial(_norm_proj_kernel, narrow=narrow),
        grid=(m // tm, n // tn),
        in_specs=[
            pl.BlockSpec((tm, d), lambda i, j: (i, 0)),
            pl.BlockSpec((1, d), lambda i, j: (0, 0)),
            pl.BlockSpec((None, d, tn), lambda i, j: (layer, 0, j)),
        ] + [pl.BlockSpec((d, LANES), lambda i, j: (0, 0))] * narrow,
        out_specs=[pl.BlockSpec((tm, tn), lambda i, j: (i, j))] + [row] * narrow,
        out_shape=[jax.ShapeDtypeStruct((m, n), BF16)] + [jax.ShapeDtypeStruct((m, LANES), F32)] * narrow,
        scratch_shapes=[pltpu.VMEM((tm, d), BF16)],
        compiler_params=_params(("parallel", "arbitrary")),
        name="norm_proj",
    )(x, g.reshape(1, d), w, *([w_narrow] if narrow else []))
    return tuple(outs) if narrow else outs[0]


Z_SPLIT = 4
Z_BLOCK = BRANCH // Z_SPLIT


def _gate_out_kernel(o_ref, *refs):
    z_refs, (w_ref, g_ref, x_ref, out_ref) = refs[:Z_SPLIT], refs[Z_SPLIT:]
    z = jnp.concatenate([r[...] for r in z_refs], axis=1).astype(F32)
    gated = (o_ref[...].astype(F32) * (z * jax.nn.sigmoid(z))).astype(BF16)
    y = jnp.dot(gated, w_ref[...], preferred_element_type=F32)
    out_ref[...] = x_ref[...] + _rmsnorm(y, g_ref[...])


def _gate_out(o, proj, z_col, w_out, g, x, tm=GATE_TM):
    m, d = x.shape
    z0 = z_col // Z_BLOCK
    z_specs = [pl.BlockSpec((tm, Z_BLOCK), lambda i, c=c: (i, z0 + c)) for c in range(Z_SPLIT)]
    return pl.pallas_call(
        _gate_out_kernel,
        grid=(m // tm,),
        in_specs=[pl.BlockSpec((tm, BRANCH), lambda i: (i, 0))] + z_specs + [
            pl.BlockSpec((BRANCH, d), lambda i: (0, 0)),
            pl.BlockSpec((1, d), lambda i: (0, 0)),
            pl.BlockSpec((tm, d), lambda i: (i, 0)),
        ],
        out_specs=pl.BlockSpec((tm, d), lambda i: (i, 0)),
        out_shape=jax.ShapeDtypeStruct((m, d), F32),
        compiler_params=_params(("parallel",)),
        name="gate_out",
    )(o, *([proj] * Z_SPLIT), w_out, g.reshape(1, d), x)


def _head_split(x):
    lane = lax.broadcasted_iota(jnp.int32, x.shape, 1)
    zero = jnp.zeros_like(x)
    return jnp.concatenate([jnp.where(lane < HEAD_DIM, x, zero),
                            jnp.where(lane >= HEAD_DIM, x, zero)], axis=0)


SWA_BLOCKS = 4


def _alibi_slopes():
    n = N_HEADS
    return (2.0 ** (-8.0 * np.arange(1, n + 1, dtype=np.float32) / n)).astype(np.float32)


def _swa_kernel(sink_ref, q_ref, kp_ref, kc_ref, vp_ref, vc_ref, o_ref, bias_ref):
    qi = lax.broadcasted_iota(jnp.int32, (BLOCK, BLOCK), 0)
    kj = lax.broadcasted_iota(jnp.int32, (BLOCK, BLOCK), 1)
    from_prev = kj > qi
    first = (pl.program_id(0) == 0) & (pl.program_id(1) == 0)

    @pl.when(first)
    def _():
        distf = jnp.where(from_prev, qi + BLOCK - kj, qi - kj).astype(F32)
        slopes = _alibi_slopes()
        for h in range(N_HEADS):
            alibi = -float(slopes[h]) * distf
            bias_ref[1, h] = alibi
            bias_ref[0, h] = jnp.where(from_prev, NEG, alibi)

    low_q = lax.broadcasted_iota(jnp.int32, (BLOCK, LANES), 1) < HEAD_DIM
    src = lax.broadcasted_iota(jnp.int32, (LANES, LANES), 0)
    dst = lax.broadcasted_iota(jnp.int32, (LANES, LANES), 1)
    for sub in range(SWA_BLOCKS):
        rows = slice(sub * BLOCK, (sub + 1) * BLOCK)
        before = slice((sub - 1) * BLOCK, sub * BLOCK)
        has_prev = 1 if sub else jnp.minimum(pl.program_id(1), 1)
        for g in range(N_KV_A):
            p, half = divmod(g, 2)
            cols = slice(p * LANES, (p + 1) * LANES)
            k_prev = kc_ref[before, cols] if sub else kp_ref[:, cols]
            v_prev = vc_ref[before, cols] if sub else vp_ref[:, cols]
            k2 = jnp.concatenate([k_prev, kc_ref[rows, cols]], axis=0)
            v2 = jnp.concatenate([v_prev, vc_ref[rows, cols]], axis=0)
            pick = jnp.where(src == dst % HEAD_DIM + half * HEAD_DIM, 1.0, 0.0).astype(BF16)
            kd = jnp.dot(k2, pick, preferred_element_type=F32).astype(BF16)
            vd = jnp.dot(v2, pick, preferred_element_type=F32).astype(BF16)

            q = q_ref[rows, g * GROUP_A * HEAD_DIM:(g + 1) * GROUP_A * HEAD_DIM] * SCALE
            lhs = jnp.concatenate([_head_split(q[:, i * LANES:(i + 1) * LANES])
                                   for i in range(GROUP_A // 2)], axis=0)
            s = lax.dot_general(lhs, kd, _NT, preferred_element_type=F32)
            ps, ls = [], []
            for hh in range(GROUP_A):
                h = g * GROUP_A + hh
                sb = s[hh * BLOCK:(hh + 1) * BLOCK]
                sh = jnp.where(from_prev, sb[:, :BLOCK], sb[:, BLOCK:]) + bias_ref[has_prev, h]
                sink = sink_ref[h]
                m = jnp.maximum(jnp.max(sh, axis=-1, keepdims=True), sink)
                e = jnp.exp(sh - m)
                ls.append(jnp.sum(e, axis=-1, keepdims=True) + jnp.exp(sink - m))
                ps.append(jnp.concatenate([jnp.where(from_prev, e, 0.0),
                                           jnp.where(from_prev, 0.0, e)], axis=1).astype(BF16))
            pv = jnp.dot(jnp.concatenate(ps, axis=0), vd, preferred_element_type=F32)
            for i in range(GROUP_A // 2):
                o0 = pv[(2 * i) * BLOCK:(2 * i + 1) * BLOCK] / ls[2 * i]
                o1 = pv[(2 * i + 1) * BLOCK:(2 * i + 2) * BLOCK] / ls[2 * i + 1]
                c0 = (g * GROUP_A // 2 + i) * LANES
                o_ref[rows, c0:c0 + LANES] = jnp.where(low_q, o0, o1).astype(o_ref.dtype)


def _swa_attention(proj, sinks, batch, seq):
    m = proj.shape[0]
    step = SWA_BLOCKS * BLOCK
    ns = seq // step
    kcol = BRANCH // KV_A
    vcol = kcol + 1

    def cur(b, n):
        return b * ns + n

    def prev(b, n):
        return b * ns * SWA_BLOCKS + jnp.maximum(n * SWA_BLOCKS - 1, 0)

    return pl.pallas_call(
        _swa_kernel,
        grid=(batch, ns),
        in_specs=[
            pl.BlockSpec(memory_space=pltpu.SMEM),
            pl.BlockSpec((step, BRANCH), lambda b, n: (cur(b, n), 0)),
            pl.BlockSpec((BLOCK, KV_A), lambda b, n: (prev(b, n), kcol)),
            pl.BlockSpec((step, KV_A), lambda b, n: (cur(b, n), kcol)),
            pl.BlockSpec((BLOCK, KV_A), lambda b, n: (prev(b, n), vcol)),
            pl.BlockSpec((step, KV_A), lambda b, n: (cur(b, n), vcol)),
        ],
        out_specs=pl.BlockSpec((step, BRANCH), lambda b, n: (cur(b, n), 0)),
        out_shape=jax.ShapeDtypeStruct((m, BRANCH), BF16),
        scratch_shapes=[pltpu.VMEM((2, N_HEADS, BLOCK, BLOCK), F32)],
        compiler_params=_params(("arbitrary", "arbitrary")),
        name="swa_attention",
    )(sinks, proj, proj, proj, proj, proj)


class _Tiling(NamedTuple):
    tile: int
    pairs: int

    @property
    def groups(self):
        return N_PAIRS // self.pairs

    @property
    def heads(self):
        return 2 * self.pairs


STICK_TILING = _Tiling(tile=256, pairs=16)
FOX_TILING = _Tiling(tile=512, pairs=2)


def _chunk_rows(c, tile):
    if isinstance(c, int):
        return pl.ds(c * tile, tile)
    return pl.ds(pl.multiple_of(c * tile, tile), tile)


def _causal_attention_call(kernel_fn, name, tiling, proj, extra_inputs, extra_specs, scratch,
                           batch, seq):
    m = proj.shape[0]
    tile, groups = tiling.tile, tiling.groups
    nq = seq // tile
    width = tiling.pairs * LANES
    return pl.pallas_call(
        functools.partial(kernel_fn, tiling=tiling),
        grid=(batch, groups, nq),
        in_specs=[
            pl.BlockSpec((tile, width), lambda b, g, i: (b * nq + i, g)),
            pl.BlockSpec((seq, width), lambda b, g, i: (b, groups + g)),
            pl.BlockSpec((seq, width), lambda b, g, i: (b, 2 * groups + g)),
        ] + extra_specs,
        out_specs=pl.BlockSpec((tile, width), lambda b, g, i: (b * nq + i, g)),
        out_shape=jax.ShapeDtypeStruct((m, BRANCH), BF16),
        scratch_shapes=scratch,
        compiler_params=_params(("arbitrary", "arbitrary", "arbitrary")),
        name=name,
    )(proj, proj, proj, *extra_inputs)


def _suffix_matrix():
    j = lax.broadcasted_iota(jnp.int32, (BLOCK, 2 * BLOCK), 0)
    s = lax.broadcasted_iota(jnp.int32, (BLOCK, 2 * BLOCK), 1)
    return jnp.where((s >= BLOCK) | (j > s), 1.0, 0.0).astype(BF16)


SOFTPLUS_CLAMP = 60.0
EXP_UNDERFLOW = 105.0


def _softplus(s):
    return jnp.maximum(s, jnp.log(1.0 + jnp.exp(jnp.minimum(s, SOFTPLUS_CLAMP))))


def _stick_kernel(q_ref, k_ref, v_ref, o_ref, acc_ref, carry_ref, *, tiling):
    tile, pairs = tiling
    nkb = tile // BLOCK
    i = pl.program_id(2)
    u = _suffix_matrix()

    row = lax.broadcasted_iota(jnp.int32, (tile, tile), 0)
    col = lax.broadcasted_iota(jnp.int32, (tile, tile), 1)
    before = jnp.concatenate([col < row, col < row], axis=1)

    def chunk(c, mask):
        rows = _chunk_rows(c, tile)
        for pr in range(pairs):
            cols = slice(pr * LANES, (pr + 1) * LANES)
            q2 = q_ref[:, cols] * SCALE
            s = lax.dot_general(q2, _head_split(k_ref[rows, cols]), _NT,
                                preferred_element_type=F32)
            sp = _softplus(s)
            if mask is not None:
                sp = jnp.where(mask, sp, 0.0)
            spb = sp.astype(BF16)
            base = s - sp
            ps = [None] * (2 * nkb)
            for h in range(2):
                carry = carry_ref[pr, h]
                for kb in reversed(range(nkb)):
                    sl = slice(h * tile + kb * BLOCK, h * tile + (kb + 1) * BLOCK)
                    cs = jnp.dot(spb[:, sl], u, preferred_element_type=F32)
                    ps[h * nkb + kb] = jnp.exp(base[:, sl] - cs[:, :BLOCK] - carry)
                    carry = carry + cs[:, BLOCK:]
                carry_ref[pr, h] = carry
            p = jnp.concatenate(ps, axis=1)
            if mask is not None:
                p = jnp.where(mask, p, 0.0)
            acc_ref[pr] += jnp.dot(p.astype(BF16), _head_split(v_ref[rows, cols]),
                                   preferred_element_type=F32)

    acc_ref[...] = jnp.zeros_like(acc_ref)
    carry_ref[...] = jnp.zeros_like(carry_ref)
    chunk(i, before)

    def weights_vanish():
        return (jnp.min(carry_ref[...]) >= EXP_UNDERFLOW).astype(jnp.int32)

    def cond(st):
        t, done = st
        return jnp.logical_and(t < i, done == 0)

    def body(st):
        t, _ = st
        chunk(i - 1 - t, None)
        return t + 1, weights_vanish()

    lax.while_loop(cond, body, (jnp.int32(0), weights_vanish()))
    for pr in range(pairs):
        o_ref[:, pr * LANES:(pr + 1) * LANES] = acc_ref[pr].astype(o_ref.dtype)


def _stick_attention(proj, batch, seq):
    tile, pairs = STICK_TILING
    scratch = [pltpu.VMEM((pairs, tile, LANES), F32),
               pltpu.VMEM((pairs, 2, tile, LANES), F32)]
    return _causal_attention_call(_stick_kernel, "stick_attention", STICK_TILING, proj, [], [],
                                  scratch, batch, seq)


def _split3(x):
    hi = x.astype(BF16)
    r = x - hi.astype(F32)
    mid = r.astype(BF16)
    lo = (r - mid.astype(F32)).astype(BF16)
    return hi, mid, lo


def _cum_kernel(fl_ref, bf_ref, cum_ref):
    x = fl_ref[...] + bf_ref[...]
    lf = jnp.minimum(x, 0.0) - jnp.log(1.0 + jnp.exp(-jnp.abs(x)))
    t = lax.broadcasted_iota(jnp.int32, (BLOCK, BLOCK), 0)
    j = lax.broadcasted_iota(jnp.int32, (BLOCK, BLOCK), 1)
    tri = jnp.where(j <= t, 1.0, 0.0).astype(BF16)
    parts = jnp.concatenate(_split3(lf), axis=1)
    total = jnp.zeros((1, LANES), F32)
    for blk in range(x.shape[0] // BLOCK):
        rows = slice(blk * BLOCK, (blk + 1) * BLOCK)
        c = jnp.dot(tri, parts[rows], preferred_element_type=F32)
        c = c[:, :LANES] + c[:, LANES:2 * LANES] + c[:, 2 * LANES:] + total
        cum_ref[rows, :] = c
        total = c[BLOCK - 1:BLOCK, :]


def _cum_log_forget(f_logit, b_f, batch, seq):
    return pl.pallas_call(
        _cum_kernel,
        grid=(batch,),
        in_specs=[pl.BlockSpec((seq, LANES), lambda b: (b, 0)),
                  pl.BlockSpec((1, LANES), lambda b: (0, 0))],
        out_specs=pl.BlockSpec((seq, LANES), lambda b: (b, 0)),
        out_shape=jax.ShapeDtypeStruct((batch * seq, LANES), F32),
        compiler_params=_params(("parallel",)),
        name="cum_log_forget",
    )(f_logit, b_f)


def _fox_kernel(q_ref, k_ref, v_ref, cq_ref, ck_ref, o_ref, u_ref, acc_ref, top_ref, mask_ref,
                ones_ref, *, tiling):
    tile, pairs = tiling
    nl = tile // LANES
    i = pl.program_id(2)
    cq_all = cq_ref[0, 0]

    @pl.when((pl.program_id(0) == 0) & (pl.program_id(1) == 0) & (i == 0))
    def _():
        row = lax.broadcasted_iota(jnp.int32, (BLOCK, BLOCK), 0)
        col = lax.broadcasted_iota(jnp.int32, (BLOCK, BLOCK), 1)
        mask_ref[...] = jnp.where(col <= row, 0.0, NEG)
        r2 = lax.broadcasted_iota(jnp.int32, (2 * tile, LANES), 0)
        l2 = lax.broadcasted_iota(jnp.int32, (2 * tile, LANES), 1)
        ones_ref[...] = jnp.where((r2 < tile) == (l2 < HEAD_DIM), 1.0, 0.0).astype(BF16)

    def lane_max(top, u):
        for b in range(u.shape[1] // LANES):
            top = jnp.maximum(top, u[:, b * LANES:(b + 1) * LANES])
        return top

    def pair_scores(c, pr):
        rows = _chunk_rows(c, tile)
        cols = slice(pr * LANES, (pr + 1) * LANES)
        return lax.dot_general(q_ref[:, cols] * SCALE, _head_split(k_ref[rows, cols]), _NT,
                               preferred_element_type=F32)

    def scores(c, _):
        ck_all = ck_ref[0, c, 0]
        for pr in range(pairs):
            s = pair_scores(c, pr)
            for h in range(2):
                hd = 2 * pr + h
                u = (s[:, h * tile:(h + 1) * tile] - ck_all[hd:hd + 1, :]) * LOG2E
                u_ref[pr, c, :, h * tile:(h + 1) * tile] = u
                top_ref[hd] = lane_max(top_ref[hd], u)
        return 0

    def diagonal_scores(d):
        ck_all = ck_ref[0, d, 0]
        for pr in range(pairs):
            s = pair_scores(d, pr)
            for h in range(2):
                hd = 2 * pr + h
                for rb in range(nl):
                    r = slice(rb * BLOCK, (rb + 1) * BLOCK)
                    vis = (rb + 1) * BLOCK
                    u = (s[r, h * tile:h * tile + vis] - ck_all[hd:hd + 1, :vis]) * LOG2E
                    last = u[:, rb * BLOCK:] + mask_ref[...]
                    u = jnp.concatenate([u[:, :rb * BLOCK], last], axis=1) if rb else last
                    u_ref[pr, d, r, h * tile:h * tile + vis] = u
                    top_ref[hd, r, :] = lane_max(top_ref[hd, r, :], u)

    def weighted_values(c, pr, ps):
        rows = _chunk_rows(c, tile)
        rhs = jnp.concatenate([_head_split(v_ref[rows, pr * LANES:(pr + 1) * LANES]), ones_ref[...]],
                              axis=1)
        return jnp.dot(jnp.concatenate(ps, axis=1), rhs, preferred_element_type=F32)

    def diagonal_weigh(d, shifts):
        for pr in range(pairs):
            ps = []
            for h in range(2):
                blocks = []
                for rb in range(nl):
                    r = slice(rb * BLOCK, (rb + 1) * BLOCK)
                    vis = (rb + 1) * BLOCK
                    p = jnp.exp2(u_ref[pr, d, r, h * tile:h * tile + vis] - shifts[2 * pr + h][r, :vis])
                    if vis < tile:
                        p = jnp.concatenate([p, jnp.zeros((BLOCK, tile - vis), F32)], axis=1)
                    blocks.append(p.astype(BF16))
                ps.append(jnp.concatenate(blocks, axis=0))
            acc_ref[pr] = weighted_values(d, pr, ps)

    def weigh(c, shifts):
        for pr in range(pairs):
            ps = [jnp.exp2(u_ref[pr, c, :, h * tile:(h + 1) * tile] - shifts[2 * pr + h]).astype(BF16)
                  for h in range(2)]
            acc_ref[pr] += weighted_values(c, pr, ps)

    def query_tile(d):
        top_ref[...] = jnp.full_like(top_ref, NEG)
        for c in range(d):
            scores(c, 0)
        diagonal_scores(d)
        shifts = []
        for hd in range(2 * pairs):
            cq = cq_all[:, hd:hd + 1] * LOG2E
            m = jnp.max(top_ref[hd], axis=-1, keepdims=True) + cq
            shifts.append(jnp.concatenate([jnp.broadcast_to(m - cq, (tile, LANES))] * nl, axis=1))
        diagonal_weigh(d, shifts)
        for c in range(d):
            weigh(c, shifts)
        for pr in range(pairs):
            acc = acc_ref[pr]
            o_ref[:, pr * LANES:(pr + 1) * LANES] = (
                acc[:, :LANES] / acc[:, LANES:]).astype(o_ref.dtype)

    for d in range(u_ref.shape[1]):
        pl.when(i == d)(functools.partial(query_tile, d))


def _fox_attention(proj, cum, batch, seq):
    tile, groups, heads = FOX_TILING.tile, FOX_TILING.groups, FOX_TILING.heads
    nc = seq // tile
    c = cum.reshape(batch, seq, LANES)[:, :, :N_HEADS]
    cq = c.reshape(batch, seq, groups, heads).transpose(0, 2, 1, 3)
    ck = c.reshape(batch, nc, tile, groups, heads).transpose(0, 1, 3, 4, 2)
    specs = [
        pl.BlockSpec((1, 1, tile, heads), lambda b, g, i: (b, g, i, 0)),
        pl.BlockSpec((1, nc, 1, heads, tile), lambda b, g, i: (b, 0, g, 0, 0)),
    ]
    pairs = FOX_TILING.pairs
    scratch = [pltpu.VMEM((pairs, nc, tile, 2 * tile), F32),
               pltpu.VMEM((pairs, tile, 2 * LANES), F32),
               pltpu.VMEM((heads, tile, LANES), F32),
               pltpu.VMEM((BLOCK, BLOCK), F32),
               pltpu.VMEM((2 * tile, LANES), BF16)]
    return _causal_attention_call(_fox_kernel, "fox_attention", FOX_TILING, proj, [cq, ck], specs,
                                  scratch, batch, seq)


def kernel(x, g_pre, g_post, w_in_a, w_out_a, sinks_a, w_in_b, w_out_b, w_in_c, b_f_c, w_out_c):
    batch, seq, d = x.shape
    depth = g_pre.shape[0]
    xf = x.reshape(batch * seq, d)
    wa, wb, wc = w_in_a.astype(BF16), w_in_b.astype(BF16), w_in_c.astype(BF16)
    for i in range(depth):
        kind, j = i % 3, i // 3
        if kind == 0:
            proj = _norm_proj(xf, g_pre[i], wa, j, BRANCH + 2 * KV_A + BRANCH, PROJ_TN_A)
            o = _swa_attention(proj, sinks_a[j], batch, seq)
            z_col, w_out = BRANCH + 2 * KV_A, w_out_a[j]
        elif kind == 1:
            proj = _norm_proj(xf, g_pre[i], wb, j, 4 * BRANCH, PROJ_TN)
            o = _stick_attention(proj, batch, seq)
            z_col, w_out = 3 * BRANCH, w_out_b[j]
        else:
            w_f = jnp.pad(wc[j, :, 4 * BRANCH:], ((0, 0), (0, LANES - N_HEADS)))
            proj, f_logit = _norm_proj(xf, g_pre[i], wc, j, 4 * BRANCH, PROJ_TN, w_narrow=w_f)
            b_f = jnp.pad(b_f_c[j], (0, LANES - N_HEADS)).reshape(1, LANES)
            cum = _cum_log_forget(f_logit, b_f, batch, seq)
            o = _fox_attention(proj, cum, batch, seq)
            z_col, w_out = 3 * BRANCH, w_out_c[j]
        xf = _gate_out(o, proj, z_col, w_out.astype(BF16), g_post[i], xf)
    return xf.reshape(batch, seq, d)
```

```python
import functools
from typing import NamedTuple

import jax
import jax.numpy as jnp
import numpy as np
from jax import lax
from jax.experimental import pallas as pl
from jax.experimental.pallas import tpu as pltpu

HEAD_DIM = 64
N_HEADS = 32
BRANCH = N_HEADS * HEAD_DIM
N_KV_A = 4
KV_A = N_KV_A * HEAD_DIM
GROUP_A = N_HEADS // N_KV_A
BLOCK = 128
LANES = 128
NORM_EPS = 1e-6
NEG = -1e30
SCALE = HEAD_DIM ** -0.5
LOG2E = 1.4426950408889634
N_PAIRS = N_HEADS // 2

F32 = jnp.float32
BF16 = jnp.bfloat16

VMEM_LIMIT = 52 * 1024 * 1024
PROJ_TM = 1024
PROJ_TN_A, PROJ_TN = 1536, 2048
GATE_TM = 512

_NT = (((1,), (1,)), ((), ()))


def _params(sem):
    return pltpu.CompilerParams(dimension_semantics=sem, vmem_limit_bytes=VMEM_LIMIT)


def _rmsnorm(x, g):
    r = lax.rsqrt(jnp.mean(x * x, axis=-1, keepdims=True) + NORM_EPS)
    return x * r * g


def _norm_proj_kernel(x_ref, g_ref, w_ref, *refs, narrow):
    o_ref, h_ref = refs[-3 if narrow else -2], refs[-1]

    @pl.when(pl.program_id(1) == 0)
    def _():
        h_ref[...] = _rmsnorm(x_ref[...], g_ref[...]).astype(BF16)
        if narrow:
            wn_ref, on_ref = refs[0], refs[2]
            on_ref[...] = jnp.dot(h_ref[...], wn_ref[...], preferred_element_type=F32)

    o_ref[...] = jnp.dot(h_ref[...], w_ref[...], preferred_element_type=F32).astype(o_ref.dtype)


def _norm_proj(x, g, w, layer, n, tn, w_narrow=None, tm=PROJ_TM):
    m, d = x.shape
    narrow = w_narrow is not None
    row = pl.BlockSpec((tm, LANES), lambda i, j: (i, 0))
    outs = pl.pallas_call(
        functools.partial(_norm_proj_kernel, narrow=narrow),
        grid=(m // tm, n // tn),
        in_specs=[
            pl.BlockSpec((tm, d), lambda i, j: (i, 0)),
            pl.BlockSpec((1, d), lambda i, j: (0, 0)),
            pl.BlockSpec((None, d, tn), lambda i, j: (layer, 0, j)),
        ] + [pl.BlockSpec((d, LANES), lambda i, j: (0, 0))] * narrow,
        out_specs=[pl.BlockSpec((tm, tn), lambda i, j: (i, j))] + [row] * narrow,
        out_shape=[jax.ShapeDtypeStruct((m, n), BF16)] + [jax.ShapeDtypeStruct((m, LANES), F32)] * narrow,
        scratch_shapes=[pltpu.VMEM((tm, d), BF16)],
        compiler_params=_params(("parallel", "arbitrary")),
        name="norm_proj",
    )(x, g.reshape(1, d), w, *([w_narrow] if narrow else []))
    return tuple(outs) if narrow else outs[0]


Z_SPLIT = 4
Z_BLOCK = BRANCH // Z_SPLIT


def _gate_out_kernel(o_ref, *refs):
    z_refs, (w_ref, g_ref, x_ref, out_ref) = refs[:Z_SPLIT], refs[Z_SPLIT:]
    z = jnp.concatenate([r[...] for r in z_refs], axis=1).astype(F32)
    gated = (o_ref[...].astype(F32) * (z * jax.nn.sigmoid(z))).astype(BF16)
    y = jnp.dot(gated, w_ref[...], preferred_element_type=F32)
    out_ref[...] = x_ref[...] + _rmsnorm(y, g_ref[...])


def _gate_out(o, proj, z_col, w_out, g, x, tm=GATE_TM):
    m, d = x.shape
    z0 = z_col // Z_BLOCK
    z_specs = [pl.BlockSpec((tm, Z_BLOCK), lambda i, c=c: (i, z0 + c)) for c in range(Z_SPLIT)]
    return pl.pallas_call(
        _gate_out_kernel,
        grid=(m // tm,),
        in_specs=[pl.BlockSpec((tm, BRANCH), lambda i: (i, 0))] + z_specs + [
            pl.BlockSpec((BRANCH, d), lambda i: (0, 0)),
            pl.BlockSpec((1, d), lambda i: (0, 0)),
            pl.BlockSpec((tm, d), lambda i: (i, 0)),
        ],
        out_specs=pl.BlockSpec((tm, d), lambda i: (i, 0)),
        out_shape=jax.ShapeDtypeStruct((m, d), F32),
        compiler_params=_params(("parallel",)),
        name="gate_out",
    )(o, *([proj] * Z_SPLIT), w_out, g.reshape(1, d), x)


def _head_split(x):
    lane = lax.broadcasted_iota(jnp.int32, x.shape, 1)
    zero = jnp.zeros_like(x)
    return jnp.concatenate([jnp.where(lane < HEAD_DIM, x, zero),
                            jnp.where(lane >= HEAD_DIM, x, zero)], axis=0)


SWA_BLOCKS = 4


def _alibi_slopes():
    n = N_HEADS
    return (2.0 ** (-8.0 * np.arange(1, n + 1, dtype=np.float32) / n)).astype(np.float32)


def _swa_kernel(sink_ref, q_ref, kp_ref, kc_ref, vp_ref, vc_ref, o_ref, bias_ref):
    first = (pl.program_id(0) == 0) & (pl.program_id(1) == 0)
    _swa_blocks(sink_ref, q_ref, kp_ref, kc_ref, vp_ref, vc_ref, o_ref, bias_ref, first,
                jnp.minimum(pl.program_id(1), 1), SWA_BLOCKS)


def _swa_blocks(sink_ref, q_ref, kp_ref, kc_ref, vp_ref, vc_ref, o_ref, bias_ref, first,
                has_prev_block, n_blocks):
    qi = lax.broadcasted_iota(jnp.int32, (BLOCK, BLOCK), 0)
    kj = lax.broadcasted_iota(jnp.int32, (BLOCK, BLOCK), 1)
    from_prev = kj > qi

    @pl.when(first)
    def _():
        distf = jnp.where(from_prev, qi + BLOCK - kj, qi - kj).astype(F32)
        slopes = _alibi_slopes()
        for h in range(N_HEADS):
            alibi = -float(slopes[h]) * distf
            bias_ref[1, h] = alibi
            bias_ref[0, h] = jnp.where(from_prev, NEG, alibi)

    low_q = lax.broadcasted_iota(jnp.int32, (BLOCK, LANES), 1) < HEAD_DIM
    src = lax.broadcasted_iota(jnp.int32, (LANES, LANES), 0)
    dst = lax.broadcasted_iota(jnp.int32, (LANES, LANES), 1)
    for sub in range(n_blocks):
        rows = slice(sub * BLOCK, (sub + 1) * BLOCK)
        before = slice((sub - 1) * BLOCK, sub * BLOCK)
        has_prev = 1 if sub else has_prev_block
        for g in range(N_KV_A):
            p, half = divmod(g, 2)
            cols = slice(p * LANES, (p + 1) * LANES)
            k_prev = kc_ref[before, cols] if sub else kp_ref[:, cols]
            v_prev = vc_ref[before, cols] if sub else vp_ref[:, cols]
            k2 = jnp.concatenate([k_prev, kc_ref[rows, cols]], axis=0)
            v2 = jnp.concatenate([v_prev, vc_ref[rows, cols]], axis=0)
            pick = jnp.where(src == dst % HEAD_DIM + half * HEAD_DIM, 1.0, 0.0).astype(BF16)
            kd = jnp.dot(k2, pick, preferred_element_type=F32).astype(BF16)
            vd = jnp.dot(v2, pick, preferred_element_type=F32).astype(BF16)

            q = q_ref[rows, g * GROUP_A * HEAD_DIM:(g + 1) * GROUP_A * HEAD_DIM] * SCALE
            lhs = jnp.concatenate([_head_split(q[:, i * LANES:(i + 1) * LANES])
                                   for i in range(GROUP_A // 2)], axis=0)
            s = lax.dot_general(lhs, kd, _NT, preferred_element_type=F32)
            ps, ls = [], []
            for hh in range(GROUP_A):
                h = g * GROUP_A + hh
                sb = s[hh * BLOCK:(hh + 1) * BLOCK]
                sh = jnp.where(from_prev, sb[:, :BLOCK], sb[:, BLOCK:]) + bias_ref[has_prev, h]
                sink = sink_ref[h]
                m = jnp.maximum(jnp.max(sh, axis=-1, keepdims=True), sink)
                e = jnp.exp(sh - m)
                ls.append(jnp.sum(e, axis=-1, keepdims=True) + jnp.exp(sink - m))
                ps.append(jnp.concatenate([jnp.where(from_prev, e, 0.0),
                                           jnp.where(from_prev, 0.0, e)], axis=1).astype(BF16))
            pv = jnp.dot(jnp.concatenate(ps, axis=0), vd, preferred_element_type=F32)
            for i in range(GROUP_A // 2):
                o0 = pv[(2 * i) * BLOCK:(2 * i + 1) * BLOCK] / ls[2 * i]
                o1 = pv[(2 * i + 1) * BLOCK:(2 * i + 2) * BLOCK] / ls[2 * i + 1]
                c0 = (g * GROUP_A // 2 + i) * LANES
                o_ref[rows, c0:c0 + LANES] = jnp.where(low_q, o0, o1).astype(o_ref.dtype)


def _swa_attention(proj, sinks, batch, seq):
    m = proj.shape[0]
    step = SWA_BLOCKS * BLOCK
    ns = seq // step
    kcol = BRANCH // KV_A
    vcol = kcol + 1

    def cur(b, n):
        return b * ns + n

    def prev(b, n):
        return b * ns * SWA_BLOCKS + jnp.maximum(n * SWA_BLOCKS - 1, 0)

    return pl.pallas_call(
        _swa_kernel,
        grid=(batch, ns),
        in_specs=[
            pl.BlockSpec(memory_space=pltpu.SMEM),
            pl.BlockSpec((step, BRANCH), lambda b, n: (cur(b, n), 0)),
            pl.BlockSpec((BLOCK, KV_A), lambda b, n: (prev(b, n), kcol)),
            pl.BlockSpec((step, KV_A), lambda b, n: (cur(b, n), kcol)),
            pl.BlockSpec((BLOCK, KV_A), lambda b, n: (prev(b, n), vcol)),
            pl.BlockSpec((step, KV_A), lambda b, n: (cur(b, n), vcol)),
        ],
        out_specs=pl.BlockSpec((step, BRANCH), lambda b, n: (cur(b, n), 0)),
        out_shape=jax.ShapeDtypeStruct((m, BRANCH), BF16),
        scratch_shapes=[pltpu.VMEM((2, N_HEADS, BLOCK, BLOCK), F32)],
        compiler_params=_params(("arbitrary", "arbitrary")),
        name="swa_attention",
    )(sinks, proj, proj, proj, proj, proj)


FUSED_BLOCKS = 2


def _swa_gate_kernel(sink_ref, q_ref, kp_ref, kc_ref, vp_ref, vc_ref, *refs, steps_per_seq):
    z_refs = refs[:Z_SPLIT]
    w_ref, g_ref, x_ref, out_ref, bias_ref, o_scr = refs[Z_SPLIT:]
    s = pl.program_id(0)
    last = pl.num_programs(0) - 2

    @pl.when(s == 0)
    def _():
        o_scr[...] = jnp.zeros_like(o_scr)

    slot = s % 2
    n = jnp.minimum(s, last) % steps_per_seq
    _swa_blocks(sink_ref, q_ref, kp_ref, kc_ref, vp_ref, vc_ref, o_scr.at[slot], bias_ref,
                s == 0, jnp.minimum(n, 1), FUSED_BLOCKS)

    z = jnp.concatenate([r[...] for r in z_refs], axis=1).astype(F32)
    gated = (o_scr[1 - slot].astype(F32) * (z * jax.nn.sigmoid(z))).astype(BF16)
    y = jnp.dot(gated, w_ref[...], preferred_element_type=F32)
    out_ref[...] = x_ref[...] + _rmsnorm(y, g_ref[...])


def _swa_gate(proj, sinks, w_out, g, x, batch, seq):
    m, d = x.shape
    step = FUSED_BLOCKS * BLOCK
    ns = seq // step
    total = m // step
    kcol = BRANCH // KV_A
    vcol = kcol + 1
    z0 = (BRANCH + 2 * KV_A) // Z_BLOCK

    def att(s):
        return jnp.minimum(s, total - 1)

    def prev(s):
        return jnp.maximum(att(s) * FUSED_BLOCKS - 1, 0)

    def gate(s):
        return jnp.maximum(s - 1, 0)

    z_specs = [pl.BlockSpec((step, Z_BLOCK), lambda s, c=c: (gate(s), z0 + c)) for c in range(Z_SPLIT)]
    return pl.pallas_call(
        functools.partial(_swa_gate_kernel, steps_per_seq=ns),
        grid=(total + 1,),
        in_specs=[
            pl.BlockSpec(memory_space=pltpu.SMEM),
            pl.BlockSpec((step, BRANCH), lambda s: (att(s), 0)),
            pl.BlockSpec((BLOCK, KV_A), lambda s: (prev(s), kcol)),
            pl.BlockSpec((step, KV_A), lambda s: (att(s), kcol)),
            pl.BlockSpec((BLOCK, KV_A), lambda s: (prev(s), vcol)),
            pl.BlockSpec((step, KV_A), lambda s: (att(s), vcol)),
        ] + z_specs + [
            pl.BlockSpec((BRANCH, d), lambda s: (0, 0)),
            pl.BlockSpec((1, d), lambda s: (0, 0)),
            pl.BlockSpec((step, d), lambda s: (gate(s), 0)),
        ],
        out_specs=pl.BlockSpec((step, d), lambda s: (gate(s), 0)),
        out_shape=jax.ShapeDtypeStruct((m, d), F32),
        scratch_shapes=[pltpu.VMEM((2, N_HEADS, BLOCK, BLOCK), F32),
                        pltpu.VMEM((2, step, BRANCH), BF16)],
        compiler_params=_params(("arbitrary",)),
        name="swa_gate",
    )(sinks, proj, proj, proj, proj, proj, *([proj] * Z_SPLIT), w_out, g.reshape(1, d), x)


class _Tiling(NamedTuple):
    tile: int
    pairs: int

    @property
    def groups(self):
        return N_PAIRS // self.pairs

    @property
    def heads(self):
        return 2 * self.pairs


STICK_TILING = _Tiling(tile=256, pairs=16)
FOX_TILING = _Tiling(tile=512, pairs=2)


def _chunk_rows(c, tile):
    if isinstance(c, int):
        return pl.ds(c * tile, tile)
    return pl.ds(pl.multiple_of(c * tile, tile), tile)


def _causal_attention_call(kernel_fn, name, tiling, proj, extra_inputs, extra_specs, scratch,
                           batch, seq):
    m = proj.shape[0]
    tile, groups = tiling.tile, tiling.groups
    nq = seq // tile
    width = tiling.pairs * LANES
    return pl.pallas_call(
        functools.partial(kernel_fn, tiling=tiling),
        grid=(batch, groups, nq),
        in_specs=[
            pl.BlockSpec((tile, width), lambda b, g, i: (b * nq + i, g)),
            pl.BlockSpec((seq, width), lambda b, g, i: (b, groups + g)),
            pl.BlockSpec((seq, width), lambda b, g, i: (b, 2 * groups + g)),
        ] + extra_specs,
        out_specs=pl.BlockSpec((tile, width), lambda b, g, i: (b * nq + i, g)),
        out_shape=jax.ShapeDtypeStruct((m, BRANCH), BF16),
        scratch_shapes=scratch,
        compiler_params=_params(("arbitrary", "arbitrary", "arbitrary")),
        name=name,
    )(proj, proj, proj, *extra_inputs)


def _suffix_matrix():
    j = lax.broadcasted_iota(jnp.int32, (BLOCK, 2 * BLOCK), 0)
    s = lax.broadcasted_iota(jnp.int32, (BLOCK, 2 * BLOCK), 1)
    return jnp.where((s >= BLOCK) | (j > s), 1.0, 0.0).astype(BF16)


SOFTPLUS_CLAMP = 60.0
EXP_UNDERFLOW = 105.0


def _softplus(s):
    return jnp.maximum(s, jnp.log(1.0 + jnp.exp(jnp.minimum(s, SOFTPLUS_CLAMP))))


def _stick_kernel(q_ref, k_ref, v_ref, o_ref, acc_ref, carry_ref, *, tiling):
    tile, pairs = tiling
    nkb = tile // BLOCK
    i = pl.program_id(2)
    u = _suffix_matrix()

    row = lax.broadcasted_iota(jnp.int32, (tile, tile), 0)
    col = lax.broadcasted_iota(jnp.int32, (tile, tile), 1)
    before = jnp.concatenate([col < row, col < row], axis=1)

    def chunk(c, mask):
        rows = _chunk_rows(c, tile)
        for pr in range(pairs):
            cols = slice(pr * LANES, (pr + 1) * LANES)
            q2 = q_ref[:, cols] * SCALE
            s = lax.dot_general(q2, _head_split(k_ref[rows, cols]), _NT,
                                preferred_element_type=F32)
            sp = _softplus(s)
            if mask is not None:
                sp = jnp.where(mask, sp, 0.0)
            spb = sp.astype(BF16)
            base = s - sp
            ps = [None] * (2 * nkb)
            for h in range(2):
                carry = carry_ref[pr, h]
                for kb in reversed(range(nkb)):
                    sl = slice(h * tile + kb * BLOCK, h * tile + (kb + 1) * BLOCK)
                    cs = jnp.dot(spb[:, sl], u, preferred_element_type=F32)
                    ps[h * nkb + kb] = jnp.exp(base[:, sl] - cs[:, :BLOCK] - carry)
                    carry = carry + cs[:, BLOCK:]
                carry_ref[pr, h] = carry
            p = jnp.concatenate(ps, axis=1)
            if mask is not None:
                p = jnp.where(mask, p, 0.0)
            acc_ref[pr] += jnp.dot(p.astype(BF16), _head_split(v_ref[rows, cols]),
                                   preferred_element_type=F32)

    acc_ref[...] = jnp.zeros_like(acc_ref)
    carry_ref[...] = jnp.zeros_like(carry_ref)
    chunk(i, before)

    def weights_vanish():
        return (jnp.min(carry_ref[...]) >= EXP_UNDERFLOW).astype(jnp.int32)

    def cond(st):
        t, done = st
        return jnp.logical_and(t < i, done == 0)

    def body(st):
        t, _ = st
        chunk(i - 1 - t, None)
        return t + 1, weights_vanish()

    lax.while_loop(cond, body, (jnp.int32(0), weights_vanish()))
    for pr in range(pairs):
        o_ref[:, pr * LANES:(pr + 1) * LANES] = acc_ref[pr].astype(o_ref.dtype)


def _stick_attention(proj, batch, seq):
    tile, pairs = STICK_TILING
    scratch = [pltpu.VMEM((pairs, tile, LANES), F32),
               pltpu.VMEM((pairs, 2, tile, LANES), F32)]
    return _causal_attention_call(_stick_kernel, "stick_attention", STICK_TILING, proj, [], [],
                                  scratch, batch, seq)


def _split3(x):
    hi = x.astype(BF16)
    r = x - hi.astype(F32)
    mid = r.astype(BF16)
    lo = (r - mid.astype(F32)).astype(BF16)
    return hi, mid, lo


def _cum_kernel(fl_ref, bf_ref, cum_ref):
    x = fl_ref[...] + bf_ref[...]
    lf = jnp.minimum(x, 0.0) - jnp.log(1.0 + jnp.exp(-jnp.abs(x)))
    t = lax.broadcasted_iota(jnp.int32, (BLOCK, BLOCK), 0)
    j = lax.broadcasted_iota(jnp.int32, (BLOCK, BLOCK), 1)
    tri = jnp.where(j <= t, 1.0, 0.0).astype(BF16)
    parts = jnp.concatenate(_split3(lf), axis=1)
    total = jnp.zeros((1, LANES), F32)
    for blk in range(x.shape[0] // BLOCK):
        rows = slice(blk * BLOCK, (blk + 1) * BLOCK)
        c = jnp.dot(tri, parts[rows], preferred_element_type=F32)
        c = c[:, :LANES] + c[:, LANES:2 * LANES] + c[:, 2 * LANES:] + total
        cum_ref[rows, :] = c
        total = c[BLOCK - 1:BLOCK, :]


def _cum_log_forget(f_logit, b_f, batch, seq):
    return pl.pallas_call(
        _cum_kernel,
        grid=(batch,),
        in_specs=[pl.BlockSpec((seq, LANES), lambda b: (b, 0)),
                  pl.BlockSpec((1, LANES), lambda b: (0, 0))],
        out_specs=pl.BlockSpec((seq, LANES), lambda b: (b, 0)),
        out_shape=jax.ShapeDtypeStruct((batch * seq, LANES), F32),
        compiler_params=_params(("parallel",)),
        name="cum_log_forget",
    )(f_logit, b_f)


def _fox_kernel(q_ref, k_ref, v_ref, cq_ref, ck_ref, o_ref, u_ref, acc_ref, top_ref, mask_ref,
                ones_ref, *, tiling):
    tile, pairs = tiling
    nl = tile // LANES
    i = pl.program_id(2)
    cq_all = cq_ref[0, 0]

    @pl.when((pl.program_id(0) == 0) & (pl.program_id(1) == 0) & (i == 0))
    def _():
        row = lax.broadcasted_iota(jnp.int32, (BLOCK, BLOCK), 0)
        col = lax.broadcasted_iota(jnp.int32, (BLOCK, BLOCK), 1)
        mask_ref[...] = jnp.where(col <= row, 0.0, NEG)
        r2 = lax.broadcasted_iota(jnp.int32, (2 * tile, LANES), 0)
        l2 = lax.broadcasted_iota(jnp.int32, (2 * tile, LANES), 1)
        ones_ref[...] = jnp.where((r2 < tile) == (l2 < HEAD_DIM), 1.0, 0.0).astype(BF16)

    def lane_max(top, u):
        for b in range(u.shape[1] // LANES):
            top = jnp.maximum(top, u[:, b * LANES:(b + 1) * LANES])
        return top

    def pair_scores(c, pr):
        rows = _chunk_rows(c, tile)
        cols = slice(pr * LANES, (pr + 1) * LANES)
        return lax.dot_general(q_ref[:, cols] * SCALE, _head_split(k_ref[rows, cols]), _NT,
                               preferred_element_type=F32)

    def scores(c, _):
        ck_all = ck_ref[0, c, 0]
        for pr in range(pairs):
            s = pair_scores(c, pr)
            for h in range(2):
                hd = 2 * pr + h
                u = (s[:, h * tile:(h + 1) * tile] - ck_all[hd:hd + 1, :]) * LOG2E
                u_ref[pr, c, :, h * tile:(h + 1) * tile] = u
                top_ref[hd] = lane_max(top_ref[hd], u)
        return 0

    def diagonal_scores(d):
        ck_all = ck_ref[0, d, 0]
        for pr in range(pairs):
            s = pair_scores(d, pr)
            for h in range(2):
                hd = 2 * pr + h
                for rb in range(nl):
                    r = slice(rb * BLOCK, (rb + 1) * BLOCK)
                    vis = (rb + 1) * BLOCK
                    u = (s[r, h * tile:h * tile + vis] - ck_all[hd:hd + 1, :vis]) * LOG2E
                    last = u[:, rb * BLOCK:] + mask_ref[...]
                    u = jnp.concatenate([u[:, :rb * BLOCK], last], axis=1) if rb else last
                    u_ref[pr, d, r, h * tile:h * tile + vis] = u
                    top_ref[hd, r, :] = lane_max(top_ref[hd, r, :], u)

    def weighted_values(c, pr, ps):
        rows = _chunk_rows(c, tile)
        rhs = jnp.concatenate([_head_split(v_ref[rows, pr * LANES:(pr + 1) * LANES]), ones_ref[...]],
                              axis=1)
        return jnp.dot(jnp.concatenate(ps, axis=1), rhs, preferred_element_type=F32)

    def diagonal_weigh(d, shifts):
        for pr in range(pairs):
            ps = []
            for h in range(2):
                blocks = []
                for rb in range(nl):
                    r = slice(rb * BLOCK, (rb + 1) * BLOCK)
                    vis = (rb + 1) * BLOCK
                    p = jnp.exp2(u_ref[pr, d, r, h * tile:h * tile + vis] - shifts[2 * pr + h][r, :vis])
                    if vis < tile:
                        p = jnp.concatenate([p, jnp.zeros((BLOCK, tile - vis), F32)], axis=1)
                    blocks.append(p.astype(BF16))
                ps.append(jnp.concatenate(blocks, axis=0))
            acc_ref[pr] = weighted_values(d, pr, ps)

    def weigh(c, shifts):
        for pr in range(pairs):
            ps = [jnp.exp2(u_ref[pr, c, :, h * tile:(h + 1) * tile] - shifts[2 * pr + h]).astype(BF16)
                  for h in range(2)]
            acc_ref[pr] += weighted_values(c, pr, ps)

    def query_tile(d):
        top_ref[...] = jnp.full_like(top_ref, NEG)
        for c in range(d):
            scores(c, 0)
        diagonal_scores(d)
        shifts = []
        for hd in range(2 * pairs):
            cq = cq_all[:, hd:hd + 1] * LOG2E
            m = jnp.max(top_ref[hd], axis=-1, keepdims=True) + cq
            shifts.append(jnp.concatenate([jnp.broadcast_to(m - cq, (tile, LANES))] * nl, axis=1))
        diagonal_weigh(d, shifts)
        for c in range(d):
            weigh(c, shifts)
        for pr in range(pairs):
            acc = acc_ref[pr]
            o_ref[:, pr * LANES:(pr + 1) * LANES] = (
                acc[:, :LANES] / acc[:, LANES:]).astype(o_ref.dtype)

    for d in range(u_ref.shape[1]):
        pl.when(i == d)(functools.partial(query_tile, d))


def _fox_attention(proj, cum, batch, seq):
    tile, groups, heads = FOX_TILING.tile, FOX_TILING.groups, FOX_TILING.heads
    nc = seq // tile
    c = cum.reshape(batch, seq, LANES)[:, :, :N_HEADS]
    cq = c.reshape(batch, seq, groups, heads).transpose(0, 2, 1, 3)
    ck = c.reshape(batch, nc, tile, groups, heads).transpose(0, 1, 3, 4, 2)
    specs = [
        pl.BlockSpec((1, 1, tile, heads), lambda b, g, i: (b, g, i, 0)),
        pl.BlockSpec((1, nc, 1, heads, tile), lambda b, g, i: (b, 0, g, 0, 0)),
    ]
    pairs = FOX_TILING.pairs
    scratch = [pltpu.VMEM((pairs, nc, tile, 2 * tile), F32),
               pltpu.VMEM((pairs, tile, 2 * LANES), F32),
               pltpu.VMEM((heads, tile, LANES), F32),
               pltpu.VMEM((BLOCK, BLOCK), F32),
               pltpu.VMEM((2 * tile, LANES), BF16)]
    return _causal_attention_call(_fox_kernel, "fox_attention", FOX_TILING, proj, [cq, ck], specs,
                                  scratch, batch, seq)


def kernel(x, g_pre, g_post, w_in_a, w_out_a, sinks_a, w_in_b, w_out_b, w_in_c, b_f_c, w_out_c):
    batch, seq, d = x.shape
    depth = g_pre.shape[0]
    xf = x.reshape(batch * seq, d)
    wa, wb, wc = w_in_a.astype(BF16), w_in_b.astype(BF16), w_in_c.astype(BF16)
    for i in range(depth):
        kind, j = i % 3, i // 3
        if kind == 0:
            proj = _norm_proj(xf, g_pre[i], wa, j, BRANCH + 2 * KV_A + BRANCH, PROJ_TN_A)
            xf = _swa_gate(proj, sinks_a[j], w_out_a[j].astype(BF16), g_post[i], xf, batch, seq)
            continue
        elif kind == 1:
            proj = _norm_proj(xf, g_pre[i], wb, j, 4 * BRANCH, PROJ_TN)
            o = _stick_attention(proj, batch, seq)
            z_col, w_out = 3 * BRANCH, w_out_b[j]
        else:
            w_f = jnp.pad(wc[j, :, 4 * BRANCH:], ((0, 0), (0, LANES - N_HEADS)))
            proj, f_logit = _norm_proj(xf, g_pre[i], wc, j, 4 * BRANCH, PROJ_TN, w_narrow=w_f)
            b_f = jnp.pad(b_f_c[j], (0, LANES - N_HEADS)).reshape(1, LANES)
            cum = _cum_log_forget(f_logit, b_f, batch, seq)
            o = _fox_attention(proj, cum, batch, seq)
            z_col, w_out = 3 * BRANCH, w_out_c[j]
        xf = _gate_out(o, proj, z_col, w_out.astype(BF16), g_post[i], xf)
    return xf.reshape(batch, seq, d)
```

```python
import functools
from typing import NamedTuple

import jax
import jax.numpy as jnp
import numpy as np
from jax import lax
from jax.experimental import pallas as pl
from jax.experimental.pallas import tpu as pltpu

HEAD_DIM = 64
N_HEADS = 32
BRANCH = N_HEADS * HEAD_DIM
N_KV_A = 4
KV_A = N_KV_A * HEAD_DIM
GROUP_A = N_HEADS // N_KV_A
BLOCK = 128
LANES = 128
NORM_EPS = 1e-6
NEG = -1e30
SCALE = HEAD_DIM ** -0.5
LOG2E = 1.4426950408889634
N_PAIRS = N_HEADS // 2

F32 = jnp.float32
BF16 = jnp.bfloat16

VMEM_LIMIT = 52 * 1024 * 1024
PROJ_TM = 1024
PROJ_TN_A, PROJ_TN = 1536, 2048
GATE_TM = 512

_NT = (((1,), (1,)), ((), ()))


def _params(sem):
    return pltpu.CompilerParams(dimension_semantics=sem, vmem_limit_bytes=VMEM_LIMIT)


def _rmsnorm(x, g):
    r = lax.rsqrt(jnp.mean(x * x, axis=-1, keepdims=True) + NORM_EPS)
    return x * r * g


def _norm_proj_kernel(x_ref, g_ref, w_ref, *refs, narrow):
    o_ref, h_ref = refs[-3 if narrow else -2], refs[-1]

    @pl.when(pl.program_id(1) == 0)
    def _():
        h_ref[...] = _rmsnorm(x_ref[...], g_ref[...]).astype(BF16)
        if narrow:
            wn_ref, on_ref = refs[0], refs[2]
            on_ref[...] = jnp.dot(h_ref[...], wn_ref[...], preferred_element_type=F32)

    o_ref[...] = jnp.dot(h_ref[...], w_ref[...], preferred_element_type=F32).astype(o_ref.dtype)


def _norm_proj(x, g, w, layer, n, tn, w_narrow=None, tm=PROJ_TM):
    m, d = x.shape
    narrow = w_narrow is not None
    row = pl.BlockSpec((tm, LANES), lambda i, j: (i, 0))
    outs = pl.pallas_call(
        functools.partial(_norm_proj_kernel, narrow=narrow),
        grid=(m // tm, n // tn),
        in_specs=[
            pl.BlockSpec((tm, d), lambda i, j: (i, 0)),
            pl.BlockSpec((1, d), lambda i, j: (0, 0)),
            pl.BlockSpec((None, d, tn), lambda i, j: (layer, 0, j)),
        ] + [pl.BlockSpec((d, LANES), lambda i, j: (0, 0))] * narrow,
        out_specs=[pl.BlockSpec((tm, tn), lambda i, j: (i, j))] + [row] * narrow,
        out_shape=[jax.ShapeDtypeStruct((m, n), BF16)] + [jax.ShapeDtypeStruct((m, LANES), F32)] * narrow,
        scratch_shapes=[pltpu.VMEM((tm, d), BF16)],
        compiler_params=_params(("parallel", "arbitrary")),
        name="norm_proj",
    )(x, g.reshape(1, d), w, *([w_narrow] if narrow else []))
    return tuple(outs) if narrow else outs[0]


Z_SPLIT = 4
Z_BLOCK = BRANCH // Z_SPLIT


def _gate_out_kernel(o_ref, *refs):
    z_refs, (w_ref, g_ref, x_ref, out_ref) = refs[:Z_SPLIT], refs[Z_SPLIT:]
    z = jnp.concatenate([r[...] for r in z_refs], axis=1).astype(F32)
    gated = (o_ref[...].astype(F32) * (z * jax.nn.sigmoid(z))).astype(BF16)
    y = jnp.dot(gated, w_ref[...], preferred_element_type=F32)
    out_ref[...] = x_ref[...] + _rmsnorm(y, g_ref[...])


def _gate_out(o, proj, z_col, w_out, g, x, tm=GATE_TM):
    m, d = x.shape
    z0 = z_col // Z_BLOCK
    z_specs = [pl.BlockSpec((tm, Z_BLOCK), lambda i, c=c: (i, z0 + c)) for c in range(Z_SPLIT)]
    return pl.pallas_call(
        _gate_out_kernel,
        grid=(m // tm,),
        in_specs=[pl.BlockSpec((tm, BRANCH), lambda i: (i, 0))] + z_specs + [
            pl.BlockSpec((BRANCH, d), lambda i: (0, 0)),
            pl.BlockSpec((1, d), lambda i: (0, 0)),
            pl.BlockSpec((tm, d), lambda i: (i, 0)),
        ],
        out_specs=pl.BlockSpec((tm, d), lambda i: (i, 0)),
        out_shape=jax.ShapeDtypeStruct((m, d), F32),
        compiler_params=_params(("parallel",)),
        name="gate_out",
    )(o, *([proj] * Z_SPLIT), w_out, g.reshape(1, d), x)


def _head_split(x):
    lane = lax.broadcasted_iota(jnp.int32, x.shape, 1)
    zero = jnp.zeros_like(x)
    return jnp.concatenate([jnp.where(lane < HEAD_DIM, x, zero),
                            jnp.where(lane >= HEAD_DIM, x, zero)], axis=0)


SWA_BLOCKS = 4


def _alibi_slopes():
    n = N_HEADS
    return (2.0 ** (-8.0 * np.arange(1, n + 1, dtype=np.float32) / n)).astype(np.float32)


def _swa_kernel(sink_ref, q_ref, kp_ref, kc_ref, vp_ref, vc_ref, o_ref, bias_ref):
    qi = lax.broadcasted_iota(jnp.int32, (BLOCK, BLOCK), 0)
    kj = lax.broadcasted_iota(jnp.int32, (BLOCK, BLOCK), 1)
    from_prev = kj > qi
    first = (pl.program_id(0) == 0) & (pl.program_id(1) == 0)

    @pl.when(first)
    def _():
        distf = jnp.where(from_prev, qi + BLOCK - kj, qi - kj).astype(F32)
        slopes = _alibi_slopes()
        for h in range(N_HEADS):
            alibi = -float(slopes[h]) * distf
            bias_ref[1, h] = alibi
            bias_ref[0, h] = jnp.where(from_prev, NEG, alibi)

    low_q = lax.broadcasted_iota(jnp.int32, (BLOCK, LANES), 1) < HEAD_DIM
    src = lax.broadcasted_iota(jnp.int32, (LANES, LANES), 0)
    dst = lax.broadcasted_iota(jnp.int32, (LANES, LANES), 1)
    for sub in range(SWA_BLOCKS):
        rows = slice(sub * BLOCK, (sub + 1) * BLOCK)
        before = slice((sub - 1) * BLOCK, sub * BLOCK)
        has_prev = 1 if sub else jnp.minimum(pl.program_id(1), 1)
        for g in range(N_KV_A):
            p, half = divmod(g, 2)
            cols = slice(p * LANES, (p + 1) * LANES)
            k_prev = kc_ref[before, cols] if sub else kp_ref[:, cols]
            v_prev = vc_ref[before, cols] if sub else vp_ref[:, cols]
            k2 = jnp.concatenate([k_prev, kc_ref[rows, cols]], axis=0)
            v2 = jnp.concatenate([v_prev, vc_ref[rows, cols]], axis=0)
            pick = jnp.where(src == dst % HEAD_DIM + half * HEAD_DIM, 1.0, 0.0).astype(BF16)
            kd = jnp.dot(k2, pick, preferred_element_type=F32).astype(BF16)
            vd = jnp.dot(v2, pick, preferred_element_type=F32).astype(BF16)

            q = q_ref[rows, g * GROUP_A * HEAD_DIM:(g + 1) * GROUP_A * HEAD_DIM] * SCALE
            lhs = jnp.concatenate([_head_split(q[:, i * LANES:(i + 1) * LANES])
                                   for i in range(GROUP_A // 2)], axis=0)
            s = lax.dot_general(lhs, kd, _NT, preferred_element_type=F32)
            ps, ls = [], []
            for hh in range(GROUP_A):
                h = g * GROUP_A + hh
                sb = s[hh * BLOCK:(hh + 1) * BLOCK]
                sh = jnp.where(from_prev, sb[:, :BLOCK], sb[:, BLOCK:]) + bias_ref[has_prev, h]
                sink = sink_ref[h]
                m = jnp.maximum(jnp.max(sh, axis=-1, keepdims=True), sink)
                e = jnp.exp(sh - m)
                ls.append(jnp.sum(e, axis=-1, keepdims=True) + jnp.exp(sink - m))
                ps.append(jnp.concatenate([jnp.where(from_prev, e, 0.0),
                                           jnp.where(from_prev, 0.0, e)], axis=1).astype(BF16))
            pv = jnp.dot(jnp.concatenate(ps, axis=0), vd, preferred_element_type=F32)
            for i in range(GROUP_A // 2):
                o0 = pv[(2 * i) * BLOCK:(2 * i + 1) * BLOCK] / ls[2 * i]
                o1 = pv[(2 * i + 1) * BLOCK:(2 * i + 2) * BLOCK] / ls[2 * i + 1]
                c0 = (g * GROUP_A // 2 + i) * LANES
                o_ref[rows, c0:c0 + LANES] = jnp.where(low_q, o0, o1).astype(o_ref.dtype)


def _swa_attention(proj, sinks, batch, seq):
    m = proj.shape[0]
    step = SWA_BLOCKS * BLOCK
    ns = seq // step
    kcol = BRANCH // KV_A
    vcol = kcol + 1

    def cur(b, n):
        return b * ns + n

    def prev(b, n):
        return b * ns * SWA_BLOCKS + jnp.maximum(n * SWA_BLOCKS - 1, 0)

    return pl.pallas_call(
        _swa_kernel,
        grid=(batch, ns),
        in_specs=[
            pl.BlockSpec(memory_space=pltpu.SMEM),
            pl.BlockSpec((step, BRANCH), lambda b, n: (cur(b, n), 0)),
            pl.BlockSpec((BLOCK, KV_A), lambda b, n: (prev(b, n), kcol)),
            pl.BlockSpec((step, KV_A), lambda b, n: (cur(b, n), kcol)),
            pl.BlockSpec((BLOCK, KV_A), lambda b, n: (prev(b, n), vcol)),
            pl.BlockSpec((step, KV_A), lambda b, n: (cur(b, n), vcol)),
        ],
        out_specs=pl.BlockSpec((step, BRANCH), lambda b, n: (cur(b, n), 0)),
        out_shape=jax.ShapeDtypeStruct((m, BRANCH), BF16),
        scratch_shapes=[pltpu.VMEM((2, N_HEADS, BLOCK, BLOCK), F32)],
        compiler_params=_params(("arbitrary", "arbitrary")),
        name="swa_attention",
    )(sinks, proj, proj, proj, proj, proj)


class _Tiling(NamedTuple):
    tile: int
    pairs: int

    @property
    def groups(self):
        return N_PAIRS // self.pairs

    @property
    def heads(self):
        return 2 * self.pairs


STICK_TILING = _Tiling(tile=256, pairs=16)
FOX_TILING = _Tiling(tile=512, pairs=2)


def _chunk_rows(c, tile):
    if isinstance(c, int):
        return pl.ds(c * tile, tile)
    return pl.ds(pl.multiple_of(c * tile, tile), tile)


def _causal_attention_call(kernel_fn, name, tiling, proj, extra_inputs, extra_specs, scratch,
                           batch, seq):
    m = proj.shape[0]
    tile, groups = tiling.tile, tiling.groups
    nq = seq // tile
    width = tiling.pairs * LANES
    return pl.pallas_call(
        functools.partial(kernel_fn, tiling=tiling),
        grid=(batch, groups, nq),
        in_specs=[
            pl.BlockSpec((tile, width), lambda b, g, i: (b * nq + i, g)),
            pl.BlockSpec((seq, width), lambda b, g, i: (b, groups + g)),
            pl.BlockSpec((seq, width), lambda b, g, i: (b, 2 * groups + g)),
        ] + extra_specs,
        out_specs=pl.BlockSpec((tile, width), lambda b, g, i: (b * nq + i, g)),
        out_shape=jax.ShapeDtypeStruct((m, BRANCH), BF16),
        scratch_shapes=scratch,
        compiler_params=_params(("arbitrary", "arbitrary", "arbitrary")),
        name=name,
    )(proj, proj, proj, *extra_inputs)


def _suffix_matrix():
    j = lax.broadcasted_iota(jnp.int32, (BLOCK, 2 * BLOCK), 0)
    s = lax.broadcasted_iota(jnp.int32, (BLOCK, 2 * BLOCK), 1)
    return jnp.where((s >= BLOCK) | (j > s), 1.0, 0.0).astype(BF16)


SOFTPLUS_CLAMP = 60.0
EXP_UNDERFLOW = 105.0


def _softplus(s):
    return jnp.maximum(s, jnp.log(1.0 + jnp.exp(jnp.minimum(s, SOFTPLUS_CLAMP))))


def _stick_kernel(q_ref, k_ref, v_ref, o_ref, acc_ref, carry_ref, *, tiling):
    tile, pairs = tiling
    nkb = tile // BLOCK
    i = pl.program_id(2)
    u = _suffix_matrix()

    row = lax.broadcasted_iota(jnp.int32, (tile, tile), 0)
    col = lax.broadcasted_iota(jnp.int32, (tile, tile), 1)
    before = jnp.concatenate([col < row, col < row], axis=1)

    def chunk(c, mask):
        rows = _chunk_rows(c, tile)
        lowest = None
        for pr in range(pairs):
            cols = slice(pr * LANES, (pr + 1) * LANES)
            q2 = q_ref[:, cols] * SCALE
            s = lax.dot_general(q2, _head_split(k_ref[rows, cols]), _NT,
                                preferred_element_type=F32)
            sp = _softplus(s)
            if mask is not None:
                sp = jnp.where(mask, sp, 0.0)
            spb = sp.astype(BF16)
            base = s - sp
            ps = [None] * (2 * nkb)
            for h in range(2):
                carry = carry_ref[pr, h]
                for kb in reversed(range(nkb)):
                    sl = slice(h * tile + kb * BLOCK, h * tile + (kb + 1) * BLOCK)
                    cs = jnp.dot(spb[:, sl], u, preferred_element_type=F32)
                    ps[h * nkb + kb] = jnp.exp(base[:, sl] - cs[:, :BLOCK] - carry)
                    carry = carry + cs[:, BLOCK:]
                carry_ref[pr, h] = carry
                lowest = carry if lowest is None else jnp.minimum(lowest, carry)
            p = jnp.concatenate(ps, axis=1)
            if mask is not None:
                p = jnp.where(mask, p, 0.0)
            acc_ref[pr] += jnp.dot(p.astype(BF16), _head_split(v_ref[rows, cols]),
                                   preferred_element_type=F32)
        return (jnp.min(lowest) >= EXP_UNDERFLOW).astype(jnp.int32)

    acc_ref[...] = jnp.zeros_like(acc_ref)
    carry_ref[...] = jnp.zeros_like(carry_ref)

    def cond(st):
        t, done = st
        return jnp.logical_and(t < i, done == 0)

    def body(st):
        t, _ = st
        return t + 1, chunk(i - 1 - t, None)

    lax.while_loop(cond, body, (jnp.int32(0), chunk(i, before)))
    for pr in range(pairs):
        o_ref[:, pr * LANES:(pr + 1) * LANES] = acc_ref[pr].astype(o_ref.dtype)


def _stick_attention(proj, batch, seq):
    tile, pairs = STICK_TILING
    scratch = [pltpu.VMEM((pairs, tile, LANES), F32),
               pltpu.VMEM((pairs, 2, tile, LANES), F32)]
    return _causal_attention_call(_stick_kernel, "stick_attention", STICK_TILING, proj, [], [],
                                  scratch, batch, seq)


def _split3(x):
    hi = x.astype(BF16)
    r = x - hi.astype(F32)
    mid = r.astype(BF16)
    lo = (r - mid.astype(F32)).astype(BF16)
    return hi, mid, lo


def _cum_kernel(fl_ref, bf_ref, cum_ref):
    x = fl_ref[...] + bf_ref[...]
    lf = jnp.minimum(x, 0.0) - jnp.log(1.0 + jnp.exp(-jnp.abs(x)))
    t = lax.broadcasted_iota(jnp.int32, (BLOCK, BLOCK), 0)
    j = lax.broadcasted_iota(jnp.int32, (BLOCK, BLOCK), 1)
    tri = jnp.where(j <= t, 1.0, 0.0).astype(BF16)
    parts = jnp.concatenate(_split3(lf), axis=1)
    total = jnp.zeros((1, LANES), F32)
    for blk in range(x.shape[0] // BLOCK):
        rows = slice(blk * BLOCK, (blk + 1) * BLOCK)
        c = jnp.dot(tri, parts[rows], preferred_element_type=F32)
        c = c[:, :LANES] + c[:, LANES:2 * LANES] + c[:, 2 * LANES:] + total
        cum_ref[rows, :] = c
        total = c[BLOCK - 1:BLOCK, :]


def _cum_log_forget(f_logit, b_f, batch, seq):
    return pl.pallas_call(
        _cum_kernel,
        grid=(batch,),
        in_specs=[pl.BlockSpec((seq, LANES), lambda b: (b, 0)),
                  pl.BlockSpec((1, LANES), lambda b: (0, 0))],
        out_specs=pl.BlockSpec((seq, LANES), lambda b: (b, 0)),
        out_shape=jax.ShapeDtypeStruct((batch * seq, LANES), F32),
        compiler_params=_params(("parallel",)),
        name="cum_log_forget",
    )(f_logit, b_f)


def _fox_kernel(q_ref, k_ref, v_ref, cq_ref, ck_ref, o_ref, u_ref, acc_ref, top_ref, mask_ref,
                ones_ref, *, tiling):
    tile, pairs = tiling
    nl = tile // LANES
    i = pl.program_id(2)
    cq_all = cq_ref[0, 0]

    @pl.when((pl.program_id(0) == 0) & (pl.program_id(1) == 0) & (i == 0))
    def _():
        row = lax.broadcasted_iota(jnp.int32, (BLOCK, BLOCK), 0)
        col = lax.broadcasted_iota(jnp.int32, (BLOCK, BLOCK), 1)
        mask_ref[...] = jnp.where(col <= row, 0.0, NEG)
        r2 = lax.broadcasted_iota(jnp.int32, (2 * tile, LANES), 0)
        l2 = lax.broadcasted_iota(jnp.int32, (2 * tile, LANES), 1)
        ones_ref[...] = jnp.where((r2 < tile) == (l2 < HEAD_DIM), 1.0, 0.0).astype(BF16)

    def lane_max(top, u):
        for b in range(u.shape[1] // LANES):
            top = jnp.maximum(top, u[:, b * LANES:(b + 1) * LANES])
        return top

    def pair_scores(c, pr):
        rows = _chunk_rows(c, tile)
        cols = slice(pr * LANES, (pr + 1) * LANES)
        return lax.dot_general(q_ref[:, cols] * SCALE, _head_split(k_ref[rows, cols]), _NT,
                               preferred_element_type=F32)

    def scores(c, _):
        ck_all = ck_ref[0, c, 0]
        for pr in range(pairs):
            s = pair_scores(c, pr)
            for h in range(2):
                hd = 2 * pr + h
                u = (s[:, h * tile:(h + 1) * tile] - ck_all[hd:hd + 1, :]) * LOG2E
                u_ref[pr, c, :, h * tile:(h + 1) * tile] = u
                top_ref[hd] = lane_max(top_ref[hd], u)
        return 0

    def diagonal_scores(d):
        ck_all = ck_ref[0, d, 0]
        for pr in range(pairs):
            s = pair_scores(d, pr)
            for h in range(2):
                hd = 2 * pr + h
                for rb in range(nl):
                    r = slice(rb * BLOCK, (rb + 1) * BLOCK)
                    vis = (rb + 1) * BLOCK
                    u = (s[r, h * tile:h * tile + vis] - ck_all[hd:hd + 1, :vis]) * LOG2E
                    last = u[:, rb * BLOCK:] + mask_ref[...]
                    u = jnp.concatenate([u[:, :rb * BLOCK], last], axis=1) if rb else last
                    u_ref[pr, d, r, h * tile:h * tile + vis] = u
                    top_ref[hd, r, :] = lane_max(top_ref[hd, r, :], u)

    def weighted_values(c, pr, ps):
        rows = _chunk_rows(c, tile)
        rhs = jnp.concatenate([_head_split(v_ref[rows, pr * LANES:(pr + 1) * LANES]), ones_ref[...]],
                              axis=1)
        return jnp.dot(jnp.concatenate(ps, axis=1), rhs, preferred_element_type=F32)

    def diagonal_weigh(d, shifts):
        for pr in range(pairs):
            ps = []
            for h in range(2):
                blocks = []
                for rb in range(nl):
                    r = slice(rb * BLOCK, (rb + 1) * BLOCK)
                    vis = (rb + 1) * BLOCK
                    p = jnp.exp2(u_ref[pr, d, r, h * tile:h * tile + vis] - shifts[2 * pr + h][r, :vis])
                    if vis < tile:
                        p = jnp.concatenate([p, jnp.zeros((BLOCK, tile - vis), F32)], axis=1)
                    blocks.append(p.astype(BF16))
                ps.append(jnp.concatenate(blocks, axis=0))
            acc_ref[pr] = weighted_values(d, pr, ps)

    def weigh(c, shifts):
        for pr in range(pairs):
            ps = [jnp.exp2(u_ref[pr, c, :, h * tile:(h + 1) * tile] - shifts[2 * pr + h]).astype(BF16)
                  for h in range(2)]
            acc_ref[pr] += weighted_values(c, pr, ps)

    def query_tile(d):
        top_ref[...] = jnp.full_like(top_ref, NEG)
        for c in range(d):
            scores(c, 0)
        diagonal_scores(d)
        shifts = []
        for hd in range(2 * pairs):
            cq = cq_all[:, hd:hd + 1] * LOG2E
            m = jnp.max(top_ref[hd], axis=-1, keepdims=True) + cq
            shifts.append(jnp.concatenate([jnp.broadcast_to(m - cq, (tile, LANES))] * nl, axis=1))
        diagonal_weigh(d, shifts)
        for c in range(d):
            weigh(c, shifts)
        for pr in range(pairs):
            acc = acc_ref[pr]
            o_ref[:, pr * LANES:(pr + 1) * LANES] = (
                acc[:, :LANES] / acc[:, LANES:]).astype(o_ref.dtype)

    for d in range(u_ref.shape[1]):
        pl.when(i == d)(functools.partial(query_tile, d))


def _fox_attention(proj, cum, batch, seq):
    tile, groups, heads = FOX_TILING.tile, FOX_TILING.groups, FOX_TILING.heads
    nc = seq // tile
    c = cum.reshape(batch, seq, LANES)[:, :, :N_HEADS]
    cq = c.reshape(batch, seq, groups, heads).transpose(0, 2, 1, 3)
    ck = c.reshape(batch, nc, tile, groups, heads).transpose(0, 1, 3, 4, 2)
    specs = [
        pl.BlockSpec((1, 1, tile, heads), lambda b, g, i: (b, g, i, 0)),
        pl.BlockSpec((1, nc, 1, heads, tile), lambda b, g, i: (b, 0, g, 0, 0)),
    ]
    pairs = FOX_TILING.pairs
    scratch = [pltpu.VMEM((pairs, nc, tile, 2 * tile), F32),
               pltpu.VMEM((pairs, tile, 2 * LANES), F32),
               pltpu.VMEM((heads, tile, LANES), F32),
               pltpu.VMEM((BLOCK, BLOCK), F32),
               pltpu.VMEM((2 * tile, LANES), BF16)]
    return _causal_attention_call(_fox_kernel, "fox_attention", FOX_TILING, proj, [cq, ck], specs,
                                  scratch, batch, seq)


def kernel(x, g_pre, g_post, w_in_a, w_out_a, sinks_a, w_in_b, w_out_b, w_in_c, b_f_c, w_out_c):
    batch, seq, d = x.shape
    depth = g_pre.shape[0]
    xf = x.reshape(batch * seq, d)
    wa, wb, wc = w_in_a.astype(BF16), w_in_b.astype(BF16), w_in_c.astype(BF16)
    for i in range(depth):
        kind, j = i % 3, i // 3
        if kind == 0:
            proj = _norm_proj(xf, g_pre[i], wa, j, BRANCH + 2 * KV_A + BRANCH, PROJ_TN_A)
            o = _swa_attention(proj, sinks_a[j], batch, seq)
            z_col, w_out = BRANCH + 2 * KV_A, w_out_a[j]
        elif kind == 1:
            proj = _norm_proj(xf, g_pre[i], wb, j, 4 * BRANCH, PROJ_TN)
            o = _stick_attention(proj, batch, seq)
            z_col, w_out = 3 * BRANCH, w_out_b[j]
        else:
            w_f = jnp.pad(wc[j, :, 4 * BRANCH:], ((0, 0), (0, LANES - N_HEADS)))
            proj, f_logit = _norm_proj(xf, g_pre[i], wc, j, 4 * BRANCH, PROJ_TN, w_narrow=w_f)
            b_f = jnp.pad(b_f_c[j], (0, LANES - N_HEADS)).reshape(1, LANES)
            cum = _cum_log_forget(f_logit, b_f, batch, seq)
            o = _fox_attention(proj, cum, batch, seq)
            z_col, w_out = 3 * BRANCH, w_out_c[j]
        xf = _gate_out(o, proj, z_col, w_out.astype(BF16), g_post[i], xf)
    return xf.reshape(batch, seq, d)
```
